```python
import jax, jax.numpy as jnp
from jax import lax
import numpy as np

D_MODEL = 1024
BATCH = 4
SEQ = 8192
DEPTH = 4

HEAD_DIM = 64
N_HEADS_MOBA = 4
N_HEADS_FOX = 4
N_HEADS_MLA = 4
N_HEADS_DIL = 4
W_MOBA = N_HEADS_MOBA * HEAD_DIM
W_FOX = N_HEADS_FOX * HEAD_DIM
W_DIL = N_HEADS_DIL * HEAD_DIM

MOBA_BLOCK = 256
MOBA_TOPK = 3
MOBA_Q_CHUNK = 64
Q_BLOCK = 128

MLA_Q_RANK = 192
MLA_KV_RANK = 128
MLA_NOPE_DIM = 64
MLA_ROPE_DIM = 32
MLA_V_DIM = 64
MLA_Q_UP = N_HEADS_MLA * (MLA_NOPE_DIM + MLA_ROPE_DIM)
MLA_KV_UP = N_HEADS_MLA * (MLA_NOPE_DIM + MLA_V_DIM)
W_MLA = N_HEADS_MLA * MLA_V_DIM

DILATED_CONFIGS = ((128, 1), (512, 4), (2048, 16))
ROPE_THETA = 10000.0
RMS_EPS = 1e-6
D_FF = -(-8 * D_MODEL // (3 * 256)) * 256

IN_SIZES = (3 * W_MOBA, 3 * W_FOX, N_HEADS_FOX, MLA_Q_RANK, MLA_KV_RANK, MLA_ROPE_DIM, 3 * W_DIL)
D_IN = 3 * W_MOBA + 3 * W_FOX + N_HEADS_FOX + MLA_Q_RANK + MLA_KV_RANK + MLA_ROPE_DIM + 3 * W_DIL
SPLIT_POINTS = (
    3 * W_MOBA,
    3 * W_MOBA + 3 * W_FOX,
    3 * W_MOBA + 3 * W_FOX + N_HEADS_FOX,
    3 * W_MOBA + 3 * W_FOX + N_HEADS_FOX + MLA_Q_RANK,
    3 * W_MOBA + 3 * W_FOX + N_HEADS_FOX + MLA_Q_RANK + MLA_KV_RANK,
    3 * W_MOBA + 3 * W_FOX + N_HEADS_FOX + MLA_Q_RANK + MLA_KV_RANK + MLA_ROPE_DIM,
)
MIX_WIDTH = W_MOBA + W_FOX + W_MLA + W_DIL

kernel_name = "hybrid_parallel_heads_decoder"


def rmsnorm(x, g):
    xf = x.astype(jnp.float32)
    y = xf * lax.rsqrt(jnp.mean(xf * xf, axis=-1, keepdims=True) + RMS_EPS)
    return (y * g.astype(jnp.float32)).astype(x.dtype)


def rope_tables(positions, dim):
    inv_freq = ROPE_THETA ** (-jnp.arange(0, dim, 2, dtype=jnp.float32) / dim)
    ang = positions.astype(jnp.float32)[..., None] * inv_freq
    return jnp.cos(ang), jnp.sin(ang)


def apply_rope(x, cos, sin):
    x1, x2 = jnp.split(x.astype(jnp.float32), 2, axis=-1)
    c = cos[:, None]
    s = sin[:, None]
    return jnp.concatenate([x1 * c - x2 * s, x2 * c + x1 * s], axis=-1).astype(x.dtype)


def to_heads(t, n_heads):
    b, s, _ = t.shape
    return t.reshape(b, s, n_heads, -1).transpose(0, 2, 1, 3)


def from_heads(t):
    b, h, s, d = t.shape
    return t.transpose(0, 2, 1, 3).reshape(b, s, h * d)


def pad_seq(t, new_len):
    pad = new_len - t.shape[2]
    return jnp.pad(t, ((0, 0), (0, 0), (0, pad), (0, 0)))


def blocked_causal_attention(q, k, v, scale, cum=None):
    b, h, s, dq = q.shape
    nq = s // Q_BLOCK
    qb = jnp.moveaxis(q.reshape(b, h, nq, Q_BLOCK, dq), 2, 0)
    starts = jnp.arange(nq, dtype=jnp.int32) * Q_BLOCK
    key_pos = jnp.arange(s, dtype=jnp.int32)

    def body(xs):
        if cum is None:
            qi, start = xs
        else:
            qi, start, ci = xs
        sc = jnp.einsum('bhqd,bhkd->bhqk', qi, k, preferred_element_type=jnp.float32) * scale
        if cum is not None:
            sc = sc + (ci[..., None] - cum[:, :, None, :])
        q_pos = start + jnp.arange(Q_BLOCK, dtype=jnp.int32)
        sc = jnp.where(key_pos[None, :] <= q_pos[:, None], sc, -jnp.inf)
        p = jax.nn.softmax(sc, axis=-1)
        return jnp.einsum('bhqk,bhkd->bhqd', p.astype(v.dtype), v)

    if cum is None:
        xs = (qb, starts)
    else:
        xs = (qb, starts, jnp.moveaxis(cum.reshape(b, h, nq, Q_BLOCK), 2, 0))
    out = lax.map(body, xs)
    return jnp.moveaxis(out, 0, 2).reshape(b, h, s, v.shape[-1])


def moba_attention(q, k, v):
    b, h, s, d = q.shape
    sp = -(-s // MOBA_BLOCK) * MOBA_BLOCK
    q, k, v = pad_seq(q, sp), pad_seq(k, sp), pad_seq(v, sp)
    nb = sp // MOBA_BLOCK
    scale = d ** -0.5
    kb = k.reshape(b, h, nb, MOBA_BLOCK, d)
    vb = v.reshape(b, h, nb, MOBA_BLOCK, d)
    k_mean = jnp.mean(kb.astype(jnp.float32), axis=3)
    gate = jnp.einsum('bhsd,bhnd->bhsn', q.astype(jnp.float32), k_mean)
    q_blk = jnp.arange(sp, dtype=jnp.int32) // MOBA_BLOCK
    past = jnp.arange(nb, dtype=jnp.int32)[None, :] < q_blk[:, None]
    gate = jnp.where(past, gate, -jnp.inf)
    k_sel = min(MOBA_TOPK, nb)
    _, sel = lax.top_k(gate, k_sel)
    sel_valid = jnp.arange(k_sel, dtype=jnp.int32)[None, :] < q_blk[:, None]

    nc = sp // MOBA_Q_CHUNK
    qc = jnp.moveaxis(q.reshape(b, h, nc, MOBA_Q_CHUNK, d), 2, 0)
    selc = jnp.moveaxis(sel.reshape(b, h, nc, MOBA_Q_CHUNK, k_sel), 2, 0)
    validc = sel_valid.reshape(nc, MOBA_Q_CHUNK, k_sel)
    starts = jnp.arange(nc, dtype=jnp.int32) * MOBA_Q_CHUNK
    b_idx = jnp.arange(b)[:, None, None, None]
    h_idx = jnp.arange(h)[None, :, None, None]
    n_sel = k_sel * MOBA_BLOCK

    def body(xs):
        qi, si, vi, start = xs
        own = start // MOBA_BLOCK
        k_own = lax.dynamic_index_in_dim(kb, own, axis=2, keepdims=False)
        v_own = lax.dynamic_index_in_dim(vb, own, axis=2, keepdims=False)
        q_pos = start + jnp.arange(MOBA_Q_CHUNK, dtype=jnp.int32)
        k_pos = own * MOBA_BLOCK + jnp.arange(MOBA_BLOCK, dtype=jnp.int32)
        s_own = jnp.einsum('bhqd,bhkd->bhqk', qi, k_own, preferred_element_type=jnp.float32) * scale
        s_own = jnp.where(k_pos[None, :] <= q_pos[:, None], s_own, -jnp.inf)
        k_g = kb[b_idx, h_idx, si]
        v_g = vb[b_idx, h_idx, si]
        s_g = jnp.einsum('bhqd,bhqjkd->bhqjk', qi, k_g, preferred_element_type=jnp.float32) * scale
        s_g = jnp.where(vi[None, None, :, :, None], s_g, -jnp.inf)
        s_all = jnp.concatenate([s_g.reshape(b, h, MOBA_Q_CHUNK, n_sel), s_own], axis=-1)
        p = jax.nn.softmax(s_all, axis=-1).astype(v.dtype)
        p_g = p[..., :n_sel].reshape(b, h, MOBA_Q_CHUNK, k_sel, MOBA_BLOCK)
        p_own = p[..., n_sel:]
        return (jnp.einsum('bhqjk,bhqjkd->bhqd', p_g, v_g)
                + jnp.einsum('bhqk,bhkd->bhqd', p_own, v_own))

    out = lax.map(body, (qc, selc, validc, starts))
    return jnp.moveaxis(out, 0, 2).reshape(b, h, sp, d)[:, :, :s]


def dilated_branch(q, k, v, window, dilation):
    b, h, s, d = q.shape
    L = window // dilation
    seg = dilation * L
    sp = -(-s // seg) * seg
    n = sp // dilation
    nb = n // L
    scale = d ** -0.5

    def to_sub(t):
        t = pad_seq(t, sp).reshape(b, h, n, dilation, d).transpose(0, 1, 3, 2, 4)
        return t.reshape(b, h, dilation, nb, L, d)

    def with_prev(t):
        prev = jnp.concatenate([jnp.zeros_like(t[:, :, :, :1]), t[:, :, :, :-1]], axis=3)
        return jnp.concatenate([prev, t], axis=4)

    qs = to_sub(q)
    k2 = with_prev(to_sub(k))
    v2 = with_prev(to_sub(v))
    sc = jnp.einsum('bhrnqd,bhrnkd->bhrnqk', qs, k2, preferred_element_type=jnp.float32) * scale
    i = jnp.arange(L)[:, None]
    m = jnp.arange(2 * L)[None, :]
    diff = L + i - m
    blk = jnp.arange(nb)[:, None, None]
    mask = (diff >= 0) & (diff <= L) & ((m >= L) | (blk > 0))
    sc = jnp.where(mask, sc, -jnp.inf)
    lse = jax.nn.logsumexp(sc, axis=-1)
    p = jnp.exp(sc - lse[..., None]).astype(v.dtype)
    o = jnp.einsum('bhrnqk,bhrnkd->bhrnqd', p, v2)
    o = o.reshape(b, h, dilation, n, d).transpose(0, 1, 3, 2, 4).reshape(b, h, sp, d)[:, :, :s]
    lse = lse.reshape(b, h, dilation, n).transpose(0, 1, 3, 2).reshape(b, h, sp)[:, :, :s]
    return o, lse


def dilated_mixture(q, k, v):
    outs, lses = [], []
    for window, dilation in DILATED_CONFIGS:
        o, l = dilated_branch(q, k, v, window, dilation)
        outs.append(o.astype(jnp.float32))
        lses.append(l)
    wts = jax.nn.softmax(jnp.stack(lses, axis=0), axis=0)
    out = jnp.sum(wts[..., None] * jnp.stack(outs, axis=0), axis=0)
    return out.astype(q.dtype)


def setup_inputs(seed: int = 0) -> dict:
    key = jax.random.key(seed)
    ks = jax.random.split(key, 16)
    f32 = jnp.float32

    def nrm(k, shape, fan_in):
        return jax.random.normal(k, shape, f32) * (fan_in ** -0.5)

    def gain(k, dim):
        return 1.0 + 0.02 * jax.random.normal(k, (DEPTH, dim), f32)

    x = jax.random.normal(ks[0], (BATCH, SEQ, D_MODEL), f32)
    positions = jnp.broadcast_to(jnp.arange(SEQ, dtype=jnp.int32)[None, :], (BATCH, SEQ))
    return {
        "x": x,
        "positions": positions,
        "w_in": nrm(ks[1], (DEPTH, D_MODEL, D_IN), D_MODEL),
        "b_forget": 0.1 * jax.random.normal(ks[2], (DEPTH, N_HEADS_FOX), f32),
        "g_mla_q": gain(ks[3], MLA_Q_RANK),
        "w_mla_q_up": nrm(ks[4], (DEPTH, MLA_Q_RANK, MLA_Q_UP), MLA_Q_RANK),
        "g_mla_kv": gain(ks[5], MLA_KV_RANK),
        "w_mla_kv_up": nrm(ks[6], (DEPTH, MLA_KV_RANK, MLA_KV_UP), MLA_KV_RANK),
        "w_out": nrm(ks[7], (DEPTH, MIX_WIDTH, D_MODEL), MIX_WIDTH),
        "g_pre_mix": gain(ks[8], D_MODEL),
        "g_post_mix": gain(ks[9], D_MODEL),
        "w_gate": nrm(ks[10], (DEPTH, D_MODEL, D_FF), D_MODEL),
        "w_up": nrm(ks[11], (DEPTH, D_MODEL, D_FF), D_MODEL),
        "w_down": nrm(ks[12], (DEPTH, D_FF, D_MODEL), D_FF),
        "g_pre_ffn": gain(ks[13], D_MODEL),
        "g_post_ffn": gain(ks[14], D_MODEL),
    }


def reference(x, positions, w_in, b_forget, g_mla_q, w_mla_q_up, g_mla_kv, w_mla_kv_up, w_out,
              g_pre_mix, g_post_mix, w_gate, w_up, w_down, g_pre_ffn, g_post_ffn):
    cos_h, sin_h = rope_tables(positions, HEAD_DIM)
    cos_r, sin_r = rope_tables(positions, MLA_ROPE_DIM)
    b, s, _ = x.shape
    for l in range(DEPTH):
        h = rmsnorm(x, g_pre_mix[l])
        z = jnp.einsum('bsd,de->bse', h, w_in[l])
        z_moba, z_fox, z_fg, z_cq, z_ckv, z_kr, z_dil = jnp.split(z, SPLIT_POINTS, axis=-1)

        qa, ka, va = [to_heads(t, N_HEADS_MOBA) for t in jnp.split(z_moba, 3, axis=-1)]
        qa, ka = apply_rope(qa, cos_h, sin_h), apply_rope(ka, cos_h, sin_h)
        o_moba = moba_attention(qa, ka, va)

        qb, kb, vb = [to_heads(t, N_HEADS_FOX) for t in jnp.split(z_fox, 3, axis=-1)]
        log_f = jax.nn.log_sigmoid(z_fg.astype(jnp.float32) + b_forget[l].astype(jnp.float32))
        cum = jnp.cumsum(log_f, axis=1).transpose(0, 2, 1)
        o_fox = blocked_causal_attention(qb, kb, vb, HEAD_DIM ** -0.5, cum)

        c_q = rmsnorm(z_cq, g_mla_q[l])
        q_c = to_heads(jnp.einsum('bsr,re->bse', c_q, w_mla_q_up[l]), N_HEADS_MLA)
        q_nope, q_rope = q_c[..., :MLA_NOPE_DIM], q_c[..., MLA_NOPE_DIM:]
        c_kv = rmsnorm(z_ckv, g_mla_kv[l])
        kv = to_heads(jnp.einsum('bsr,re->bse', c_kv, w_mla_kv_up[l]), N_HEADS_MLA)
        k_nope, v_c = kv[..., :MLA_NOPE_DIM], kv[..., MLA_NOPE_DIM:]
        k_rope = apply_rope(z_kr[:, None], cos_r, sin_r)
        q_full = jnp.concatenate([q_nope, apply_rope(q_rope, cos_r, sin_r)], axis=-1)
        k_full = jnp.concatenate(
            [k_nope, jnp.broadcast_to(k_rope, (b, N_HEADS_MLA, s, MLA_ROPE_DIM))], axis=-1)
        o_mla = blocked_causal_attention(q_full, k_full, v_c, (MLA_NOPE_DIM + MLA_ROPE_DIM) ** -0.5)

        qd, kd, vd = [to_heads(t, N_HEADS_DIL) for t in jnp.split(z_dil, 3, axis=-1)]
        qd, kd = apply_rope(qd, cos_h, sin_h), apply_rope(kd, cos_h, sin_h)
        o_dil = dilated_mixture(qd, kd, vd)

        mix = jnp.concatenate([from_heads(o_moba), from_heads(o_fox), from_heads(o_mla),
                               from_heads(o_dil)], axis=-1)
        y = jnp.einsum('bse,ed->bsd', mix, w_out[l])
        x = x + rmsnorm(y, g_post_mix[l])

        h = rmsnorm(x, g_pre_ffn[l])
        f = jax.nn.silu(jnp.einsum('bsd,df->bsf', h, w_gate[l])) * jnp.einsum('bsd,df->bsf', h, w_up[l])
        f = jnp.einsum('bsf,fd->bsd', f, w_down[l])
        x = x + rmsnorm(f, g_post_ffn[l])
    return x
```

```python
import functools

import numpy as np
import jax
import jax.numpy as jnp
from jax import lax
from jax.experimental import pallas as pl
from jax.experimental.pallas import tpu as pltpu

D_MODEL = 1024
HEAD_DIM = 64
N_HEADS = 4
GROUP_W = N_HEADS * HEAD_DIM
MOBA_BLOCK = 256
MOBA_TOPK = 3
MLA_Q_RANK = 192
MLA_KV_RANK = 128
MLA_NOPE = 64
MLA_ROPE = 32
DIL_WINDOW_STEPS = 128
DILATIONS = (1, 4, 16)
ROPE_THETA = 10000.0
RMS_EPS = 1e-6
D_FF = 2816
FF_CHUNK = 256

ROW_TILE = 512
ATT_TILE = 512
DIL_TILE = 256
VMEM_LIMIT = 56 * 1024 * 1024

_C_KMOBA, _C_KFOX, _C_QDIL, _C_KDIL, _C_VDIL, _C_CQ = 0, 256, 512, 768, 1024, 1280
_C_CKV, _C_X, _C_Y, _N_TOK = 1536, 1664, 1792, 1920
_FG_LANE = 32

_NT = (((1,), (1,)), ((), ()))


def _dot(a, b):
    return jnp.dot(a, b, preferred_element_type=jnp.float32)


def _dot_nt(a, b):
    return lax.dot_general(a, b, _NT, preferred_element_type=jnp.float32)


def _rms_scale(v, n):
    return lax.rsqrt(jnp.sum(v * v, axis=-1, keepdims=True) * (1.0 / n) + RMS_EPS)


def _in_proj_kernel(x_ref, g_ref, wtok_ref, wtr_ref, tri_ref, ct64_ref, st64_ref, cT64_ref, sT64_ref,
                    ct16_ref, st16_ref, cT16_ref, sT16_ref, bfg_ref, gq_ref, wqT_ref, gkv_ref, wkn_ref, wvT_ref,
                    kmoba_ref, kmean_ref, qTmoba_ref, vTmoba_ref, kfox_ref, qTfox_ref, vTfox_ref, cum_ref,
                    qdil_ref, kdil_ref, vdil_ref, kmla_ref, qTmla_ref, vTmla_ref, carry_ref, *, tiles_per_seq):
    bf16 = jnp.bfloat16
    x = x_ref[...]
    hb = (x * _rms_scale(x, D_MODEL) * g_ref[...]).astype(bf16)
    tm = x.shape[0]

    def tok(c0, w):
        return _dot(hb, wtok_ref[:, c0:c0 + w])

    def rope_tok(z):
        c, s = ct64_ref[...], st64_ref[...]
        x1, x2 = z[:, :128], z[:, 128:]
        return jnp.concatenate([x1 * c - x2 * s, x2 * c + x1 * s], axis=1)

    def rope_tr(zT):
        c, s = cT64_ref[...], sT64_ref[...]
        x1, x2 = zT[:128], zT[128:]
        return jnp.concatenate([x1 * c - x2 * s, x2 * c + x1 * s], axis=0)

    k_moba = rope_tok(tok(_C_KMOBA, GROUP_W))
    kmoba_ref[...] = k_moba.astype(bf16)
    for blk in range(tm // MOBA_BLOCK):
        kmean_ref[0, blk:blk + 1, :] = jnp.mean(k_moba[blk * MOBA_BLOCK:(blk + 1) * MOBA_BLOCK], axis=0, keepdims=True)
    qTmoba_ref[...] = rope_tr(_dot_nt(wtr_ref[0:256, :], hb)).astype(bf16)
    vT = _dot_nt(wtr_ref[256:512, :], hb).astype(bf16)
    for blk in range(tm // MOBA_BLOCK):
        vTmoba_ref[blk] = vT[:, blk * MOBA_BLOCK:(blk + 1) * MOBA_BLOCK]

    kfox_ref[...] = tok(_C_KFOX, GROUP_W).astype(bf16)
    qTfox_ref[...] = _dot_nt(wtr_ref[512:768, :], hb).astype(bf16)
    vTfox_ref[0] = _dot_nt(wtr_ref[768:1024, :], hb).astype(bf16)

    qdil_ref[...] = rope_tok(tok(_C_QDIL, GROUP_W)).astype(bf16)
    kdil_ref[...] = rope_tok(tok(_C_KDIL, GROUP_W)).astype(bf16)
    vdil_ref[...] = tok(_C_VDIL, GROUP_W).astype(bf16)

    xblk = tok(_C_X, 128)
    yblk = tok(_C_Y, 128)
    fg = xblk + bfg_ref[...]
    logf = jnp.minimum(fg, 0.0) - jnp.log1p(jnp.exp(-jnp.abs(fg)))
    a1 = logf.astype(bf16)
    r1 = logf - a1.astype(jnp.float32)
    a2 = r1.astype(bf16)
    a3 = (r1 - a2.astype(jnp.float32)).astype(bf16)
    tri = tri_ref[...]

    @pl.when(pl.program_id(0) % tiles_per_seq == 0)
    def _():
        carry_ref[...] = jnp.zeros_like(carry_ref)

    cum = _dot(tri, a1) + _dot(tri, a2) + _dot(tri, a3) + carry_ref[...]
    cum_ref[...] = cum
    carry_ref[...] = cum[tm - 1:tm, :]

    zcq = tok(_C_CQ, 256)
    cq = (zcq * _rms_scale(zcq, MLA_Q_RANK) * gq_ref[...]).astype(bf16)
    qcT = _dot_nt(wqT_ref[...], cq)
    c16, s16 = cT16_ref[...], sT16_ref[...]
    r1h, r2h = qcT[256:320], qcT[320:384]
    qT = jnp.concatenate([qcT[0:256], r1h * c16 - r2h * s16, r2h * c16 + r1h * s16], axis=0)
    qTmla_ref[...] = (qT * ((MLA_NOPE + MLA_ROPE) ** -0.5)).astype(bf16)

    zckv = tok(_C_CKV, MLA_KV_RANK)
    ckv = (zckv * _rms_scale(zckv, MLA_KV_RANK) * gkv_ref[...]).astype(bf16)
    kn = _dot(ckv, wkn_ref[...])
    krope = xblk * ct16_ref[...] + yblk * st16_ref[...]
    kmla_ref[...] = jnp.concatenate([kn[:, :128], krope, kn[:, 128:], krope], axis=1).astype(bf16)
    vTmla_ref[0] = _dot_nt(wvT_ref[...], ckv).astype(bf16)


def _softmax_step(s, vb, m, l, acc):
    m_new = jnp.maximum(m, jnp.max(s, axis=0, keepdims=True))
    p = jnp.exp(s - m_new)
    alpha = jnp.exp(m - m_new)
    l = alpha * l + jnp.sum(p, axis=0, keepdims=True)
    acc = alpha * acc + _dot(vb, p.astype(jnp.bfloat16))
    return m_new, l, acc


def _causal_tile_mask(n):
    key = lax.broadcasted_iota(jnp.int32, (n, n), 0)
    qry = lax.broadcasted_iota(jnp.int32, (n, n), 1)
    return key <= qry


def _init_carry(tq):
    return (jnp.full((1, tq), -jnp.inf, jnp.float32), jnp.zeros((1, tq), jnp.float32),
            jnp.zeros((HEAD_DIM, tq), jnp.float32))


def _fox_kernel(qT_ref, k_ref, vT_ref, cum_ref, o_ref):
    i = pl.program_id(1)
    tq = qT_ref.shape[1]
    q = qT_ref[...]
    row = lax.broadcasted_iota(jnp.int32, q.shape, 0)
    causal = _causal_tile_mask(tq)
    outs = []
    for h in range(N_HEADS):
        qm = jnp.where((row >= h * HEAD_DIM) & (row < (h + 1) * HEAD_DIM), q, jnp.zeros_like(q))
        lane = _FG_LANE + h
        c0 = cum_ref[pl.ds(pl.multiple_of(i * tq, tq), 1), lane:lane + 1]

        def step(j, carry, diag, qm=qm, lane=lane, c0=c0, h=h):
            start = pl.multiple_of(j * tq, tq)
            s = _dot(k_ref[pl.ds(start, tq), :], qm)
            s = s + (c0 - cum_ref[pl.ds(start, tq), lane:lane + 1])
            if diag:
                s = jnp.where(causal, s, -jnp.inf)
            vb = vT_ref[j, h * HEAD_DIM:(h + 1) * HEAD_DIM, :]
            return _softmax_step(s, vb, *carry)

        carry = lax.fori_loop(0, i, functools.partial(step, diag=False), _init_carry(tq))
        _, l, acc = step(i, carry, True)
        outs.append(acc / l)
    o_ref[...] = jnp.concatenate(outs, axis=0).T.astype(o_ref.dtype)


def _mla_kernel(qT_ref, k_ref, vT_ref, o_ref):
    i = pl.program_id(1)
    tq = qT_ref.shape[1]
    causal = _causal_tile_mask(tq)
    zeros64 = jnp.zeros((MLA_NOPE, tq), jnp.bfloat16)
    zeros96 = jnp.zeros((256 - 128 - MLA_ROPE, tq), jnp.bfloat16)
    outs = []
    for h in range(N_HEADS):
        qn = qT_ref[h * MLA_NOPE:(h + 1) * MLA_NOPE, :]
        qr1 = qT_ref[256 + 16 * h:256 + 16 * (h + 1), :]
        qr2 = qT_ref[320 + 16 * h:320 + 16 * (h + 1), :]
        nope = [qn, zeros64] if h % 2 == 0 else [zeros64, qn]
        qm = jnp.concatenate(nope + [qr1, qr2, zeros96], axis=0)
        col0 = 256 * (h // 2)

        def step(j, carry, diag, qm=qm, col0=col0, h=h):
            start = pl.multiple_of(j * tq, tq)
            s = _dot(k_ref[pl.ds(start, tq), col0:col0 + 256], qm)
            if diag:
                s = jnp.where(causal, s, -jnp.inf)
            vb = vT_ref[j, h * HEAD_DIM:(h + 1) * HEAD_DIM, :]
            return _softmax_step(s, vb, *carry)

        carry = lax.fori_loop(0, i, functools.partial(step, diag=False), _init_carry(tq))
        _, l, acc = step(i, carry, True)
        outs.append(acc / l)
    o_ref[...] = jnp.concatenate(outs, axis=0).T.astype(o_ref.dtype)


def _moba_kernel(qT_ref, k_ref, vT_ref, km_ref, o_ref, bias_ref):
    i = pl.program_id(1)
    tq = qT_ref.shape[1]
    nb = km_ref.shape[1]
    q = qT_ref[...]
    row = lax.broadcasted_iota(jnp.int32, q.shape, 0)
    causal = _causal_tile_mask(tq)
    km = km_ref[0].astype(jnp.bfloat16)
    blk = lax.broadcasted_iota(jnp.int32, (nb, tq), 0)
    neg_inf = jnp.float32(-jnp.inf)
    outs = []
    for h in range(N_HEADS):
        in_head = (((row >= 32 * h) & (row < 32 * (h + 1)))
                   | ((row >= 128 + 32 * h) & (row < 128 + 32 * (h + 1))))
        qm = jnp.where(in_head, q, jnp.zeros_like(q))
        g = jnp.where(blk < i, _dot(km, qm), neg_inf)
        sel = jnp.zeros((nb, tq), jnp.bool_)
        for _ in range(MOBA_TOPK):
            mx = jnp.max(g, axis=0, keepdims=True)
            cand = jnp.where((g == mx) & (mx > neg_inf), blk, nb)
            chosen = blk == jnp.min(cand, axis=0, keepdims=True)
            sel = sel | chosen
            g = jnp.where(chosen, neg_inf, g)
        bias_ref[h * nb:(h + 1) * nb, :] = jnp.where(sel, 0.0, neg_inf)

        def vblock(j, h=h):
            return vT_ref[j, h * HEAD_DIM:(h + 1) * HEAD_DIM, :]

        own = pl.multiple_of(i * tq, tq)
        s = jnp.where(causal, _dot(k_ref[pl.ds(own, tq), :], qm), neg_inf)
        carry = _softmax_step(s, vblock(i), *_init_carry(tq))

        def step(j, carry, qm=qm, h=h, vblock=vblock):
            start = pl.multiple_of(j * tq, tq)
            s = _dot(k_ref[pl.ds(start, tq), :], qm) + bias_ref[pl.ds(h * nb + j, 1), :]
            return _softmax_step(s, vblock(j), *carry)

        _, l, acc = lax.fori_loop(0, i, step, carry)
        outs.append(acc / l)
    o_ref[...] = jnp.concatenate(outs, axis=0).T.astype(o_ref.dtype)


def _dilated_kernel(q_ref, k_ref, v_ref, o_ref, lse_ref, *, window):
    i = pl.program_id(2)
    tq = q_ref.shape[1]
    n = k_ref.shape[1]
    q = q_ref[0]
    a = i * tq
    ks = jnp.clip(a - DIL_WINDOW_STEPS, 0, n - window)
    ks = pl.multiple_of(ks, DIL_WINDOW_STEPS)
    kw = k_ref[0, pl.ds(ks, window), :]
    vw = v_ref[0, pl.ds(ks, window), :]
    jq = a + lax.broadcasted_iota(jnp.int32, (tq, window), 0)
    jk = ks + lax.broadcasted_iota(jnp.int32, (tq, window), 1)
    band = (jq - jk >= 0) & (jq - jk <= DIL_WINDOW_STEPS)
    lane = lax.broadcasted_iota(jnp.int32, (tq, GROUP_W), 1)
    o = jnp.zeros((tq, GROUP_W), jnp.float32)
    lse = jnp.zeros((tq, GROUP_W), jnp.float32)
    for h in range(N_HEADS):
        in_head = (((lane >= 32 * h) & (lane < 32 * (h + 1)))
                   | ((lane >= 128 + 32 * h) & (lane < 128 + 32 * (h + 1))))
        qm = jnp.where(in_head, q, jnp.zeros_like(q))
        s = jnp.where(band, _dot_nt(qm, kw), -jnp.inf)
        m = jnp.max(s, axis=1, keepdims=True)
        p = jnp.exp(s - m)
        l = jnp.sum(p, axis=1, keepdims=True)
        oh = _dot(p.astype(jnp.bfloat16), vw) / l
        out_lanes = (lane >= h * HEAD_DIM) & (lane < (h + 1) * HEAD_DIM)
        o = jnp.where(out_lanes, oh, o)
        lse = jnp.where(out_lanes, m + jnp.log(l), lse)
    o_ref[0] = o
    lse_ref[0] = lse


def _out_ffn_kernel(x_ref, omoba_ref, ofox_ref, omla_ref, od1_ref, od4_ref, od16_ref, l1_ref, l4_ref, l16_ref,
                    wout_ref, gpost_ref, gpre_ref, wg_ref, wu_ref, wd_ref, gpf_ref, out_ref, acc_ref):
    bf16 = jnp.bfloat16
    l1, l4, l16 = l1_ref[...], l4_ref[...], l16_ref[...]
    m = jnp.maximum(jnp.maximum(l1, l4), l16)
    e1, e4, e16 = jnp.exp(l1 - m), jnp.exp(l4 - m), jnp.exp(l16 - m)
    odil = (e1 * od1_ref[...] + e4 * od4_ref[...] + e16 * od16_ref[...]) / (e1 + e4 + e16)
    y = (_dot(omoba_ref[...], wout_ref[0]) + _dot(ofox_ref[...], wout_ref[1])
         + _dot(omla_ref[...], wout_ref[2]) + _dot(odil.astype(bf16), wout_ref[3]))
    x1 = x_ref[...] + y * _rms_scale(y, D_MODEL) * gpost_ref[...]
    hb = (x1 * _rms_scale(x1, D_MODEL) * gpre_ref[...]).astype(bf16)
    acc_ref[...] = jnp.zeros_like(acc_ref)

    def chunk(c, _):
        g = _dot(hb, wg_ref[c])
        u = _dot(hb, wu_ref[c])
        f = (g * jax.nn.sigmoid(g) * u).astype(bf16)
        acc_ref[...] += _dot(f, wd_ref[c])
        return 0

    lax.fori_loop(0, wg_ref.shape[0], chunk, 0)
    f = acc_ref[...]
    out_ref[...] = x1 + f * _rms_scale(f, D_MODEL) * gpf_ref[...]


def _rope_tables(positions):
    t = positions.reshape(-1).astype(jnp.float32)

    def tab(dim):
        inv = ROPE_THETA ** (-jnp.arange(0, dim, 2, dtype=jnp.float32) / dim)
        ang = t[:, None] * inv
        return jnp.cos(ang), jnp.sin(ang)

    c64, s64 = tab(HEAD_DIM)
    c16, s16 = tab(MLA_ROPE)
    ct64, st64 = jnp.tile(c64, (1, 4)), jnp.tile(s64, (1, 4))
    pad = jnp.zeros((t.shape[0], 128 - MLA_ROPE), jnp.float32)
    ct16 = jnp.concatenate([c16, c16, pad], axis=1)
    st16 = jnp.concatenate([s16, s16, pad], axis=1)
    cT16, sT16 = jnp.tile(c16, (1, 4)).T, jnp.tile(s16, (1, 4)).T
    return ct64, st64, ct64.T, st64.T, ct16, st16, cT16, sT16


_HALF_PERM = np.array([h * 64 + half * 32 + j for half in (0, 1) for h in range(4) for j in range(32)])
_QROPE_ROWS = np.array([h * 96 + 64 + half * 16 + j for half in (0, 1) for h in range(4) for j in range(16)])
_QNOPE_ROWS = np.array([h * 96 + j for h in range(4) for j in range(64)])
_KNOPE_COLS = np.array([h * 128 + j for h in range(4) for j in range(64)])
_VMLA_COLS = np.array([h * 128 + 64 + j for h in range(4) for j in range(64)])


def _prep_layer_weights(w_in, b_forget, g_mla_q, w_mla_q_up, g_mla_kv, w_mla_kv_up, w_out, w_gate, w_up, w_down):
    bf16 = jnp.bfloat16
    depth = w_in.shape[0]
    sl = lambda a, b: w_in[:, :, a:b]
    scale = HEAD_DIM ** -0.5
    moba_q, moba_k, moba_v = sl(0, 256)[..., _HALF_PERM] * scale, sl(256, 512)[..., _HALF_PERM], sl(512, 768)
    fox_q, fox_k, fox_v = sl(768, 1024) * scale, sl(1024, 1280), sl(1280, 1536)
    fg, cq, ckv, kr = sl(1536, 1540), sl(1540, 1732), sl(1732, 1860), sl(1860, 1892)
    dil_q, dil_k, dil_v = sl(1892, 2148)[..., _HALF_PERM] * scale, sl(2148, 2404)[..., _HALF_PERM], sl(2404, 2660)
    kr_rot = jnp.concatenate([-kr[..., 16:], kr[..., :16]], axis=-1)
    z = lambda n: jnp.zeros((depth, D_MODEL, n), w_in.dtype)
    wtok = jnp.concatenate([moba_k, fox_k, dil_q, dil_k, dil_v, cq, z(256 - MLA_Q_RANK), ckv,
                            kr, fg, z(128 - 36), kr_rot, z(96)], axis=-1).astype(bf16)
    wtr = jnp.swapaxes(jnp.concatenate([moba_q, moba_v, fox_q, fox_v], axis=-1), 1, 2).astype(bf16)
    bfg = jnp.zeros((depth, 1, 128), jnp.float32).at[:, 0, _FG_LANE:_FG_LANE + N_HEADS].set(b_forget)
    gq = jnp.pad(g_mla_q, ((0, 0), (0, 256 - MLA_Q_RANK)))[:, None, :]
    wq_rows = jnp.swapaxes(w_mla_q_up, 1, 2)
    wqT = jnp.concatenate([wq_rows[:, _QNOPE_ROWS], wq_rows[:, _QROPE_ROWS]], axis=1)
    wqT = jnp.pad(wqT, ((0, 0), (0, 0), (0, 256 - MLA_Q_RANK))).astype(bf16)
    gkv = g_mla_kv[:, None, :]
    wkn = w_mla_kv_up[:, :, _KNOPE_COLS].astype(bf16)
    wvT = jnp.swapaxes(w_mla_kv_up[:, :, _VMLA_COLS], 1, 2).astype(bf16)
    wout = w_out.reshape(depth, 4, GROUP_W, D_MODEL).astype(bf16)
    nc = D_FF // FF_CHUNK
    wg = jnp.swapaxes(w_gate.reshape(depth, D_MODEL, nc, FF_CHUNK), 1, 2).astype(bf16)
    wu = jnp.swapaxes(w_up.reshape(depth, D_MODEL, nc, FF_CHUNK), 1, 2).astype(bf16)
    wd = w_down.reshape(depth, nc, FF_CHUNK, D_MODEL).astype(bf16)
    return wtok, wtr, bfg, gq, wqT, gkv, wkn, wvT, wout, wg, wu, wd


def _const_spec(shape):
    return pl.BlockSpec(shape, lambda *_: (0,) * len(shape))


def _params(*sem):
    return pltpu.CompilerParams(dimension_semantics=sem, vmem_limit_bytes=VMEM_LIMIT)


def _in_proj(x2, g, wtok, wtr, tri, tables, bfg, gq, wqT, gkv, wkn, wvT, seq):
    t = x2.shape[0]
    tm = ROW_TILE
    nt = t // tm
    bf16, f32 = jnp.bfloat16, jnp.float32
    ct64, st64, cT64, sT64, ct16, st16, cT16, sT16 = tables
    tok_spec = lambda w: pl.BlockSpec((tm, w), lambda i: (i, 0))
    tr_spec = lambda r: pl.BlockSpec((r, tm), lambda i: (0, i))
    blk3 = lambda n, r, c: pl.BlockSpec((n, r, c), lambda i: (i, 0, 0))
    mb = tm // MOBA_BLOCK
    in_specs = [tok_spec(D_MODEL), _const_spec((1, D_MODEL)), _const_spec(wtok.shape), _const_spec(wtr.shape),
                _const_spec(tri.shape), tok_spec(128), tok_spec(128), tr_spec(128), tr_spec(128),
                tok_spec(128), tok_spec(128), tr_spec(64), tr_spec(64), _const_spec((1, 128)),
                _const_spec((1, 256)), _const_spec(wqT.shape), _const_spec((1, 128)), _const_spec(wkn.shape),
                _const_spec(wvT.shape)]
    out_shape = [
        jax.ShapeDtypeStruct((t, GROUP_W), bf16),
        jax.ShapeDtypeStruct((nt, mb, GROUP_W), f32),
        jax.ShapeDtypeStruct((GROUP_W, t), bf16),
        jax.ShapeDtypeStruct((t // MOBA_BLOCK, GROUP_W, MOBA_BLOCK), bf16),
        jax.ShapeDtypeStruct((t, GROUP_W), bf16),
        jax.ShapeDtypeStruct((GROUP_W, t), bf16),
        jax.ShapeDtypeStruct((nt, GROUP_W, tm), bf16),
        jax.ShapeDtypeStruct((t, 128), f32),
        jax.ShapeDtypeStruct((t, GROUP_W), bf16),
        jax.ShapeDtypeStruct((t, GROUP_W), bf16),
        jax.ShapeDtypeStruct((t, GROUP_W), bf16),
        jax.ShapeDtypeStruct((t, 512), bf16),
        jax.ShapeDtypeStruct((384, t), bf16),
        jax.ShapeDtypeStruct((nt, GROUP_W, tm), bf16),
    ]
    out_specs = [tok_spec(GROUP_W), blk3(1, mb, GROUP_W), tr_spec(GROUP_W), blk3(mb, GROUP_W, MOBA_BLOCK),
                 tok_spec(GROUP_W), tr_spec(GROUP_W), blk3(1, GROUP_W, tm), tok_spec(128),
                 tok_spec(GROUP_W), tok_spec(GROUP_W), tok_spec(GROUP_W), tok_spec(512), tr_spec(384),
                 blk3(1, GROUP_W, tm)]
    return pl.pallas_call(
        functools.partial(_in_proj_kernel, tiles_per_seq=seq // tm),
        grid=(nt,), in_specs=in_specs, out_specs=out_specs, out_shape=out_shape,
        scratch_shapes=[pltpu.VMEM((1, 128), f32)],
        compiler_params=_params("arbitrary"), name="in_proj",
    )(x2, g, wtok, wtr, tri, ct64, st64, cT64, sT64, ct16, st16, cT16, sT16, bfg, gq, wqT, gkv, wkn, wvT)


def _dense_attention(body, name, qT, k, vT, extra, batch, seq):
    tq = ATT_TILE
    nq = seq // tq
    t = batch * seq
    in_specs = [pl.BlockSpec((qT.shape[0], tq), lambda b, i: (0, b * nq + i)),
                pl.BlockSpec((seq, k.shape[1]), lambda b, i: (b, 0)),
                pl.BlockSpec((nq, GROUP_W, tq), lambda b, i: (b, 0, 0))]
    in_specs += [pl.BlockSpec((seq, 128), lambda b, i: (b, 0)) for _ in extra]
    return pl.pallas_call(
        body, grid=(batch, nq), in_specs=in_specs,
        out_specs=pl.BlockSpec((tq, GROUP_W), lambda b, i: (b * nq + i, 0)),
        out_shape=jax.ShapeDtypeStruct((t, GROUP_W), jnp.bfloat16),
        compiler_params=_params("arbitrary", "arbitrary"), name=name,
    )(qT, k, vT, *extra)


def _moba_attention(qT, k, vT, kmean, batch, seq):
    tq = MOBA_BLOCK
    nb = seq // tq
    t = batch * seq
    return pl.pallas_call(
        _moba_kernel, grid=(batch, nb),
        in_specs=[pl.BlockSpec((GROUP_W, tq), lambda b, i: (0, b * nb + i)),
                  pl.BlockSpec((seq, GROUP_W), lambda b, i: (b, 0)),
                  pl.BlockSpec((nb, GROUP_W, tq), lambda b, i: (b, 0, 0)),
                  pl.BlockSpec((1, nb, GROUP_W), lambda b, i: (b, 0, 0))],
        out_specs=pl.BlockSpec((tq, GROUP_W), lambda b, i: (b * nb + i, 0)),
        out_shape=jax.ShapeDtypeStruct((t, GROUP_W), jnp.bfloat16),
        scratch_shapes=[pltpu.VMEM((N_HEADS * nb, tq), jnp.float32)],
        compiler_params=_params("arbitrary", "arbitrary"), name="moba_attention",
    )(qT, k, vT, kmean)


def _dilated_attention(q, k, v, batch, seq, dil):
    n = seq // dil
    tq = min(DIL_TILE, n)
    window = min(tq + DIL_WINDOW_STEPS, n)
    view = lambda a: a.reshape(batch, n, dil * GROUP_W)
    qspec = pl.BlockSpec((1, tq, GROUP_W), lambda b, r, i: (b, i, r))
    kspec = pl.BlockSpec((1, n, GROUP_W), lambda b, r, i: (b, 0, r))
    shape = jax.ShapeDtypeStruct((batch, n, dil * GROUP_W), jnp.float32)
    o, lse = pl.pallas_call(
        functools.partial(_dilated_kernel, window=window), grid=(batch, dil, n // tq),
        in_specs=[qspec, kspec, kspec], out_specs=[qspec, qspec], out_shape=[shape, shape],
        compiler_params=_params("arbitrary", "arbitrary", "arbitrary"), name=f"dilated_attention_d{dil}",
    )(view(q), view(k), view(v))
    return o.reshape(batch * seq, GROUP_W), lse.reshape(batch * seq, GROUP_W)


def _out_ffn(x2, omoba, ofox, omla, dil_outs, wout, gpost, gpre, wg, wu, wd, gpf):
    t = x2.shape[0]
    tm = ROW_TILE
    row = lambda w: pl.BlockSpec((tm, w), lambda i: (i, 0))
    (od1, l1), (od4, l4), (od16, l16) = dil_outs
    in_specs = ([row(D_MODEL)] + [row(GROUP_W)] * 9
                + [_const_spec(wout.shape), _const_spec((1, D_MODEL)), _const_spec((1, D_MODEL)),
                   _const_spec(wg.shape), _const_spec(wu.shape), _const_spec(wd.shape), _const_spec((1, D_MODEL))])
    return pl.pallas_call(
        _out_ffn_kernel, grid=(t // tm,), in_specs=in_specs, out_specs=row(D_MODEL),
        out_shape=jax.ShapeDtypeStruct((t, D_MODEL), jnp.float32),
        scratch_shapes=[pltpu.VMEM((tm, D_MODEL), jnp.float32)],
        compiler_params=_params("arbitrary"), name="out_ffn",
    )(x2, omoba, ofox, omla, od1, od4, od16, l1, l4, l16, wout, gpost, gpre, wg, wu, wd, gpf)


def kernel(x, positions, w_in, b_forget, g_mla_q, w_mla_q_up, g_mla_kv, w_mla_kv_up, w_out, g_pre_mix, g_post_mix, w_gate, w_up, w_down, g_pre_ffn, g_post_ffn):
    batch, seq, _ = x.shape
    depth = w_in.shape[0]
    assert seq % ROW_TILE == 0 and seq % (DILATIONS[-1] * DIL_WINDOW_STEPS) == 0
    tables = _rope_tables(positions)
    wtok, wtr, bfg, gq, wqT, gkv, wkn, wvT, wout, wg, wu, wd = _prep_layer_weights(
        w_in, b_forget, g_mla_q, w_mla_q_up, g_mla_kv, w_mla_kv_up, w_out, w_gate, w_up, w_down)
    tri = jnp.tril(jnp.ones((ROW_TILE, ROW_TILE), jnp.bfloat16))
    x2 = x.reshape(batch * seq, D_MODEL)
    for l in range(depth):
        (k_moba, kmean, qT_moba, vT_moba, k_fox, qT_fox, vT_fox, cum, q_dil, k_dil, v_dil,
         k_mla, qT_mla, vT_mla) = _in_proj(x2, g_pre_mix[l][None], wtok[l], wtr[l], tri, tables, bfg[l], gq[l],
                                          wqT[l], gkv[l], wkn[l], wvT[l], seq)
        o_moba = _moba_attention(qT_moba, k_moba, vT_moba, kmean.reshape(batch, seq // MOBA_BLOCK, GROUP_W),
                                 batch, seq)
        o_fox = _dense_attention(_fox_kernel, "fox_attention", qT_fox, k_fox, vT_fox, [cum], batch, seq)
        o_mla = _dense_attention(_mla_kernel, "mla_attention", qT_mla, k_mla, vT_mla, [], batch, seq)
        dil_outs = [_dilated_attention(q_dil, k_dil, v_dil, batch, seq, d) for d in DILATIONS]
        x2 = _out_ffn(x2, o_moba, o_fox, o_mla, dil_outs, wout[l], g_post_mix[l][None], g_pre_ffn[l][None],
                      wg[l], wu[l], wd[l], g_post_ffn[l][None])
    return x2.reshape(batch, seq, D_MODEL)
```

```python
import functools

import numpy as np
import jax
import jax.numpy as jnp
from jax import lax
from jax.experimental import pallas as pl
from jax.experimental.pallas import tpu as pltpu

D_MODEL = 1024
HEAD_DIM = 64
N_HEADS = 4
GROUP_W = N_HEADS * HEAD_DIM
MOBA_BLOCK = 256
MOBA_TOPK = 3
MLA_Q_RANK = 192
MLA_KV_RANK = 128
MLA_NOPE = 64
MLA_ROPE = 32
DIL_WINDOW_STEPS = 128
DILATIONS = (1, 4, 16)
ROPE_THETA = 10000.0
RMS_EPS = 1e-6
D_FF = 2816
FF_CHUNK = 256
V_ROWS = 80
LOG2E = 1.4426950408889634
MASKED_LOGIT = -1e30

ROW_TILE = 512
ATT_TILE = 256
DIL_TILE = 256
VMEM_LIMIT = 56 * 1024 * 1024

_C_KMOBA, _C_KFOX, _C_QDIL, _C_KDIL, _C_VDIL, _C_CQ = 0, 256, 512, 768, 1024, 1280
_C_CKV, _C_X, _C_Y, _N_TOK = 1536, 1664, 1792, 1920
_FG_LANE = 32

_NT = (((1,), (1,)), ((), ()))


def _dot(a, b):
    return jnp.dot(a, b, preferred_element_type=jnp.float32)


def _dot_nt(a, b):
    return lax.dot_general(a, b, _NT, preferred_element_type=jnp.float32)


def _rms_scale(v, n):
    return lax.rsqrt(jnp.sum(v * v, axis=-1, keepdims=True) * (1.0 / n) + RMS_EPS)


def _in_proj_kernel(x_ref, g_ref, wtok_ref, wtr_ref, tri_ref, ct64_ref, st64_ref, cT64_ref, sT64_ref,
                    ct16_ref, st16_ref, cT16_ref, sT16_ref, bfg_ref, gq_ref, wqT_ref, gkv_ref, wkn_ref, wvT_ref,
                    kmoba_ref, kmean_ref, qTmoba_ref, vTmoba_ref, kfox_ref, qTfox_ref, vTfox_ref, cum_ref,
                    qdil_ref, kdil_ref, vdil_ref, kmla_ref, qTmla_ref, vTmla_ref, carry_ref, *, tiles_per_seq):
    bf16 = jnp.bfloat16
    x = x_ref[...]
    hb = (x * _rms_scale(x, D_MODEL) * g_ref[...]).astype(bf16)
    tm = x.shape[0]

    def tok(c0, w):
        return _dot(hb, wtok_ref[:, c0:c0 + w])

    def rope_tok(z):
        c, s = ct64_ref[...], st64_ref[...]
        x1, x2 = z[:, :128], z[:, 128:]
        return jnp.concatenate([x1 * c - x2 * s, x2 * c + x1 * s], axis=1)

    def with_ones(vT):
        ones = jnp.ones((V_ROWS - HEAD_DIM, tm), bf16)
        parts = []
        for h in range(N_HEADS):
            parts += [vT[h * HEAD_DIM:(h + 1) * HEAD_DIM].astype(bf16), ones]
        return jnp.concatenate(parts, axis=0)

    def rope_tr(zT):
        c, s = cT64_ref[...], sT64_ref[...]
        x1, x2 = zT[:128], zT[128:]
        return jnp.concatenate([x1 * c - x2 * s, x2 * c + x1 * s], axis=0)

    k_moba = rope_tok(tok(_C_KMOBA, GROUP_W))
    kmoba_ref[...] = k_moba.astype(bf16)
    for blk in range(tm // MOBA_BLOCK):
        kmean_ref[0, blk:blk + 1, :] = jnp.mean(k_moba[blk * MOBA_BLOCK:(blk + 1) * MOBA_BLOCK], axis=0, keepdims=True)
    qTmoba_ref[...] = rope_tr(_dot_nt(wtr_ref[0:256, :], hb)).astype(bf16)
    def store_value_blocks(ref, vT):
        for blk in range(tm // ATT_TILE):
            ref[blk] = vT[:, blk * ATT_TILE:(blk + 1) * ATT_TILE]

    store_value_blocks(vTmoba_ref, with_ones(_dot_nt(wtr_ref[256:512, :], hb)))

    kfox_ref[...] = tok(_C_KFOX, GROUP_W).astype(bf16)
    qTfox_ref[...] = _dot_nt(wtr_ref[512:768, :], hb).astype(bf16)
    store_value_blocks(vTfox_ref, with_ones(_dot_nt(wtr_ref[768:1024, :], hb)))

    qdil_ref[...] = rope_tok(tok(_C_QDIL, GROUP_W)).astype(bf16)
    kdil_ref[...] = rope_tok(tok(_C_KDIL, GROUP_W)).astype(bf16)
    vdil_ref[...] = tok(_C_VDIL, GROUP_W).astype(bf16)

    xblk = tok(_C_X, 128)
    yblk = tok(_C_Y, 128)
    fg = xblk + bfg_ref[...]
    logf = jnp.minimum(fg, 0.0) - jnp.log1p(jnp.exp(-jnp.abs(fg)))
    a1 = logf.astype(bf16)
    r1 = logf - a1.astype(jnp.float32)
    a2 = r1.astype(bf16)
    a3 = (r1 - a2.astype(jnp.float32)).astype(bf16)
    tri = tri_ref[...]

    @pl.when(pl.program_id(0) % tiles_per_seq == 0)
    def _():
        carry_ref[...] = jnp.zeros_like(carry_ref)

    cum = _dot(tri, a1) + _dot(tri, a2) + _dot(tri, a3) + carry_ref[...]
    cum_ref[...] = cum
    carry_ref[...] = cum[tm - 1:tm, :]

    zcq = tok(_C_CQ, 256)
    cq = (zcq * _rms_scale(zcq, MLA_Q_RANK) * gq_ref[...]).astype(bf16)
    qcT = _dot_nt(wqT_ref[...], cq)
    c16, s16 = cT16_ref[...], sT16_ref[...]
    r1h, r2h = qcT[256:320], qcT[320:384]
    qT = jnp.concatenate([qcT[0:256], r1h * c16 - r2h * s16, r2h * c16 + r1h * s16], axis=0)
    qTmla_ref[...] = (qT * ((MLA_NOPE + MLA_ROPE) ** -0.5 * LOG2E)).astype(bf16)

    zckv = tok(_C_CKV, MLA_KV_RANK)
    ckv = (zckv * _rms_scale(zckv, MLA_KV_RANK) * gkv_ref[...]).astype(bf16)
    kn = _dot(ckv, wkn_ref[...])
    krope = xblk * ct16_ref[...] + yblk * st16_ref[...]
    kmla_ref[...] = jnp.concatenate([kn[:, :128], krope, kn[:, 128:], krope], axis=1).astype(bf16)
    store_value_blocks(vTmla_ref, with_ones(_dot_nt(wvT_ref[...], ckv)))


def _softmax_update(s, vb, m_ref, acc_ref, h, valid=None):
    m_old = m_ref[h:h + 1, :]
    blk_max = jnp.max(s, axis=0, keepdims=True)
    if valid is not None:
        blk_max = jnp.where(valid, blk_max, -jnp.inf)
    m_new = jnp.maximum(m_old, blk_max)
    shift = m_new if valid is None else jnp.where(valid, m_new, jnp.inf)
    p = jnp.exp2(s - shift)
    acc_ref[h] = jnp.exp2(m_old - m_new) * acc_ref[h] + _dot(vb, p.astype(jnp.bfloat16))
    m_ref[h:h + 1, :] = m_new


def _causal_tile_mask(n):
    key = lax.broadcasted_iota(jnp.int32, (n, n), 0)
    qry = lax.broadcasted_iota(jnp.int32, (n, n), 1)
    return key <= qry


def _init_state(m_ref, acc_ref):
    m_ref[...] = jnp.full(m_ref.shape, -jnp.inf, jnp.float32)
    acc_ref[...] = jnp.zeros(acc_ref.shape, jnp.float32)


def _finalize(acc_ref, o_ref):
    outs = [acc_ref[h, 0:HEAD_DIM, :] / acc_ref[h, HEAD_DIM:HEAD_DIM + 1, :] for h in range(N_HEADS)]
    o_ref[...] = jnp.concatenate(outs, axis=0).T.astype(o_ref.dtype)


def _vblock(vT_ref, j, h):
    return vT_ref[j, h * V_ROWS:(h + 1) * V_ROWS, :]


def _run_key_blocks(i, last, sa_ref, sb_ref, scores_into, consume):
    scores_into(sa_ref, i)
    scores_into(sb_ref, 0)
    consume(sa_ref, i, True, None)

    def pair(p, carry):
        j0 = 2 * p
        scores_into(sa_ref, jnp.minimum(j0 + 1, last))
        consume(sb_ref, j0, False, None)
        scores_into(sb_ref, jnp.minimum(j0 + 2, last))
        consume(sa_ref, jnp.minimum(j0 + 1, last), False, j0 + 1 < i)
        return carry

    lax.fori_loop(0, (i + 1) // 2, pair, 0)


def _key_rows(k_ref, j, tk):
    return k_ref[pl.ds(pl.multiple_of(j * tk, tk), tk), :]


def _fox_kernel(qT_ref, k_ref, vT_ref, cum_ref, o_ref, qcat_ref, m_ref, acc_ref, sa_ref, sb_ref):
    i = pl.program_id(1)
    tq = qT_ref.shape[1]
    q = qT_ref[...]
    row = lax.broadcasted_iota(jnp.int32, q.shape, 0)
    for h in range(N_HEADS):
        in_head = (row >= h * HEAD_DIM) & (row < (h + 1) * HEAD_DIM)
        qcat_ref[:, h * tq:(h + 1) * tq] = jnp.where(in_head, q, jnp.zeros_like(q))
    _init_state(m_ref, acc_ref)
    causal = _causal_tile_mask(tq)
    c0 = cum_ref[pl.ds(pl.multiple_of(i * tq, tq), 1), :]

    def scores_into(s_ref, j):
        s_ref[...] = _dot(_key_rows(k_ref, j, tq), qcat_ref[...])

    def consume(s_ref, j, own, valid):
        bias = (c0 - _key_rows(cum_ref, j, tq)) * LOG2E
        for h in range(N_HEADS):
            s = s_ref[:, h * tq:(h + 1) * tq] + bias[:, _FG_LANE + h:_FG_LANE + h + 1]
            if own:
                s = jnp.where(causal, s, -jnp.inf)
            _softmax_update(s, _vblock(vT_ref, j, h), m_ref, acc_ref, h, valid)

    _run_key_blocks(i, pl.num_programs(1) - 1, sa_ref, sb_ref, scores_into, consume)
    _finalize(acc_ref, o_ref)


def _mla_kernel(qT_ref, k_ref, vT_ref, o_ref, qcat_ref, m_ref, acc_ref, sa_ref, sb_ref):
    i = pl.program_id(1)
    tq = qT_ref.shape[1]
    zeros64 = jnp.zeros((MLA_NOPE, tq), jnp.bfloat16)
    zeros96 = jnp.zeros((256 - 128 - MLA_ROPE, tq), jnp.bfloat16)
    for h in range(N_HEADS):
        qn = qT_ref[h * MLA_NOPE:(h + 1) * MLA_NOPE, :]
        qr1 = qT_ref[256 + 16 * h:256 + 16 * (h + 1), :]
        qr2 = qT_ref[320 + 16 * h:320 + 16 * (h + 1), :]
        nope = [qn, zeros64] if h % 2 == 0 else [zeros64, qn]
        qcat_ref[:, h * tq:(h + 1) * tq] = jnp.concatenate(nope + [qr1, qr2, zeros96], axis=0)
    _init_state(m_ref, acc_ref)
    causal = _causal_tile_mask(tq)

    def scores_into(s_ref, j):
        kb = _key_rows(k_ref, j, tq)
        for half in range(2):
            cols = slice(2 * half * tq, 2 * (half + 1) * tq)
            s_ref[:, cols] = _dot(kb[:, 256 * half:256 * (half + 1)], qcat_ref[:, cols])

    def consume(s_ref, j, own, valid):
        for h in range(N_HEADS):
            s = s_ref[:, h * tq:(h + 1) * tq]
            if own:
                s = jnp.where(causal, s, -jnp.inf)
            _softmax_update(s, _vblock(vT_ref, j, h), m_ref, acc_ref, h, valid)

    _run_key_blocks(i, pl.num_programs(1) - 1, sa_ref, sb_ref, scores_into, consume)
    _finalize(acc_ref, o_ref)


def _moba_kernel(qT_ref, k_ref, vT_ref, km_ref, o_ref, qcat_ref, m_ref, acc_ref, sa_ref, sb_ref, bias_ref):
    i = pl.program_id(1)
    tq = qT_ref.shape[1]
    nb = km_ref.shape[1]
    q = qT_ref[...]
    row = lax.broadcasted_iota(jnp.int32, q.shape, 0)
    km = km_ref[0].astype(jnp.bfloat16)
    blk = lax.broadcasted_iota(jnp.int32, (nb, tq), 0)
    neg_inf = jnp.float32(-jnp.inf)
    for h in range(N_HEADS):
        in_head = (((row >= 32 * h) & (row < 32 * (h + 1)))
                   | ((row >= 128 + 32 * h) & (row < 128 + 32 * (h + 1))))
        qm = jnp.where(in_head, q, jnp.zeros_like(q))
        qcat_ref[:, h * tq:(h + 1) * tq] = qm
        g = jnp.where(blk < i, _dot(km, qm), neg_inf)
        sel = jnp.zeros((nb, tq), jnp.bool_)
        for _ in range(MOBA_TOPK):
            mx = jnp.max(g, axis=0, keepdims=True)
            cand = jnp.where((g == mx) & (mx > neg_inf), blk, nb)
            chosen = blk == jnp.min(cand, axis=0, keepdims=True)
            sel = sel | chosen
            g = jnp.where(chosen, neg_inf, g)
        bias_ref[h * nb:(h + 1) * nb, :] = jnp.where(sel, 0.0, MASKED_LOGIT)
    _init_state(m_ref, acc_ref)
    causal = _causal_tile_mask(tq)

    def scores_into(s_ref, j):
        s_ref[...] = _dot(_key_rows(k_ref, j, tq), qcat_ref[...])

    def consume(s_ref, j, own, valid):
        for h in range(N_HEADS):
            s = s_ref[:, h * tq:(h + 1) * tq]
            if own:
                s = jnp.where(causal, s, neg_inf)
            else:
                s = s + bias_ref[pl.ds(h * nb + j, 1), :]
            _softmax_update(s, _vblock(vT_ref, j, h), m_ref, acc_ref, h, valid)

    _run_key_blocks(i, nb - 1, sa_ref, sb_ref, scores_into, consume)
    _finalize(acc_ref, o_ref)


def _dilated_kernel(q_ref, k_ref, v_ref, o_ref, lse_ref, *, window):
    i = pl.program_id(2)
    tq = q_ref.shape[1]
    n = k_ref.shape[1]
    q = q_ref[0]
    a = i * tq
    ks = jnp.clip(a - DIL_WINDOW_STEPS, 0, n - window)
    ks = pl.multiple_of(ks, DIL_WINDOW_STEPS)
    kw = k_ref[0, pl.ds(ks, window), :]
    vw = v_ref[0, pl.ds(ks, window), :]
    jq = a + lax.broadcasted_iota(jnp.int32, (tq, window), 0)
    jk = ks + lax.broadcasted_iota(jnp.int32, (tq, window), 1)
    band = (jq - jk >= 0) & (jq - jk <= DIL_WINDOW_STEPS)
    lane = lax.broadcasted_iota(jnp.int32, (tq, GROUP_W), 1)
    o = jnp.zeros((tq, GROUP_W), jnp.float32)
    lse = jnp.zeros((tq, GROUP_W), jnp.float32)
    for h in range(N_HEADS):
        in_head = (((lane >= 32 * h) & (lane < 32 * (h + 1)))
                   | ((lane >= 128 + 32 * h) & (lane < 128 + 32 * (h + 1))))
        qm = jnp.where(in_head, q, jnp.zeros_like(q))
        s = jnp.where(band, _dot_nt(qm, kw), -jnp.inf)
        m = jnp.max(s, axis=1, keepdims=True)
        p = jnp.exp2(s - m)
        l = jnp.sum(p, axis=1, keepdims=True)
        oh = _dot(p.astype(jnp.bfloat16), vw) / l
        out_lanes = (lane >= h * HEAD_DIM) & (lane < (h + 1) * HEAD_DIM)
        o = jnp.where(out_lanes, oh, o)
        lse = jnp.where(out_lanes, m + jnp.log2(l), lse)
    o_ref[0] = o
    lse_ref[0] = lse


def _out_ffn_kernel(x_ref, omoba_ref, ofox_ref, omla_ref, od1_ref, od4_ref, od16_ref, l1_ref, l4_ref, l16_ref,
                    wout_ref, gpost_ref, gpre_ref, wg_ref, wu_ref, wd_ref, gpf_ref, out_ref, acc_ref):
    bf16 = jnp.bfloat16
    l1, l4, l16 = l1_ref[...], l4_ref[...], l16_ref[...]
    m = jnp.maximum(jnp.maximum(l1, l4), l16)
    e1, e4, e16 = jnp.exp2(l1 - m), jnp.exp2(l4 - m), jnp.exp2(l16 - m)
    odil = (e1 * od1_ref[...] + e4 * od4_ref[...] + e16 * od16_ref[...]) / (e1 + e4 + e16)
    y = (_dot(omoba_ref[...], wout_ref[0]) + _dot(ofox_ref[...], wout_ref[1])
         + _dot(omla_ref[...], wout_ref[2]) + _dot(odil.astype(bf16), wout_ref[3]))
    x1 = x_ref[...] + y * _rms_scale(y, D_MODEL) * gpost_ref[...]
    hb = (x1 * _rms_scale(x1, D_MODEL) * gpre_ref[...]).astype(bf16)
    acc_ref[...] = jnp.zeros_like(acc_ref)

    def chunk(c, _):
        g = _dot(hb, wg_ref[c])
        u = _dot(hb, wu_ref[c])
        f = (g * jax.nn.sigmoid(g) * u).astype(bf16)
        acc_ref[...] += _dot(f, wd_ref[c])
        return 0

    lax.fori_loop(0, wg_ref.shape[0], chunk, 0)
    f = acc_ref[...]
    out_ref[...] = x1 + f * _rms_scale(f, D_MODEL) * gpf_ref[...]


def _rope_tables(positions):
    t = positions.reshape(-1).astype(jnp.float32)

    def tab(dim):
        inv = ROPE_THETA ** (-jnp.arange(0, dim, 2, dtype=jnp.float32) / dim)
        ang = t[:, None] * inv
        return jnp.cos(ang), jnp.sin(ang)

    c64, s64 = tab(HEAD_DIM)
    c16, s16 = tab(MLA_ROPE)
    ct64, st64 = jnp.tile(c64, (1, 4)), jnp.tile(s64, (1, 4))
    pad = jnp.zeros((t.shape[0], 128 - MLA_ROPE), jnp.float32)
    ct16 = jnp.concatenate([c16, c16, pad], axis=1)
    st16 = jnp.concatenate([s16, s16, pad], axis=1)
    cT16, sT16 = jnp.tile(c16, (1, 4)).T, jnp.tile(s16, (1, 4)).T
    return ct64, st64, ct64.T, st64.T, ct16, st16, cT16, sT16


_HALF_PERM = np.array([h * 64 + half * 32 + j for half in (0, 1) for h in range(4) for j in range(32)])
_QROPE_ROWS = np.array([h * 96 + 64 + half * 16 + j for half in (0, 1) for h in range(4) for j in range(16)])
_QNOPE_ROWS = np.array([h * 96 + j for h in range(4) for j in range(64)])
_KNOPE_COLS = np.array([h * 128 + j for h in range(4) for j in range(64)])
_VMLA_COLS = np.array([h * 128 + 64 + j for h in range(4) for j in range(64)])


def _prep_layer_weights(w_in, b_forget, g_mla_q, w_mla_q_up, g_mla_kv, w_mla_kv_up, w_out, w_gate, w_up, w_down):
    bf16 = jnp.bfloat16
    depth = w_in.shape[0]
    sl = lambda a, b: w_in[:, :, a:b]
    scale = HEAD_DIM ** -0.5 * LOG2E
    moba_q, moba_k, moba_v = sl(0, 256)[..., _HALF_PERM] * scale, sl(256, 512)[..., _HALF_PERM], sl(512, 768)
    fox_q, fox_k, fox_v = sl(768, 1024) * scale, sl(1024, 1280), sl(1280, 1536)
    fg, cq, ckv, kr = sl(1536, 1540), sl(1540, 1732), sl(1732, 1860), sl(1860, 1892)
    dil_q, dil_k, dil_v = sl(1892, 2148)[..., _HALF_PERM] * scale, sl(2148, 2404)[..., _HALF_PERM], sl(2404, 2660)
    kr_rot = jnp.concatenate([-kr[..., 16:], kr[..., :16]], axis=-1)
    z = lambda n: jnp.zeros((depth, D_MODEL, n), w_in.dtype)
    wtok = jnp.concatenate([moba_k, fox_k, dil_q, dil_k, dil_v, cq, z(256 - MLA_Q_RANK), ckv,
                            kr, fg, z(128 - 36), kr_rot, z(96)], axis=-1).astype(bf16)
    wtr = jnp.swapaxes(jnp.concatenate([moba_q, moba_v, fox_q, fox_v], axis=-1), 1, 2).astype(bf16)
    bfg = jnp.zeros((depth, 1, 128), jnp.float32).at[:, 0, _FG_LANE:_FG_LANE + N_HEADS].set(b_forget)
    gq = jnp.pad(g_mla_q, ((0, 0), (0, 256 - MLA_Q_RANK)))[:, None, :]
    wq_rows = jnp.swapaxes(w_mla_q_up, 1, 2)
    wqT = jnp.concatenate([wq_rows[:, _QNOPE_ROWS], wq_rows[:, _QROPE_ROWS]], axis=1)
    wqT = jnp.pad(wqT, ((0, 0), (0, 0), (0, 256 - MLA_Q_RANK))).astype(bf16)
    gkv = g_mla_kv[:, None, :]
    wkn = w_mla_kv_up[:, :, _KNOPE_COLS].astype(bf16)
    wvT = jnp.swapaxes(w_mla_kv_up[:, :, _VMLA_COLS], 1, 2).astype(bf16)
    wout = w_out.reshape(depth, 4, GROUP_W, D_MODEL).astype(bf16)
    nc = D_FF // FF_CHUNK
    wg = jnp.swapaxes(w_gate.reshape(depth, D_MODEL, nc, FF_CHUNK), 1, 2).astype(bf16)
    wu = jnp.swapaxes(w_up.reshape(depth, D_MODEL, nc, FF_CHUNK), 1, 2).astype(bf16)
    wd = w_down.reshape(depth, nc, FF_CHUNK, D_MODEL).astype(bf16)
    return wtok, wtr, bfg, gq, wqT, gkv, wkn, wvT, wout, wg, wu, wd


def _const_spec(shape):
    return pl.BlockSpec(shape, lambda *_: (0,) * len(shape))


def _params(*sem):
    return pltpu.CompilerParams(dimension_semantics=sem, vmem_limit_bytes=VMEM_LIMIT)


def _in_proj(x2, g, wtok, wtr, tri, tables, bfg, gq, wqT, gkv, wkn, wvT, seq):
    t = x2.shape[0]
    tm = ROW_TILE
    nt = t // tm
    bf16, f32 = jnp.bfloat16, jnp.float32
    ct64, st64, cT64, sT64, ct16, st16, cT16, sT16 = tables
    tok_spec = lambda w: pl.BlockSpec((tm, w), lambda i: (i, 0))
    tr_spec = lambda r: pl.BlockSpec((r, tm), lambda i: (0, i))
    blk3 = lambda n, r, c: pl.BlockSpec((n, r, c), lambda i: (i, 0, 0))
    mb = tm // MOBA_BLOCK
    ab = tm // ATT_TILE
    vrows = N_HEADS * V_ROWS
    vt_shape = jax.ShapeDtypeStruct((t // ATT_TILE, vrows, ATT_TILE), bf16)
    in_specs = [tok_spec(D_MODEL), _const_spec((1, D_MODEL)), _const_spec(wtok.shape), _const_spec(wtr.shape),
                _const_spec(tri.shape), tok_spec(128), tok_spec(128), tr_spec(128), tr_spec(128),
                tok_spec(128), tok_spec(128), tr_spec(64), tr_spec(64), _const_spec((1, 128)),
                _const_spec((1, 256)), _const_spec(wqT.shape), _const_spec((1, 128)), _const_spec(wkn.shape),
                _const_spec(wvT.shape)]
    out_shape = [
        jax.ShapeDtypeStruct((t, GROUP_W), bf16),
        jax.ShapeDtypeStruct((nt, mb, GROUP_W), f32),
        jax.ShapeDtypeStruct((GROUP_W, t), bf16),
        vt_shape,
        jax.ShapeDtypeStruct((t, GROUP_W), bf16),
        jax.ShapeDtypeStruct((GROUP_W, t), bf16),
        vt_shape,
        jax.ShapeDtypeStruct((t, 128), f32),
        jax.ShapeDtypeStruct((t, GROUP_W), bf16),
        jax.ShapeDtypeStruct((t, GROUP_W), bf16),
        jax.ShapeDtypeStruct((t, GROUP_W), bf16),
        jax.ShapeDtypeStruct((t, 512), bf16),
        jax.ShapeDtypeStruct((384, t), bf16),
        vt_shape,
    ]
    vt_spec = blk3(ab, vrows, ATT_TILE)
    out_specs = [tok_spec(GROUP_W), blk3(1, mb, GROUP_W), tr_spec(GROUP_W), vt_spec,
                 tok_spec(GROUP_W), tr_spec(GROUP_W), vt_spec, tok_spec(128),
                 tok_spec(GROUP_W), tok_spec(GROUP_W), tok_spec(GROUP_W), tok_spec(512), tr_spec(384),
                 vt_spec]
    return pl.pallas_call(
        functools.partial(_in_proj_kernel, tiles_per_seq=seq // tm),
        grid=(nt,), in_specs=in_specs, out_specs=out_specs, out_shape=out_shape,
        scratch_shapes=[pltpu.VMEM((1, 128), f32)],
        compiler_params=_params("arbitrary"), name="in_proj",
    )(x2, g, wtok, wtr, tri, ct64, st64, cT64, sT64, ct16, st16, cT16, sT16, bfg, gq, wqT, gkv, wkn, wvT)


def _attention_scratch(tq):
    return [pltpu.VMEM((GROUP_W, N_HEADS * tq), jnp.bfloat16),
            pltpu.VMEM((8, tq), jnp.float32),
            pltpu.VMEM((N_HEADS, V_ROWS, tq), jnp.float32),
            pltpu.VMEM((tq, N_HEADS * tq), jnp.float32),
            pltpu.VMEM((tq, N_HEADS * tq), jnp.float32)]


def _dense_attention(body, name, qT, k, vT, extra, batch, seq):
    tq = ATT_TILE
    nq = seq // tq
    t = batch * seq
    in_specs = [pl.BlockSpec((qT.shape[0], tq), lambda b, i: (0, b * nq + i)),
                pl.BlockSpec((seq, k.shape[1]), lambda b, i: (b, 0)),
                pl.BlockSpec((nq, N_HEADS * V_ROWS, tq), lambda b, i: (b, 0, 0))]
    in_specs += [pl.BlockSpec((seq, 128), lambda b, i: (b, 0)) for _ in extra]
    return pl.pallas_call(
        body, grid=(batch, nq), in_specs=in_specs,
        out_specs=pl.BlockSpec((tq, GROUP_W), lambda b, i: (b * nq + i, 0)),
        out_shape=jax.ShapeDtypeStruct((t, GROUP_W), jnp.bfloat16),
        scratch_shapes=_attention_scratch(tq),
        compiler_params=_params("arbitrary", "arbitrary"), name=name,
    )(qT, k, vT, *extra)


def _moba_attention(qT, k, vT, kmean, batch, seq):
    tq = MOBA_BLOCK
    nb = seq // tq
    t = batch * seq
    return pl.pallas_call(
        _moba_kernel, grid=(batch, nb),
        in_specs=[pl.BlockSpec((GROUP_W, tq), lambda b, i: (0, b * nb + i)),
                  pl.BlockSpec((seq, GROUP_W), lambda b, i: (b, 0)),
                  pl.BlockSpec((nb, N_HEADS * V_ROWS, tq), lambda b, i: (b, 0, 0)),
                  pl.BlockSpec((1, nb, GROUP_W), lambda b, i: (b, 0, 0))],
        out_specs=pl.BlockSpec((tq, GROUP_W), lambda b, i: (b * nb + i, 0)),
        out_shape=jax.ShapeDtypeStruct((t, GROUP_W), jnp.bfloat16),
        scratch_shapes=_attention_scratch(tq) + [pltpu.VMEM((N_HEADS * nb, tq), jnp.float32)],
        compiler_params=_params("arbitrary", "arbitrary"), name="moba_attention",
    )(qT, k, vT, kmean)


def _dilated_attention(q, k, v, batch, seq, dil):
    n = seq // dil
    tq = min(DIL_TILE, n)
    window = min(tq + DIL_WINDOW_STEPS, n)
    view = lambda a: a.reshape(batch, n, dil * GROUP_W)
    qspec = pl.BlockSpec((1, tq, GROUP_W), lambda b, r, i: (b, i, r))
    kspec = pl.BlockSpec((1, n, GROUP_W), lambda b, r, i: (b, 0, r))
    shape = jax.ShapeDtypeStruct((batch, n, dil * GROUP_W), jnp.float32)
    o, lse = pl.pallas_call(
        functools.partial(_dilated_kernel, window=window), grid=(batch, dil, n // tq),
        in_specs=[qspec, kspec, kspec], out_specs=[qspec, qspec], out_shape=[shape, shape],
        compiler_params=_params("arbitrary", "arbitrary", "arbitrary"), name=f"dilated_attention_d{dil}",
    )(view(q), view(k), view(v))
    return o.reshape(batch * seq, GROUP_W), lse.reshape(batch * seq, GROUP_W)


def _out_ffn(x2, omoba, ofox, omla, dil_outs, wout, gpost, gpre, wg, wu, wd, gpf):
    t = x2.shape[0]
    tm = ROW_TILE
    row = lambda w: pl.BlockSpec((tm, w), lambda i: (i, 0))
    (od1, l1), (od4, l4), (od16, l16) = dil_outs
    in_specs = ([row(D_MODEL)] + [row(GROUP_W)] * 9
                + [_const_spec(wout.shape), _const_spec((1, D_MODEL)), _const_spec((1, D_MODEL)),
                   _const_spec(wg.shape), _const_spec(wu.shape), _const_spec(wd.shape), _const_spec((1, D_MODEL))])
    return pl.pallas_call(
        _out_ffn_kernel, grid=(t // tm,), in_specs=in_specs, out_specs=row(D_MODEL),
        out_shape=jax.ShapeDtypeStruct((t, D_MODEL), jnp.float32),
        scratch_shapes=[pltpu.VMEM((tm, D_MODEL), jnp.float32)],
        compiler_params=_params("arbitrary"), name="out_ffn",
    )(x2, omoba, ofox, omla, od1, od4, od16, l1, l4, l16, wout, gpost, gpre, wg, wu, wd, gpf)


def kernel(x, positions, w_in, b_forget, g_mla_q, w_mla_q_up, g_mla_kv, w_mla_kv_up, w_out, g_pre_mix, g_post_mix, w_gate, w_up, w_down, g_pre_ffn, g_post_ffn):
    batch, seq, _ = x.shape
    depth = w_in.shape[0]
    assert seq % ROW_TILE == 0 and seq % (DILATIONS[-1] * DIL_WINDOW_STEPS) == 0
    tables = _rope_tables(positions)
    wtok, wtr, bfg, gq, wqT, gkv, wkn, wvT, wout, wg, wu, wd = _prep_layer_weights(
        w_in, b_forget, g_mla_q, w_mla_q_up, g_mla_kv, w_mla_kv_up, w_out, w_gate, w_up, w_down)
    tri = jnp.tril(jnp.ones((ROW_TILE, ROW_TILE), jnp.bfloat16))
    x2 = x.reshape(batch * seq, D_MODEL)
    for l in range(depth):
        (k_moba, kmean, qT_moba, vT_moba, k_fox, qT_fox, vT_fox, cum, q_dil, k_dil, v_dil,
         k_mla, qT_mla, vT_mla) = _in_proj(x2, g_pre_mix[l][None], wtok[l], wtr[l], tri, tables, bfg[l], gq[l],
                                          wqT[l], gkv[l], wkn[l], wvT[l], seq)
        o_moba = _moba_attention(qT_moba, k_moba, vT_moba, kmean.reshape(batch, seq // MOBA_BLOCK, GROUP_W),
                                 batch, seq)
        o_fox = _dense_attention(_fox_kernel, "fox_attention", qT_fox, k_fox, vT_fox, [cum], batch, seq)
        o_mla = _dense_attention(_mla_kernel, "mla_attention", qT_mla, k_mla, vT_mla, [], batch, seq)
        dil_outs = [_dilated_attention(q_dil, k_dil, v_dil, batch, seq, d) for d in DILATIONS]
        x2 = _out_ffn(x2, o_moba, o_fox, o_mla, dil_outs, wout[l], g_post_mix[l][None], g_pre_ffn[l][None],
                      wg[l], wu[l], wd[l], g_post_ffn[l][None])
    return x2.reshape(batch, seq, D_MODEL)
```

```python
import functools

import numpy as np
import jax
import jax.numpy as jnp
from jax import lax
from jax.experimental import pallas as pl
from jax.experimental.pallas import tpu as pltpu

D_MODEL = 1024
HEAD_DIM = 64
N_HEADS = 4
GROUP_W = N_HEADS * HEAD_DIM
MOBA_BLOCK = 256
MOBA_TOPK = 3
MLA_Q_RANK = 192
MLA_KV_RANK = 128
MLA_NOPE = 64
MLA_ROPE = 32
DIL_WINDOW_STEPS = 128
DILATIONS = (1, 4, 16)
ROPE_THETA = 10000.0
RMS_EPS = 1e-6
D_FF = 2816
FF_CHUNK = 256
V_ROWS = 80
LOG2E = 1.4426950408889634
MASKED_LOGIT = -1e30

ROW_TILE = 512
ATT_TILE = 256
KEY_BLOCK = 512
DIL_TILE = 256
VMEM_LIMIT = 56 * 1024 * 1024

_C_KMOBA, _C_KFOX, _C_QDIL, _C_KDIL, _C_VDIL, _C_CQ = 0, 256, 512, 768, 1024, 1280
_C_CKV, _C_X, _C_Y, _N_TOK = 1536, 1664, 1792, 1920
_FG_LANE = 32

_NT = (((1,), (1,)), ((), ()))


def _dot(a, b):
    return jnp.dot(a, b, preferred_element_type=jnp.float32)


def _dot_nt(a, b):
    return lax.dot_general(a, b, _NT, preferred_element_type=jnp.float32)


def _rms_scale(v, n):
    return lax.rsqrt(jnp.sum(v * v, axis=-1, keepdims=True) * (1.0 / n) + RMS_EPS)


def _in_proj_kernel(x_ref, g_ref, wtok_ref, wtr_ref, tri_ref, ct64_ref, st64_ref, cT64_ref, sT64_ref,
                    ct16_ref, st16_ref, cT16_ref, sT16_ref, bfg_ref, gq_ref, wqT_ref, gkv_ref, wkn_ref, wvT_ref,
                    kmoba_ref, kmean_ref, qTmoba_ref, vTmoba_ref, kfox_ref, qTfox_ref, vTfox_ref, cum_ref,
                    qdil_ref, kdil_ref, vdil_ref, kmla_ref, qTmla_ref, vTmla_ref, carry_ref, *, tiles_per_seq):
    bf16 = jnp.bfloat16
    x = x_ref[...]
    hb = (x * _rms_scale(x, D_MODEL) * g_ref[...]).astype(bf16)
    tm = x.shape[0]

    def tok(c0, w):
        return _dot(hb, wtok_ref[:, c0:c0 + w])

    def rope_tok(z):
        c, s = ct64_ref[...], st64_ref[...]
        x1, x2 = z[:, :128], z[:, 128:]
        return jnp.concatenate([x1 * c - x2 * s, x2 * c + x1 * s], axis=1)

    def with_ones(vT):
        ones = jnp.ones((V_ROWS - HEAD_DIM, tm), bf16)
        parts = []
        for h in range(N_HEADS):
            parts += [vT[h * HEAD_DIM:(h + 1) * HEAD_DIM].astype(bf16), ones]
        return jnp.concatenate(parts, axis=0)

    def rope_tr(zT):
        c, s = cT64_ref[...], sT64_ref[...]
        x1, x2 = zT[:128], zT[128:]
        return jnp.concatenate([x1 * c - x2 * s, x2 * c + x1 * s], axis=0)

    k_moba = rope_tok(tok(_C_KMOBA, GROUP_W))
    kmoba_ref[...] = k_moba.astype(bf16)
    for blk in range(tm // MOBA_BLOCK):
        kmean_ref[0, blk:blk + 1, :] = jnp.mean(k_moba[blk * MOBA_BLOCK:(blk + 1) * MOBA_BLOCK], axis=0, keepdims=True)
    qTmoba_ref[...] = rope_tr(_dot_nt(wtr_ref[0:256, :], hb)).astype(bf16)
    def store_value_blocks(ref, vT):
        for blk in range(tm // KEY_BLOCK):
            ref[blk] = vT[:, blk * KEY_BLOCK:(blk + 1) * KEY_BLOCK]

    store_value_blocks(vTmoba_ref, with_ones(_dot_nt(wtr_ref[256:512, :], hb)))

    kfox_ref[...] = tok(_C_KFOX, GROUP_W).astype(bf16)
    qTfox_ref[...] = _dot_nt(wtr_ref[512:768, :], hb).astype(bf16)
    store_value_blocks(vTfox_ref, with_ones(_dot_nt(wtr_ref[768:1024, :], hb)))

    qdil_ref[...] = rope_tok(tok(_C_QDIL, GROUP_W)).astype(bf16)
    kdil_ref[...] = rope_tok(tok(_C_KDIL, GROUP_W)).astype(bf16)
    vdil_ref[...] = tok(_C_VDIL, GROUP_W).astype(bf16)

    xblk = tok(_C_X, 128)
    yblk = tok(_C_Y, 128)
    fg = xblk + bfg_ref[...]
    logf = jnp.minimum(fg, 0.0) - jnp.log1p(jnp.exp(-jnp.abs(fg)))
    a1 = logf.astype(bf16)
    r1 = logf - a1.astype(jnp.float32)
    a2 = r1.astype(bf16)
    a3 = (r1 - a2.astype(jnp.float32)).astype(bf16)
    tri = tri_ref[...]

    @pl.when(pl.program_id(0) % tiles_per_seq == 0)
    def _():
        carry_ref[...] = jnp.zeros_like(carry_ref)

    cum = _dot(tri, a1) + _dot(tri, a2) + _dot(tri, a3) + carry_ref[...]
    cum_ref[...] = cum
    carry_ref[...] = cum[tm - 1:tm, :]

    zcq = tok(_C_CQ, 256)
    cq = (zcq * _rms_scale(zcq, MLA_Q_RANK) * gq_ref[...]).astype(bf16)
    qcT = _dot_nt(wqT_ref[...], cq)
    c16, s16 = cT16_ref[...], sT16_ref[...]
    r1h, r2h = qcT[256:320], qcT[320:384]
    qT = jnp.concatenate([qcT[0:256], r1h * c16 - r2h * s16, r2h * c16 + r1h * s16], axis=0)
    qTmla_ref[...] = (qT * ((MLA_NOPE + MLA_ROPE) ** -0.5 * LOG2E)).astype(bf16)

    zckv = tok(_C_CKV, MLA_KV_RANK)
    ckv = (zckv * _rms_scale(zckv, MLA_KV_RANK) * gkv_ref[...]).astype(bf16)
    kn = _dot(ckv, wkn_ref[...])
    krope = xblk * ct16_ref[...] + yblk * st16_ref[...]
    kmla_ref[...] = jnp.concatenate([kn[:, :128], krope, kn[:, 128:], krope], axis=1).astype(bf16)
    store_value_blocks(vTmla_ref, with_ones(_dot_nt(wvT_ref[...], ckv)))


def _softmax_update(s, vb, m_ref, acc_ref, h, valid=None):
    m_old = m_ref[h:h + 1, :]
    blk_max = jnp.max(s, axis=0, keepdims=True)
    if valid is not None:
        blk_max = jnp.where(valid, blk_max, -jnp.inf)
    m_new = jnp.maximum(m_old, blk_max)
    shift = m_new if valid is None else jnp.where(valid, m_new, jnp.inf)
    p = jnp.exp2(s - shift)
    acc_ref[h] = jnp.exp2(m_old - m_new) * acc_ref[h] + _dot(vb, p.astype(jnp.bfloat16))
    m_ref[h:h + 1, :] = m_new


def _own_block_mask(i, tq, tk):
    key = lax.broadcasted_iota(jnp.int32, (tk, tq), 0)
    qry = lax.broadcasted_iota(jnp.int32, (tk, tq), 1) + (i % (tk // tq)) * tq
    return key <= qry


def _init_state(m_ref, acc_ref):
    m_ref[...] = jnp.full(m_ref.shape, -jnp.inf, jnp.float32)
    acc_ref[...] = jnp.zeros(acc_ref.shape, jnp.float32)


def _finalize(acc_ref, o_ref):
    outs = [acc_ref[h, 0:HEAD_DIM, :] / acc_ref[h, HEAD_DIM:HEAD_DIM + 1, :] for h in range(N_HEADS)]
    o_ref[...] = jnp.concatenate(outs, axis=0).T.astype(o_ref.dtype)


def _vblock(vT_ref, j, h):
    return vT_ref[j, h * V_ROWS:(h + 1) * V_ROWS, :]


def _run_key_blocks(own, last, sa_ref, sb_ref, scores_into, consume):
    scores_into(sa_ref, own)
    scores_into(sb_ref, 0)
    consume(sa_ref, own, True, None)

    def pair(p, carry):
        j0 = 2 * p
        scores_into(sa_ref, jnp.minimum(j0 + 1, last))
        consume(sb_ref, j0, False, None)
        scores_into(sb_ref, jnp.minimum(j0 + 2, last))
        consume(sa_ref, jnp.minimum(j0 + 1, last), False, j0 + 1 < own)
        return carry

    lax.fori_loop(0, (own + 1) // 2, pair, 0)


def _key_rows(k_ref, j, tk):
    return k_ref[pl.ds(pl.multiple_of(j * tk, tk), tk), :]


def _fox_kernel(qT_ref, k_ref, vT_ref, cum_ref, o_ref, qcat_ref, m_ref, acc_ref, sa_ref, sb_ref):
    i = pl.program_id(1)
    tq = qT_ref.shape[1]
    q = qT_ref[...]
    row = lax.broadcasted_iota(jnp.int32, q.shape, 0)
    for h in range(N_HEADS):
        in_head = (row >= h * HEAD_DIM) & (row < (h + 1) * HEAD_DIM)
        qcat_ref[:, h * tq:(h + 1) * tq] = jnp.where(in_head, q, jnp.zeros_like(q))
    _init_state(m_ref, acc_ref)
    tk = sa_ref.shape[0]
    causal = _own_block_mask(i, tq, tk)
    c0 = cum_ref[pl.ds(pl.multiple_of(i * tq, tq), 1), :]

    def scores_into(s_ref, j):
        s_ref[...] = _dot(_key_rows(k_ref, j, tk), qcat_ref[...])

    def consume(s_ref, j, own, valid):
        bias = (c0 - _key_rows(cum_ref, j, tk)) * LOG2E
        for h in range(N_HEADS):
            s = s_ref[:, h * tq:(h + 1) * tq] + bias[:, _FG_LANE + h:_FG_LANE + h + 1]
            if own:
                s = jnp.where(causal, s, -jnp.inf)
            _softmax_update(s, _vblock(vT_ref, j, h), m_ref, acc_ref, h, valid)

    _run_key_blocks(i * tq // tk, k_ref.shape[0] // tk - 1, sa_ref, sb_ref, scores_into, consume)
    _finalize(acc_ref, o_ref)


def _mla_kernel(qT_ref, k_ref, vT_ref, o_ref, qcat_ref, m_ref, acc_ref, sa_ref, sb_ref):
    i = pl.program_id(1)
    tq = qT_ref.shape[1]
    zeros64 = jnp.zeros((MLA_NOPE, tq), jnp.bfloat16)
    zeros96 = jnp.zeros((256 - 128 - MLA_ROPE, tq), jnp.bfloat16)
    for h in range(N_HEADS):
        qn = qT_ref[h * MLA_NOPE:(h + 1) * MLA_NOPE, :]
        qr1 = qT_ref[256 + 16 * h:256 + 16 * (h + 1), :]
        qr2 = qT_ref[320 + 16 * h:320 + 16 * (h + 1), :]
        nope = [qn, zeros64] if h % 2 == 0 else [zeros64, qn]
        qcat_ref[:, h * tq:(h + 1) * tq] = jnp.concatenate(nope + [qr1, qr2, zeros96], axis=0)
    _init_state(m_ref, acc_ref)
    tk = sa_ref.shape[0]
    causal = _own_block_mask(i, tq, tk)

    def scores_into(s_ref, j):
        kb = _key_rows(k_ref, j, tk)
        for half in range(2):
            cols = slice(2 * half * tq, 2 * (half + 1) * tq)
            s_ref[:, cols] = _dot(kb[:, 256 * half:256 * (half + 1)], qcat_ref[:, cols])

    def consume(s_ref, j, own, valid):
        for h in range(N_HEADS):
            s = s_ref[:, h * tq:(h + 1) * tq]
            if own:
                s = jnp.where(causal, s, -jnp.inf)
            _softmax_update(s, _vblock(vT_ref, j, h), m_ref, acc_ref, h, valid)

    _run_key_blocks(i * tq // tk, k_ref.shape[0] // tk - 1, sa_ref, sb_ref, scores_into, consume)
    _finalize(acc_ref, o_ref)


def _moba_kernel(qT_ref, k_ref, vT_ref, km_ref, o_ref, qcat_ref, m_ref, acc_ref, sa_ref, sb_ref, bias_ref):
    i = pl.program_id(1)
    tq = qT_ref.shape[1]
    nb = km_ref.shape[1]
    q = qT_ref[...]
    row = lax.broadcasted_iota(jnp.int32, q.shape, 0)
    km = km_ref[0].astype(jnp.bfloat16)
    blk = lax.broadcasted_iota(jnp.int32, (nb, tq), 0)
    neg_inf = jnp.float32(-jnp.inf)
    for h in range(N_HEADS):
        in_head = (((row >= 32 * h) & (row < 32 * (h + 1)))
                   | ((row >= 128 + 32 * h) & (row < 128 + 32 * (h + 1))))
        qm = jnp.where(in_head, q, jnp.zeros_like(q))
        qcat_ref[:, h * tq:(h + 1) * tq] = qm
        g = jnp.where(blk < i, _dot(km, qm), neg_inf)
        sel = jnp.zeros((nb, tq), jnp.bool_)
        for _ in range(MOBA_TOPK):
            mx = jnp.max(g, axis=0, keepdims=True)
            cand = jnp.where((g == mx) & (mx > neg_inf), blk, nb)
            chosen = blk == jnp.min(cand, axis=0, keepdims=True)
            sel = sel | chosen
            g = jnp.where(chosen, neg_inf, g)
        bias_ref[h * nb:(h + 1) * nb, :] = jnp.where(sel, 0.0, MASKED_LOGIT)
    _init_state(m_ref, acc_ref)
    tk = sa_ref.shape[0]
    per = tk // tq
    causal = _own_block_mask(i, tq, tk)

    def scores_into(s_ref, j):
        s_ref[...] = _dot(_key_rows(k_ref, j, tk), qcat_ref[...])

    def consume(s_ref, j, own, valid):
        for h in range(N_HEADS):
            parts = []
            for sub in range(per):
                n = per * j + sub
                gate = bias_ref[pl.ds(h * nb + n, 1), :]
                if own:
                    gate = jnp.where(n < i, gate, 0.0)
                parts.append(s_ref[sub * tq:(sub + 1) * tq, h * tq:(h + 1) * tq] + gate)
            s = parts[0] if per == 1 else jnp.concatenate(parts, axis=0)
            if own:
                s = jnp.where(causal, s, neg_inf)
            _softmax_update(s, _vblock(vT_ref, j, h), m_ref, acc_ref, h, valid)

    _run_key_blocks(i // per, nb // per - 1, sa_ref, sb_ref, scores_into, consume)
    _finalize(acc_ref, o_ref)


def _dilated_kernel(q_ref, k_ref, v_ref, o_ref, lse_ref, *, window):
    i = pl.program_id(2)
    tq = q_ref.shape[1]
    n = k_ref.shape[1]
    q = q_ref[0]
    a = i * tq
    ks = jnp.clip(a - DIL_WINDOW_STEPS, 0, n - window)
    ks = pl.multiple_of(ks, DIL_WINDOW_STEPS)
    kw = k_ref[0, pl.ds(ks, window), :]
    vw = v_ref[0, pl.ds(ks, window), :]
    jq = a + lax.broadcasted_iota(jnp.int32, (tq, window), 0)
    jk = ks + lax.broadcasted_iota(jnp.int32, (tq, window), 1)
    band = (jq - jk >= 0) & (jq - jk <= DIL_WINDOW_STEPS)
    lane = lax.broadcasted_iota(jnp.int32, (tq, GROUP_W), 1)
    o = jnp.zeros((tq, GROUP_W), jnp.float32)
    lse = jnp.zeros((tq, GROUP_W), jnp.float32)
    for h in range(N_HEADS):
        in_head = (((lane >= 32 * h) & (lane < 32 * (h + 1)))
                   | ((lane >= 128 + 32 * h) & (lane < 128 + 32 * (h + 1))))
        qm = jnp.where(in_head, q, jnp.zeros_like(q))
        s = jnp.where(band, _dot_nt(qm, kw), -jnp.inf)
        m = jnp.max(s, axis=1, keepdims=True)
        p = jnp.exp2(s - m)
        l = jnp.sum(p, axis=1, keepdims=True)
        oh = _dot(p.astype(jnp.bfloat16), vw) / l
        out_lanes = (lane >= h * HEAD_DIM) & (lane < (h + 1) * HEAD_DIM)
        o = jnp.where(out_lanes, oh, o)
        lse = jnp.where(out_lanes, m + jnp.log2(l), lse)
    o_ref[0] = o
    lse_ref[0] = lse


def _out_ffn_kernel(x_ref, omoba_ref, ofox_ref, omla_ref, od1_ref, od4_ref, od16_ref, l1_ref, l4_ref, l16_ref,
                    wout_ref, gpost_ref, gpre_ref, wg_ref, wu_ref, wd_ref, gpf_ref, out_ref, acc_ref):
    bf16 = jnp.bfloat16
    l1, l4, l16 = l1_ref[...], l4_ref[...], l16_ref[...]
    m = jnp.maximum(jnp.maximum(l1, l4), l16)
    e1, e4, e16 = jnp.exp2(l1 - m), jnp.exp2(l4 - m), jnp.exp2(l16 - m)
    odil = (e1 * od1_ref[...] + e4 * od4_ref[...] + e16 * od16_ref[...]) / (e1 + e4 + e16)
    y = (_dot(omoba_ref[...], wout_ref[0]) + _dot(ofox_ref[...], wout_ref[1])
         + _dot(omla_ref[...], wout_ref[2]) + _dot(odil.astype(bf16), wout_ref[3]))
    x1 = x_ref[...] + y * _rms_scale(y, D_MODEL) * gpost_ref[...]
    hb = (x1 * _rms_scale(x1, D_MODEL) * gpre_ref[...]).astype(bf16)
    acc_ref[...] = jnp.zeros_like(acc_ref)

    def chunk(c, _):
        g = _dot(hb, wg_ref[c])
        u = _dot(hb, wu_ref[c])
        f = (g * jax.nn.sigmoid(g) * u).astype(bf16)
        acc_ref[...] += _dot(f, wd_ref[c])
        return 0

    lax.fori_loop(0, wg_ref.shape[0], chunk, 0, unroll=True)
    f = acc_ref[...]
    out_ref[...] = x1 + f * _rms_scale(f, D_MODEL) * gpf_ref[...]


def _rope_tables(positions):
    t = positions.reshape(-1).astype(jnp.float32)

    def tab(dim):
        inv = ROPE_THETA ** (-jnp.arange(0, dim, 2, dtype=jnp.float32) / dim)
        ang = t[:, None] * inv
        return jnp.cos(ang), jnp.sin(ang)

    c64, s64 = tab(HEAD_DIM)
    c16, s16 = tab(MLA_ROPE)
    ct64, st64 = jnp.tile(c64, (1, 4)), jnp.tile(s64, (1, 4))
    pad = jnp.zeros((t.shape[0], 128 - MLA_ROPE), jnp.float32)
    ct16 = jnp.concatenate([c16, c16, pad], axis=1)
    st16 = jnp.concatenate([s16, s16, pad], axis=1)
    cT16, sT16 = jnp.tile(c16, (1, 4)).T, jnp.tile(s16, (1, 4)).T
    return ct64, st64, ct64.T, st64.T, ct16, st16, cT16, sT16


_HALF_PERM = np.array([h * 64 + half * 32 + j for half in (0, 1) for h in range(4) for j in range(32)])
_QROPE_ROWS = np.array([h * 96 + 64 + half * 16 + j for half in (0, 1) for h in range(4) for j in range(16)])
_QNOPE_ROWS = np.array([h * 96 + j for h in range(4) for j in range(64)])
_KNOPE_COLS = np.array([h * 128 + j for h in range(4) for j in range(64)])
_VMLA_COLS = np.array([h * 128 + 64 + j for h in range(4) for j in range(64)])


def _prep_layer_weights(w_in, b_forget, g_mla_q, w_mla_q_up, g_mla_kv, w_mla_kv_up, w_out, w_gate, w_up, w_down):
    bf16 = jnp.bfloat16
    depth = w_in.shape[0]
    sl = lambda a, b: w_in[:, :, a:b]
    scale = HEAD_DIM ** -0.5 * LOG2E
    moba_q, moba_k, moba_v = sl(0, 256)[..., _HALF_PERM] * scale, sl(256, 512)[..., _HALF_PERM], sl(512, 768)
    fox_q, fox_k, fox_v = sl(768, 1024) * scale, sl(1024, 1280), sl(1280, 1536)
    fg, cq, ckv, kr = sl(1536, 1540), sl(1540, 1732), sl(1732, 1860), sl(1860, 1892)
    dil_q, dil_k, dil_v = sl(1892, 2148)[..., _HALF_PERM] * scale, sl(2148, 2404)[..., _HALF_PERM], sl(2404, 2660)
    kr_rot = jnp.concatenate([-kr[..., 16:], kr[..., :16]], axis=-1)
    z = lambda n: jnp.zeros((depth, D_MODEL, n), w_in.dtype)
    wtok = jnp.concatenate([moba_k, fox_k, dil_q, dil_k, dil_v, cq, z(256 - MLA_Q_RANK), ckv,
                            kr, fg, z(128 - 36), kr_rot, z(96)], axis=-1).astype(bf16)
    wtr = jnp.swapaxes(jnp.concatenate([moba_q, moba_v, fox_q, fox_v], axis=-1), 1, 2).astype(bf16)
    bfg = jnp.zeros((depth, 1, 128), jnp.float32).at[:, 0, _FG_LANE:_FG_LANE + N_HEADS].set(b_forget)
    gq = jnp.pad(g_mla_q, ((0, 0), (0, 256 - MLA_Q_RANK)))[:, None, :]
    wq_rows = jnp.swapaxes(w_mla_q_up, 1, 2)
    wqT = jnp.concatenate([wq_rows[:, _QNOPE_ROWS], wq_rows[:, _QROPE_ROWS]], axis=1)
    wqT = jnp.pad(wqT, ((0, 0), (0, 0), (0, 256 - MLA_Q_RANK))).astype(bf16)
    gkv = g_mla_kv[:, None, :]
    wkn = w_mla_kv_up[:, :, _KNOPE_COLS].astype(bf16)
    wvT = jnp.swapaxes(w_mla_kv_up[:, :, _VMLA_COLS], 1, 2).astype(bf16)
    wout = w_out.reshape(depth, 4, GROUP_W, D_MODEL).astype(bf16)
    nc = D_FF // FF_CHUNK
    wg = jnp.swapaxes(w_gate.reshape(depth, D_MODEL, nc, FF_CHUNK), 1, 2).astype(bf16)
    wu = jnp.swapaxes(w_up.reshape(depth, D_MODEL, nc, FF_CHUNK), 1, 2).astype(bf16)
    wd = w_down.reshape(depth, nc, FF_CHUNK, D_MODEL).astype(bf16)
    return wtok, wtr, bfg, gq, wqT, gkv, wkn, wvT, wout, wg, wu, wd


def _const_spec(shape):
    return pl.BlockSpec(shape, lambda *_: (0,) * len(shape))


def _params(*sem):
    return pltpu.CompilerParams(dimension_semantics=sem, vmem_limit_bytes=VMEM_LIMIT)


def _in_proj(x2, g, wtok, wtr, tri, tables, bfg, gq, wqT, gkv, wkn, wvT, seq):
    t = x2.shape[0]
    tm = ROW_TILE
    nt = t // tm
    bf16, f32 = jnp.bfloat16, jnp.float32
    ct64, st64, cT64, sT64, ct16, st16, cT16, sT16 = tables
    tok_spec = lambda w: pl.BlockSpec((tm, w), lambda i: (i, 0))
    tr_spec = lambda r: pl.BlockSpec((r, tm), lambda i: (0, i))
    blk3 = lambda n, r, c: pl.BlockSpec((n, r, c), lambda i: (i, 0, 0))
    mb = tm // MOBA_BLOCK
    ab = tm // KEY_BLOCK
    vrows = N_HEADS * V_ROWS
    vt_shape = jax.ShapeDtypeStruct((t // KEY_BLOCK, vrows, KEY_BLOCK), bf16)
    in_specs = [tok_spec(D_MODEL), _const_spec((1, D_MODEL)), _const_spec(wtok.shape), _const_spec(wtr.shape),
                _const_spec(tri.shape), tok_spec(128), tok_spec(128), tr_spec(128), tr_spec(128),
                tok_spec(128), tok_spec(128), tr_spec(64), tr_spec(64), _const_spec((1, 128)),
                _const_spec((1, 256)), _const_spec(wqT.shape), _const_spec((1, 128)), _const_spec(wkn.shape),
                _const_spec(wvT.shape)]
    out_shape = [
        jax.ShapeDtypeStruct((t, GROUP_W), bf16),
        jax.ShapeDtypeStruct((nt, mb, GROUP_W), f32),
        jax.ShapeDtypeStruct((GROUP_W, t), bf16),
        vt_shape,
        jax.ShapeDtypeStruct((t, GROUP_W), bf16),
        jax.ShapeDtypeStruct((GROUP_W, t), bf16),
        vt_shape,
        jax.ShapeDtypeStruct((t, 128), f32),
        jax.ShapeDtypeStruct((t, GROUP_W), bf16),
        jax.ShapeDtypeStruct((t, GROUP_W), bf16),
        jax.ShapeDtypeStruct((t, GROUP_W), bf16),
        jax.ShapeDtypeStruct((t, 512), bf16),
        jax.ShapeDtypeStruct((384, t), bf16),
        vt_shape,
    ]
    vt_spec = blk3(ab, vrows, KEY_BLOCK)
    out_specs = [tok_spec(GROUP_W), blk3(1, mb, GROUP_W), tr_spec(GROUP_W), vt_spec,
                 tok_spec(GROUP_W), tr_spec(GROUP_W), vt_spec, tok_spec(128),
                 tok_spec(GROUP_W), tok_spec(GROUP_W), tok_spec(GROUP_W), tok_spec(512), tr_spec(384),
                 vt_spec]
    return pl.pallas_call(
        functools.partial(_in_proj_kernel, tiles_per_seq=seq // tm),
        grid=(nt,), in_specs=in_specs, out_specs=out_specs, out_shape=out_shape,
        scratch_shapes=[pltpu.VMEM((1, 128), f32)],
        compiler_params=_params("arbitrary"), name="in_proj",
    )(x2, g, wtok, wtr, tri, ct64, st64, cT64, sT64, ct16, st16, cT16, sT16, bfg, gq, wqT, gkv, wkn, wvT)


def _attention_scratch(tq, tk=KEY_BLOCK):
    return [pltpu.VMEM((GROUP_W, N_HEADS * tq), jnp.bfloat16),
            pltpu.VMEM((8, tq), jnp.float32),
            pltpu.VMEM((N_HEADS, V_ROWS, tq), jnp.float32),
            pltpu.VMEM((tk, N_HEADS * tq), jnp.float32),
            pltpu.VMEM((tk, N_HEADS * tq), jnp.float32)]


def _dense_attention(body, name, qT, k, vT, extra, batch, seq):
    tq = ATT_TILE
    nq = seq // tq
    t = batch * seq
    in_specs = [pl.BlockSpec((qT.shape[0], tq), lambda b, i: (0, b * nq + i)),
                pl.BlockSpec((seq, k.shape[1]), lambda b, i: (b, 0)),
                pl.BlockSpec((seq // KEY_BLOCK, N_HEADS * V_ROWS, KEY_BLOCK), lambda b, i: (b, 0, 0))]
    in_specs += [pl.BlockSpec((seq, 128), lambda b, i: (b, 0)) for _ in extra]
    return pl.pallas_call(
        body, grid=(batch, nq), in_specs=in_specs,
        out_specs=pl.BlockSpec((tq, GROUP_W), lambda b, i: (b * nq + i, 0)),
        out_shape=jax.ShapeDtypeStruct((t, GROUP_W), jnp.bfloat16),
        scratch_shapes=_attention_scratch(tq),
        compiler_params=_params("arbitrary", "arbitrary"), name=name,
    )(qT, k, vT, *extra)


def _moba_attention(qT, k, vT, kmean, batch, seq):
    tq = MOBA_BLOCK
    nb = seq // tq
    t = batch * seq
    return pl.pallas_call(
        _moba_kernel, grid=(batch, nb),
        in_specs=[pl.BlockSpec((GROUP_W, tq), lambda b, i: (0, b * nb + i)),
                  pl.BlockSpec((seq, GROUP_W), lambda b, i: (b, 0)),
                  pl.BlockSpec((seq // KEY_BLOCK, N_HEADS * V_ROWS, KEY_BLOCK), lambda b, i: (b, 0, 0)),
                  pl.BlockSpec((1, nb, GROUP_W), lambda b, i: (b, 0, 0))],
        out_specs=pl.BlockSpec((tq, GROUP_W), lambda b, i: (b * nb + i, 0)),
        out_shape=jax.ShapeDtypeStruct((t, GROUP_W), jnp.bfloat16),
        scratch_shapes=_attention_scratch(tq) + [pltpu.VMEM((N_HEADS * nb, tq), jnp.float32)],
        compiler_params=_params("arbitrary", "arbitrary"), name="moba_attention",
    )(qT, k, vT, kmean)


def _dilated_attention(q, k, v, batch, seq, dil):
    n = seq // dil
    tq = min(DIL_TILE, n)
    window = min(tq + DIL_WINDOW_STEPS, n)
    view = lambda a: a.reshape(batch, n, dil * GROUP_W)
    qspec = pl.BlockSpec((1, tq, GROUP_W), lambda b, r, i: (b, i, r))
    kspec = pl.BlockSpec((1, n, GROUP_W), lambda b, r, i: (b, 0, r))
    shape = jax.ShapeDtypeStruct((batch, n, dil * GROUP_W), jnp.float32)
    o, lse = pl.pallas_call(
        functools.partial(_dilated_kernel, window=window), grid=(batch, dil, n // tq),
        in_specs=[qspec, kspec, kspec], out_specs=[qspec, qspec], out_shape=[shape, shape],
        compiler_params=_params("arbitrary", "arbitrary", "arbitrary"), name=f"dilated_attention_d{dil}",
    )(view(q), view(k), view(v))
    return o.reshape(batch * seq, GROUP_W), lse.reshape(batch * seq, GROUP_W)


def _out_ffn(x2, omoba, ofox, omla, dil_outs, wout, gpost, gpre, wg, wu, wd, gpf):
    t = x2.shape[0]
    tm = ROW_TILE
    row = lambda w: pl.BlockSpec((tm, w), lambda i: (i, 0))
    (od1, l1), (od4, l4), (od16, l16) = dil_outs
    in_specs = ([row(D_MODEL)] + [row(GROUP_W)] * 9
                + [_const_spec(wout.shape), _const_spec((1, D_MODEL)), _const_spec((1, D_MODEL)),
                   _const_spec(wg.shape), _const_spec(wu.shape), _const_spec(wd.shape), _const_spec((1, D_MODEL))])
    return pl.pallas_call(
        _out_ffn_kernel, grid=(t // tm,), in_specs=in_specs, out_specs=row(D_MODEL),
        out_shape=jax.ShapeDtypeStruct((t, D_MODEL), jnp.float32),
        scratch_shapes=[pltpu.VMEM((tm, D_MODEL), jnp.float32)],
        compiler_params=_params("arbitrary"), name="out_ffn",
    )(x2, omoba, ofox, omla, od1, od4, od16, l1, l4, l16, wout, gpost, gpre, wg, wu, wd, gpf)


def kernel(x, positions, w_in, b_forget, g_mla_q, w_mla_q_up, g_mla_kv, w_mla_kv_up, w_out, g_pre_mix, g_post_mix, w_gate, w_up, w_down, g_pre_ffn, g_post_ffn):
    batch, seq, _ = x.shape
    depth = w_in.shape[0]
    assert seq % ROW_TILE == 0 and seq % (DILATIONS[-1] * DIL_WINDOW_STEPS) == 0
    tables = _rope_tables(positions)
    wtok, wtr, bfg, gq, wqT, gkv, wkn, wvT, wout, wg, wu, wd = _prep_layer_weights(
        w_in, b_forget, g_mla_q, w_mla_q_up, g_mla_kv, w_mla_kv_up, w_out, w_gate, w_up, w_down)
    tri = jnp.tril(jnp.ones((ROW_TILE, ROW_TILE), jnp.bfloat16))
    x2 = x.reshape(batch * seq, D_MODEL)
    for l in range(depth):
        (k_moba, kmean, qT_moba, vT_moba, k_fox, qT_fox, vT_fox, cum, q_dil, k_dil, v_dil,
         k_mla, qT_mla, vT_mla) = _in_proj(x2, g_pre_mix[l][None], wtok[l], wtr[l], tri, tables, bfg[l], gq[l],
                                          wqT[l], gkv[l], wkn[l], wvT[l], seq)
        o_moba = _moba_attention(qT_moba, k_moba, vT_moba, kmean.reshape(batch, seq // MOBA_BLOCK, GROUP_W),
                                 batch, seq)
        o_fox = _dense_attention(_fox_kernel, "fox_attention", qT_fox, k_fox, vT_fox, [cum], batch, seq)
        o_mla = _dense_attention(_mla_kernel, "mla_attention", qT_mla, k_mla, vT_mla, [], batch, seq)
        dil_outs = [_dilated_attention(q_dil, k_dil, v_dil, batch, seq, d) for d in DILATIONS]
        x2 = _out_ffn(x2, o_moba, o_fox, o_mla, dil_outs, wout[l], g_post_mix[l][None], g_pre_ffn[l][None],
                      wg[l], wu[l], wd[l], g_post_ffn[l][None])
    return x2.reshape(batch, seq, D_MODEL)
```

```python
import functools

import numpy as np
import jax
import jax.numpy as jnp
from jax import lax
from jax.experimental import pallas as pl
from jax.experimental.pallas import tpu as pltpu

D_MODEL = 1024
HEAD_DIM = 64
N_HEADS = 4
GROUP_W = N_HEADS * HEAD_DIM
MOBA_BLOCK = 256
MOBA_TOPK = 3
MLA_Q_RANK = 192
MLA_KV_RANK = 128
MLA_NOPE = 64
MLA_ROPE = 32
DIL_WINDOW_STEPS = 128
DILATIONS = (1, 4, 16)
ROPE_THETA = 10000.0
RMS_EPS = 1e-6
D_FF = 2816
FF_CHUNK = 256
V_ROWS = 80
LOG2E = 1.4426950408889634
MASKED_LOGIT = -1e30

ROW_TILE = 512
ATT_TILE = 256
KEY_BLOCK = 512
DIL_TILE = 256
VMEM_LIMIT = 56 * 1024 * 1024

_C_KMOBA, _C_KFOX, _C_QDIL, _C_KDIL, _C_VDIL, _C_CQ = 0, 256, 512, 768, 1024, 1280
_C_CKV, _C_X, _C_Y, _N_TOK = 1536, 1664, 1792, 1920
_FG_LANE = 32

_NT = (((1,), (1,)), ((), ()))


def _dot(a, b):
    return jnp.dot(a, b, preferred_element_type=jnp.float32)


def _dot_nt(a, b):
    return lax.dot_general(a, b, _NT, preferred_element_type=jnp.float32)


def _rms_scale(v, n):
    return lax.rsqrt(jnp.sum(v * v, axis=-1, keepdims=True) * (1.0 / n) + RMS_EPS)


def _in_proj_kernel(x_ref, g_ref, wtok_ref, wtr_ref, tri_ref, ct64_ref, st64_ref, cT64_ref, sT64_ref,
                    ct16_ref, st16_ref, cT16_ref, sT16_ref, bfg_ref, gq_ref, wqT_ref, gkv_ref, wkn_ref, wvT_ref,
                    kmoba_ref, kmean_ref, qTmoba_ref, vTmoba_ref, kfox_ref, qTfox_ref, vTfox_ref, cum_ref,
                    q1_ref, q4_ref, q16_ref, k1_ref, k4_ref, k16_ref, v1_ref, v4_ref, v16_ref,
                    kmla_ref, qTmla_ref, vTmla_ref, carry_ref, dscr_ref, *, tiles_per_seq):
    bf16 = jnp.bfloat16
    qdil_refs, kdil_refs, vdil_refs = (q1_ref, q4_ref, q16_ref), (k1_ref, k4_ref, k16_ref), (v1_ref, v4_ref, v16_ref)
    x = x_ref[...]
    hb = (x * _rms_scale(x, D_MODEL) * g_ref[...]).astype(bf16)
    tm = x.shape[0]

    def tok(c0, w):
        return _dot(hb, wtok_ref[:, c0:c0 + w])

    def rope_tok(z):
        c, s = ct64_ref[...], st64_ref[...]
        x1, x2 = z[:, :128], z[:, 128:]
        return jnp.concatenate([x1 * c - x2 * s, x2 * c + x1 * s], axis=1)

    def with_ones(vT):
        ones = jnp.ones((V_ROWS - HEAD_DIM, tm), bf16)
        parts = []
        for h in range(N_HEADS):
            parts += [vT[h * HEAD_DIM:(h + 1) * HEAD_DIM].astype(bf16), ones]
        return jnp.concatenate(parts, axis=0)

    def rope_tr(zT):
        c, s = cT64_ref[...], sT64_ref[...]
        x1, x2 = zT[:128], zT[128:]
        return jnp.concatenate([x1 * c - x2 * s, x2 * c + x1 * s], axis=0)

    k_moba = rope_tok(tok(_C_KMOBA, GROUP_W))
    kmoba_ref[...] = k_moba.astype(bf16)
    for blk in range(tm // MOBA_BLOCK):
        kmean_ref[0, blk:blk + 1, :] = jnp.mean(k_moba[blk * MOBA_BLOCK:(blk + 1) * MOBA_BLOCK], axis=0, keepdims=True)
    qTmoba_ref[...] = rope_tr(_dot_nt(wtr_ref[0:256, :], hb)).astype(bf16)
    def store_value_blocks(ref, vT):
        for blk in range(tm // KEY_BLOCK):
            ref[blk] = vT[:, blk * KEY_BLOCK:(blk + 1) * KEY_BLOCK]

    store_value_blocks(vTmoba_ref, with_ones(_dot_nt(wtr_ref[256:512, :], hb)))

    kfox_ref[...] = tok(_C_KFOX, GROUP_W).astype(bf16)
    qTfox_ref[...] = _dot_nt(wtr_ref[512:768, :], hb).astype(bf16)
    store_value_blocks(vTfox_ref, with_ones(_dot_nt(wtr_ref[768:1024, :], hb)))

    def emit_dilated(refs, slot, z):
        refs[0][...] = z.astype(bf16)
        for c in range(GROUP_W // 128):
            dscr_ref[slot + c] = z[:, c * 128:(c + 1) * 128]
        for ref, dil in zip(refs[1:], DILATIONS[1:]):
            for rho in range(dil):
                for c in range(GROUP_W // 128):
                    rows = dscr_ref[slot + c, pl.ds(rho, tm // dil, stride=dil), :]
                    ref[:, rho * GROUP_W + c * 128:rho * GROUP_W + (c + 1) * 128] = rows.astype(bf16)

    emit_dilated(qdil_refs, 0, rope_tok(tok(_C_QDIL, GROUP_W)))
    emit_dilated(kdil_refs, 2, rope_tok(tok(_C_KDIL, GROUP_W)))
    emit_dilated(vdil_refs, 4, tok(_C_VDIL, GROUP_W))

    xblk = tok(_C_X, 128)
    yblk = tok(_C_Y, 128)
    fg = xblk + bfg_ref[...]
    logf = jnp.minimum(fg, 0.0) - jnp.log1p(jnp.exp(-jnp.abs(fg)))
    a1 = logf.astype(bf16)
    r1 = logf - a1.astype(jnp.float32)
    a2 = r1.astype(bf16)
    a3 = (r1 - a2.astype(jnp.float32)).astype(bf16)
    tri = tri_ref[...]

    @pl.when(pl.program_id(0) % tiles_per_seq == 0)
    def _():
        carry_ref[...] = jnp.zeros_like(carry_ref)

    cum = _dot(tri, a1) + _dot(tri, a2) + _dot(tri, a3) + carry_ref[...]
    cum_ref[...] = cum
    carry_ref[...] = cum[tm - 1:tm, :]

    zcq = tok(_C_CQ, 256)
    cq = (zcq * _rms_scale(zcq, MLA_Q_RANK) * gq_ref[...]).astype(bf16)
    qcT = _dot_nt(wqT_ref[...], cq)
    c16, s16 = cT16_ref[...], sT16_ref[...]
    r1h, r2h = qcT[256:320], qcT[320:384]
    qT = jnp.concatenate([qcT[0:256], r1h * c16 - r2h * s16, r2h * c16 + r1h * s16], axis=0)
    qTmla_ref[...] = (qT * ((MLA_NOPE + MLA_ROPE) ** -0.5 * LOG2E)).astype(bf16)

    zckv = tok(_C_CKV, MLA_KV_RANK)
    ckv = (zckv * _rms_scale(zckv, MLA_KV_RANK) * gkv_ref[...]).astype(bf16)
    kn = _dot(ckv, wkn_ref[...])
    krope = xblk * ct16_ref[...] + yblk * st16_ref[...]
    kmla_ref[...] = jnp.concatenate([kn[:, :128], krope, kn[:, 128:], krope], axis=1).astype(bf16)
    store_value_blocks(vTmla_ref, with_ones(_dot_nt(wvT_ref[...], ckv)))


def _softmax_update(s, vb, m_ref, acc_ref, h, valid=None):
    m_old = m_ref[h:h + 1, :]
    blk_max = jnp.max(s, axis=0, keepdims=True)
    if valid is not None:
        blk_max = jnp.where(valid, blk_max, -jnp.inf)
    m_new = jnp.maximum(m_old, blk_max)
    shift = m_new if valid is None else jnp.where(valid, m_new, jnp.inf)
    p = jnp.exp2(s - shift)
    acc_ref[h] = jnp.exp2(m_old - m_new) * acc_ref[h] + _dot(vb, p.astype(jnp.bfloat16))
    m_ref[h:h + 1, :] = m_new


def _own_block_mask(i, tq, tk):
    key = lax.broadcasted_iota(jnp.int32, (tk, tq), 0)
    qry = lax.broadcasted_iota(jnp.int32, (tk, tq), 1) + (i % (tk // tq)) * tq
    return key <= qry


def _init_state(m_ref, acc_ref):
    m_ref[...] = jnp.full(m_ref.shape, -jnp.inf, jnp.float32)
    acc_ref[...] = jnp.zeros(acc_ref.shape, jnp.float32)


def _finalize(acc_ref, o_ref):
    outs = [acc_ref[h, 0:HEAD_DIM, :] / acc_ref[h, HEAD_DIM:HEAD_DIM + 1, :] for h in range(N_HEADS)]
    o_ref[...] = jnp.concatenate(outs, axis=0).T.astype(o_ref.dtype)


def _vblock(vT_ref, j, h):
    return vT_ref[j, h * V_ROWS:(h + 1) * V_ROWS, :]


def _run_key_blocks(own, last, sa_ref, sb_ref, scores_into, consume):
    scores_into(sa_ref, own)
    scores_into(sb_ref, 0)
    consume(sa_ref, own, True, None)

    def pair(p, carry):
        j0 = 2 * p
        scores_into(sa_ref, jnp.minimum(j0 + 1, last))
        consume(sb_ref, j0, False, None)
        scores_into(sb_ref, jnp.minimum(j0 + 2, last))
        consume(sa_ref, jnp.minimum(j0 + 1, last), False, j0 + 1 < own)
        return carry

    lax.fori_loop(0, (own + 1) // 2, pair, 0)


def _key_rows(k_ref, j, tk):
    return k_ref[pl.ds(pl.multiple_of(j * tk, tk), tk), :]


def _fox_kernel(qT_ref, k_ref, vT_ref, cum_ref, o_ref, qcat_ref, m_ref, acc_ref, sa_ref, sb_ref):
    i = pl.program_id(1)
    tq = qT_ref.shape[1]
    q = qT_ref[...]
    row = lax.broadcasted_iota(jnp.int32, q.shape, 0)
    for h in range(N_HEADS):
        in_head = (row >= h * HEAD_DIM) & (row < (h + 1) * HEAD_DIM)
        qcat_ref[:, h * tq:(h + 1) * tq] = jnp.where(in_head, q, jnp.zeros_like(q))
    _init_state(m_ref, acc_ref)
    tk = sa_ref.shape[0]
    causal = _own_block_mask(i, tq, tk)
    c0 = cum_ref[pl.ds(pl.multiple_of(i * tq, tq), 1), :]

    def scores_into(s_ref, j):
        s_ref[...] = _dot(_key_rows(k_ref, j, tk), qcat_ref[...])

    def consume(s_ref, j, own, valid):
        bias = (c0 - _key_rows(cum_ref, j, tk)) * LOG2E
        for h in range(N_HEADS):
            s = s_ref[:, h * tq:(h + 1) * tq] + bias[:, _FG_LANE + h:_FG_LANE + h + 1]
            if own:
                s = jnp.where(causal, s, -jnp.inf)
            _softmax_update(s, _vblock(vT_ref, j, h), m_ref, acc_ref, h, valid)

    _run_key_blocks(i * tq // tk, k_ref.shape[0] // tk - 1, sa_ref, sb_ref, scores_into, consume)
    _finalize(acc_ref, o_ref)


def _mla_kernel(qT_ref, k_ref, vT_ref, o_ref, qcat_ref, m_ref, acc_ref, sa_ref, sb_ref):
    i = pl.program_id(1)
    tq = qT_ref.shape[1]
    zeros64 = jnp.zeros((MLA_NOPE, tq), jnp.bfloat16)
    zeros96 = jnp.zeros((256 - 128 - MLA_ROPE, tq), jnp.bfloat16)
    for h in range(N_HEADS):
        qn = qT_ref[h * MLA_NOPE:(h + 1) * MLA_NOPE, :]
        qr1 = qT_ref[256 + 16 * h:256 + 16 * (h + 1), :]
        qr2 = qT_ref[320 + 16 * h:320 + 16 * (h + 1), :]
        nope = [qn, zeros64] if h % 2 == 0 else [zeros64, qn]
        qcat_ref[:, h * tq:(h + 1) * tq] = jnp.concatenate(nope + [qr1, qr2, zeros96], axis=0)
    _init_state(m_ref, acc_ref)
    tk = sa_ref.shape[0]
    causal = _own_block_mask(i, tq, tk)

    def scores_into(s_ref, j):
        kb = _key_rows(k_ref, j, tk)
        for half in range(2):
            cols = slice(2 * half * tq, 2 * (half + 1) * tq)
            s_ref[:, cols] = _dot(kb[:, 256 * half:256 * (half + 1)], qcat_ref[:, cols])

    def consume(s_ref, j, own, valid):
        for h in range(N_HEADS):
            s = s_ref[:, h * tq:(h + 1) * tq]
            if own:
                s = jnp.where(causal, s, -jnp.inf)
            _softmax_update(s, _vblock(vT_ref, j, h), m_ref, acc_ref, h, valid)

    _run_key_blocks(i * tq // tk, k_ref.shape[0] // tk - 1, sa_ref, sb_ref, scores_into, consume)
    _finalize(acc_ref, o_ref)


def _moba_kernel(qT_ref, k_ref, vT_ref, km_ref, o_ref, qcat_ref, m_ref, acc_ref, sa_ref, sb_ref, bias_ref):
    i = pl.program_id(1)
    tq = qT_ref.shape[1]
    nb = km_ref.shape[1]
    q = qT_ref[...]
    row = lax.broadcasted_iota(jnp.int32, q.shape, 0)
    km = km_ref[0].astype(jnp.bfloat16)
    blk = lax.broadcasted_iota(jnp.int32, (nb, tq), 0)
    neg_inf = jnp.float32(-jnp.inf)
    for h in range(N_HEADS):
        in_head = (((row >= 32 * h) & (row < 32 * (h + 1)))
                   | ((row >= 128 + 32 * h) & (row < 128 + 32 * (h + 1))))
        qm = jnp.where(in_head, q, jnp.zeros_like(q))
        qcat_ref[:, h * tq:(h + 1) * tq] = qm
        g = jnp.where(blk < i, _dot(km, qm), neg_inf)
        sel = jnp.zeros((nb, tq), jnp.bool_)
        for _ in range(MOBA_TOPK):
            mx = jnp.max(g, axis=0, keepdims=True)
            cand = jnp.where((g == mx) & (mx > neg_inf), blk, nb)
            chosen = blk == jnp.min(cand, axis=0, keepdims=True)
            sel = sel | chosen
            g = jnp.where(chosen, neg_inf, g)
        bias_ref[h * nb:(h + 1) * nb, :] = jnp.where(sel, 0.0, MASKED_LOGIT)
    _init_state(m_ref, acc_ref)
    tk = sa_ref.shape[0]
    per = tk // tq
    causal = _own_block_mask(i, tq, tk)

    def scores_into(s_ref, j):
        s_ref[...] = _dot(_key_rows(k_ref, j, tk), qcat_ref[...])

    def consume(s_ref, j, own, valid):
        for h in range(N_HEADS):
            parts = []
            for sub in range(per):
                n = per * j + sub
                gate = bias_ref[pl.ds(h * nb + n, 1), :]
                if own:
                    gate = jnp.where(n < i, gate, 0.0)
                parts.append(s_ref[sub * tq:(sub + 1) * tq, h * tq:(h + 1) * tq] + gate)
            s = parts[0] if per == 1 else jnp.concatenate(parts, axis=0)
            if own:
                s = jnp.where(causal, s, neg_inf)
            _softmax_update(s, _vblock(vT_ref, j, h), m_ref, acc_ref, h, valid)

    _run_key_blocks(i // per, nb // per - 1, sa_ref, sb_ref, scores_into, consume)
    _finalize(acc_ref, o_ref)


def _dilated_kernel(q_ref, k_ref, v_ref, o_ref, lse_ref, *, window):
    i = pl.program_id(2)
    tq = q_ref.shape[1]
    n = k_ref.shape[1]
    q = q_ref[0]
    a = i * tq
    ks = jnp.clip(a - DIL_WINDOW_STEPS, 0, n - window)
    ks = pl.multiple_of(ks, DIL_WINDOW_STEPS)
    kw = k_ref[0, pl.ds(ks, window), :]
    vw = v_ref[0, pl.ds(ks, window), :]
    jq = a + lax.broadcasted_iota(jnp.int32, (tq, window), 0)
    jk = ks + lax.broadcasted_iota(jnp.int32, (tq, window), 1)
    band = (jq - jk >= 0) & (jq - jk <= DIL_WINDOW_STEPS)
    lane = lax.broadcasted_iota(jnp.int32, (tq, GROUP_W), 1)
    o = jnp.zeros((tq, GROUP_W), jnp.float32)
    lse = jnp.zeros((tq, GROUP_W), jnp.float32)
    for h in range(N_HEADS):
        in_head = (((lane >= 32 * h) & (lane < 32 * (h + 1)))
                   | ((lane >= 128 + 32 * h) & (lane < 128 + 32 * (h + 1))))
        qm = jnp.where(in_head, q, jnp.zeros_like(q))
        s = jnp.where(band, _dot_nt(qm, kw), -jnp.inf)
        m = jnp.max(s, axis=1, keepdims=True)
        p = jnp.exp2(s - m)
        l = jnp.sum(p, axis=1, keepdims=True)
        oh = _dot(p.astype(jnp.bfloat16), vw) / l
        out_lanes = (lane >= h * HEAD_DIM) & (lane < (h + 1) * HEAD_DIM)
        o = jnp.where(out_lanes, oh, o)
        lse = jnp.where(out_lanes, m + jnp.log2(l), lse)
    o_ref[0] = o
    lse_ref[0] = lse


def _out_ffn_kernel(x_ref, omoba_ref, ofox_ref, omla_ref, od1_ref, od4_ref, od16_ref, l1_ref, l4_ref, l16_ref,
                    wout_ref, gpost_ref, gpre_ref, wg_ref, wu_ref, wd_ref, gpf_ref, out_ref, acc_ref,
                    dscr_ref):
    bf16 = jnp.bfloat16
    tm = x_ref.shape[0]

    def token_order(ref, slot, dil):
        halves = GROUP_W // 128
        for rho in range(dil):
            for c in range(halves):
                lanes = slice(rho * GROUP_W + c * 128, rho * GROUP_W + (c + 1) * 128)
                dscr_ref[slot + c, pl.ds(rho, tm // dil, stride=dil), :] = ref[:, lanes]
        return jnp.concatenate([dscr_ref[slot + c] for c in range(halves)], axis=1)

    l1, od1 = l1_ref[...], od1_ref[...]
    l4, od4 = token_order(l4_ref, 0, 4), token_order(od4_ref, 2, 4)
    l16, od16 = token_order(l16_ref, 4, 16), token_order(od16_ref, 6, 16)
    m = jnp.maximum(jnp.maximum(l1, l4), l16)
    e1, e4, e16 = jnp.exp2(l1 - m), jnp.exp2(l4 - m), jnp.exp2(l16 - m)
    odil = (e1 * od1 + e4 * od4 + e16 * od16) / (e1 + e4 + e16)
    y = (_dot(omoba_ref[...], wout_ref[0]) + _dot(ofox_ref[...], wout_ref[1])
         + _dot(omla_ref[...], wout_ref[2]) + _dot(odil.astype(bf16), wout_ref[3]))
    x1 = x_ref[...] + y * _rms_scale(y, D_MODEL) * gpost_ref[...]
    hb = (x1 * _rms_scale(x1, D_MODEL) * gpre_ref[...]).astype(bf16)
    acc_ref[...] = jnp.zeros_like(acc_ref)

    def chunk(c, _):
        g = _dot(hb, wg_ref[c])
        u = _dot(hb, wu_ref[c])
        f = (g * jax.nn.sigmoid(g) * u).astype(bf16)
        acc_ref[...] += _dot(f, wd_ref[c])
        return 0

    lax.fori_loop(0, wg_ref.shape[0], chunk, 0, unroll=True)
    f = acc_ref[...]
    out_ref[...] = x1 + f * _rms_scale(f, D_MODEL) * gpf_ref[...]


def _rope_tables(positions):
    t = positions.reshape(-1).astype(jnp.float32)

    def tab(dim):
        inv = ROPE_THETA ** (-jnp.arange(0, dim, 2, dtype=jnp.float32) / dim)
        ang = t[:, None] * inv
        return jnp.cos(ang), jnp.sin(ang)

    c64, s64 = tab(HEAD_DIM)
    c16, s16 = tab(MLA_ROPE)
    ct64, st64 = jnp.tile(c64, (1, 4)), jnp.tile(s64, (1, 4))
    pad = jnp.zeros((t.shape[0], 128 - MLA_ROPE), jnp.float32)
    ct16 = jnp.concatenate([c16, c16, pad], axis=1)
    st16 = jnp.concatenate([s16, s16, pad], axis=1)
    cT16, sT16 = jnp.tile(c16, (1, 4)).T, jnp.tile(s16, (1, 4)).T
    return ct64, st64, ct64.T, st64.T, ct16, st16, cT16, sT16


_HALF_PERM = np.array([h * 64 + half * 32 + j for half in (0, 1) for h in range(4) for j in range(32)])
_QROPE_ROWS = np.array([h * 96 + 64 + half * 16 + j for half in (0, 1) for h in range(4) for j in range(16)])
_QNOPE_ROWS = np.array([h * 96 + j for h in range(4) for j in range(64)])
_KNOPE_COLS = np.array([h * 128 + j for h in range(4) for j in range(64)])
_VMLA_COLS = np.array([h * 128 + 64 + j for h in range(4) for j in range(64)])


def _prep_layer_weights(w_in, b_forget, g_mla_q, w_mla_q_up, g_mla_kv, w_mla_kv_up, w_out, w_gate, w_up, w_down):
    bf16 = jnp.bfloat16
    depth = w_in.shape[0]
    sl = lambda a, b: w_in[:, :, a:b]
    scale = HEAD_DIM ** -0.5 * LOG2E
    moba_q, moba_k, moba_v = sl(0, 256)[..., _HALF_PERM] * scale, sl(256, 512)[..., _HALF_PERM], sl(512, 768)
    fox_q, fox_k, fox_v = sl(768, 1024) * scale, sl(1024, 1280), sl(1280, 1536)
    fg, cq, ckv, kr = sl(1536, 1540), sl(1540, 1732), sl(1732, 1860), sl(1860, 1892)
    dil_q, dil_k, dil_v = sl(1892, 2148)[..., _HALF_PERM] * scale, sl(2148, 2404)[..., _HALF_PERM], sl(2404, 2660)
    kr_rot = jnp.concatenate([-kr[..., 16:], kr[..., :16]], axis=-1)
    z = lambda n: jnp.zeros((depth, D_MODEL, n), w_in.dtype)
    wtok = jnp.concatenate([moba_k, fox_k, dil_q, dil_k, dil_v, cq, z(256 - MLA_Q_RANK), ckv,
                            kr, fg, z(128 - 36), kr_rot, z(96)], axis=-1).astype(bf16)
    wtr = jnp.swapaxes(jnp.concatenate([moba_q, moba_v, fox_q, fox_v], axis=-1), 1, 2).astype(bf16)
    bfg = jnp.zeros((depth, 1, 128), jnp.float32).at[:, 0, _FG_LANE:_FG_LANE + N_HEADS].set(b_forget)
    gq = jnp.pad(g_mla_q, ((0, 0), (0, 256 - MLA_Q_RANK)))[:, None, :]
    wq_rows = jnp.swapaxes(w_mla_q_up, 1, 2)
    wqT = jnp.concatenate([wq_rows[:, _QNOPE_ROWS], wq_rows[:, _QROPE_ROWS]], axis=1)
    wqT = jnp.pad(wqT, ((0, 0), (0, 0), (0, 256 - MLA_Q_RANK))).astype(bf16)
    gkv = g_mla_kv[:, None, :]
    wkn = w_mla_kv_up[:, :, _KNOPE_COLS].astype(bf16)
    wvT = jnp.swapaxes(w_mla_kv_up[:, :, _VMLA_COLS], 1, 2).astype(bf16)
    wout = w_out.reshape(depth, 4, GROUP_W, D_MODEL).astype(bf16)
    nc = D_FF // FF_CHUNK
    wg = jnp.swapaxes(w_gate.reshape(depth, D_MODEL, nc, FF_CHUNK), 1, 2).astype(bf16)
    wu = jnp.swapaxes(w_up.reshape(depth, D_MODEL, nc, FF_CHUNK), 1, 2).astype(bf16)
    wd = w_down.reshape(depth, nc, FF_CHUNK, D_MODEL).astype(bf16)
    return wtok, wtr, bfg, gq, wqT, gkv, wkn, wvT, wout, wg, wu, wd


def _const_spec(shape):
    return pl.BlockSpec(shape, lambda *_: (0,) * len(shape))


def _params(*sem):
    return pltpu.CompilerParams(dimension_semantics=sem, vmem_limit_bytes=VMEM_LIMIT)


def _in_proj(x2, g, wtok, wtr, tri, tables, bfg, gq, wqT, gkv, wkn, wvT, seq):
    t = x2.shape[0]
    tm = ROW_TILE
    nt = t // tm
    bf16, f32 = jnp.bfloat16, jnp.float32
    ct64, st64, cT64, sT64, ct16, st16, cT16, sT16 = tables
    tok_spec = lambda w: pl.BlockSpec((tm, w), lambda i: (i, 0))
    tr_spec = lambda r: pl.BlockSpec((r, tm), lambda i: (0, i))
    blk3 = lambda n, r, c: pl.BlockSpec((n, r, c), lambda i: (i, 0, 0))
    mb = tm // MOBA_BLOCK
    ab = tm // KEY_BLOCK
    vrows = N_HEADS * V_ROWS
    vt_shape = jax.ShapeDtypeStruct((t // KEY_BLOCK, vrows, KEY_BLOCK), bf16)
    in_specs = [tok_spec(D_MODEL), _const_spec((1, D_MODEL)), _const_spec(wtok.shape), _const_spec(wtr.shape),
                _const_spec(tri.shape), tok_spec(128), tok_spec(128), tr_spec(128), tr_spec(128),
                tok_spec(128), tok_spec(128), tr_spec(64), tr_spec(64), _const_spec((1, 128)),
                _const_spec((1, 256)), _const_spec(wqT.shape), _const_spec((1, 128)), _const_spec(wkn.shape),
                _const_spec(wvT.shape)]
    out_shape = [
        jax.ShapeDtypeStruct((t, GROUP_W), bf16),
        jax.ShapeDtypeStruct((nt, mb, GROUP_W), f32),
        jax.ShapeDtypeStruct((GROUP_W, t), bf16),
        vt_shape,
        jax.ShapeDtypeStruct((t, GROUP_W), bf16),
        jax.ShapeDtypeStruct((GROUP_W, t), bf16),
        vt_shape,
        jax.ShapeDtypeStruct((t, 128), f32),
        *[jax.ShapeDtypeStruct((t // d, d * GROUP_W), bf16) for d in DILATIONS] * 3,
        jax.ShapeDtypeStruct((t, 512), bf16),
        jax.ShapeDtypeStruct((384, t), bf16),
        vt_shape,
    ]
    vt_spec = blk3(ab, vrows, KEY_BLOCK)
    out_specs = [tok_spec(GROUP_W), blk3(1, mb, GROUP_W), tr_spec(GROUP_W), vt_spec,
                 tok_spec(GROUP_W), tr_spec(GROUP_W), vt_spec, tok_spec(128),
                 *[pl.BlockSpec((tm // d, d * GROUP_W), lambda i: (i, 0)) for d in DILATIONS] * 3,
                 tok_spec(512), tr_spec(384), vt_spec]
    return pl.pallas_call(
        functools.partial(_in_proj_kernel, tiles_per_seq=seq // tm),
        grid=(nt,), in_specs=in_specs, out_specs=out_specs, out_shape=out_shape,
        scratch_shapes=[pltpu.VMEM((1, 128), f32), pltpu.VMEM((3 * GROUP_W // 128, tm, 128), f32)],
        compiler_params=_params("arbitrary"), name="in_proj",
    )(x2, g, wtok, wtr, tri, ct64, st64, cT64, sT64, ct16, st16, cT16, sT16, bfg, gq, wqT, gkv, wkn, wvT)


def _attention_scratch(tq, tk=KEY_BLOCK):
    return [pltpu.VMEM((GROUP_W, N_HEADS * tq), jnp.bfloat16),
            pltpu.VMEM((8, tq), jnp.float32),
            pltpu.VMEM((N_HEADS, V_ROWS, tq), jnp.float32),
            pltpu.VMEM((tk, N_HEADS * tq), jnp.float32),
            pltpu.VMEM((tk, N_HEADS * tq), jnp.float32)]


def _dense_attention(body, name, qT, k, vT, extra, batch, seq):
    tq = ATT_TILE
    nq = seq // tq
    t = batch * seq
    in_specs = [pl.BlockSpec((qT.shape[0], tq), lambda b, i: (0, b * nq + i)),
                pl.BlockSpec((seq, k.shape[1]), lambda b, i: (b, 0)),
                pl.BlockSpec((seq // KEY_BLOCK, N_HEADS * V_ROWS, KEY_BLOCK), lambda b, i: (b, 0, 0))]
    in_specs += [pl.BlockSpec((seq, 128), lambda b, i: (b, 0)) for _ in extra]
    return pl.pallas_call(
        body, grid=(batch, nq), in_specs=in_specs,
        out_specs=pl.BlockSpec((tq, GROUP_W), lambda b, i: (b * nq + i, 0)),
        out_shape=jax.ShapeDtypeStruct((t, GROUP_W), jnp.bfloat16),
        scratch_shapes=_attention_scratch(tq),
        compiler_params=_params("arbitrary", "arbitrary"), name=name,
    )(qT, k, vT, *extra)


def _moba_attention(qT, k, vT, kmean, batch, seq):
    tq = MOBA_BLOCK
    nb = seq // tq
    t = batch * seq
    return pl.pallas_call(
        _moba_kernel, grid=(batch, nb),
        in_specs=[pl.BlockSpec((GROUP_W, tq), lambda b, i: (0, b * nb + i)),
                  pl.BlockSpec((seq, GROUP_W), lambda b, i: (b, 0)),
                  pl.BlockSpec((seq // KEY_BLOCK, N_HEADS * V_ROWS, KEY_BLOCK), lambda b, i: (b, 0, 0)),
                  pl.BlockSpec((1, nb, GROUP_W), lambda b, i: (b, 0, 0))],
        out_specs=pl.BlockSpec((tq, GROUP_W), lambda b, i: (b * nb + i, 0)),
        out_shape=jax.ShapeDtypeStruct((t, GROUP_W), jnp.bfloat16),
        scratch_shapes=_attention_scratch(tq) + [pltpu.VMEM((N_HEADS * nb, tq), jnp.float32)],
        compiler_params=_params("arbitrary", "arbitrary"), name="moba_attention",
    )(qT, k, vT, kmean)


def _dilated_attention(q, k, v, batch, seq, dil):
    n = seq // dil
    tq = min(DIL_TILE, n)
    window = min(tq + DIL_WINDOW_STEPS, n)
    view = lambda a: a.reshape(batch, n, dil * GROUP_W)
    qspec = pl.BlockSpec((1, tq, GROUP_W), lambda b, r, i: (b, i, r))
    kspec = pl.BlockSpec((1, n, GROUP_W), lambda b, r, i: (b, 0, r))
    shape = jax.ShapeDtypeStruct((batch, n, dil * GROUP_W), jnp.float32)
    o, lse = pl.pallas_call(
        functools.partial(_dilated_kernel, window=window), grid=(batch, dil, n // tq),
        in_specs=[qspec, kspec, kspec], out_specs=[qspec, qspec], out_shape=[shape, shape],
        compiler_params=_params("arbitrary", "arbitrary", "arbitrary"), name=f"dilated_attention_d{dil}",
    )(view(q), view(k), view(v))
    return o.reshape(batch * n, dil * GROUP_W), lse.reshape(batch * n, dil * GROUP_W)


def _out_ffn(x2, omoba, ofox, omla, dil_outs, wout, gpost, gpre, wg, wu, wd, gpf):
    t = x2.shape[0]
    tm = ROW_TILE
    row = lambda w: pl.BlockSpec((tm, w), lambda i: (i, 0))
    (od1, l1), (od4, l4), (od16, l16) = dil_outs
    dil_specs = [pl.BlockSpec((tm // d, d * GROUP_W), lambda i: (i, 0)) for d in DILATIONS]
    in_specs = ([row(D_MODEL)] + [row(GROUP_W)] * 3 + dil_specs * 2
                + [_const_spec(wout.shape), _const_spec((1, D_MODEL)), _const_spec((1, D_MODEL)),
                   _const_spec(wg.shape), _const_spec(wu.shape), _const_spec(wd.shape), _const_spec((1, D_MODEL))])
    return pl.pallas_call(
        _out_ffn_kernel, grid=(t // tm,), in_specs=in_specs, out_specs=row(D_MODEL),
        out_shape=jax.ShapeDtypeStruct((t, D_MODEL), jnp.float32),
        scratch_shapes=[pltpu.VMEM((tm, D_MODEL), jnp.float32), pltpu.VMEM((4 * GROUP_W // 128, tm, 128), jnp.float32)],
        compiler_params=_params("arbitrary"), name="out_ffn",
    )(x2, omoba, ofox, omla, od1, od4, od16, l1, l4, l16, wout, gpost, gpre, wg, wu, wd, gpf)


def kernel(x, positions, w_in, b_forget, g_mla_q, w_mla_q_up, g_mla_kv, w_mla_kv_up, w_out, g_pre_mix, g_post_mix, w_gate, w_up, w_down, g_pre_ffn, g_post_ffn):
    batch, seq, _ = x.shape
    depth = w_in.shape[0]
    assert seq % ROW_TILE == 0 and seq % (DILATIONS[-1] * DIL_WINDOW_STEPS) == 0
    tables = _rope_tables(positions)
    wtok, wtr, bfg, gq, wqT, gkv, wkn, wvT, wout, wg, wu, wd = _prep_layer_weights(
        w_in, b_forget, g_mla_q, w_mla_q_up, g_mla_kv, w_mla_kv_up, w_out, w_gate, w_up, w_down)
    tri = jnp.tril(jnp.ones((ROW_TILE, ROW_TILE), jnp.bfloat16))
    x2 = x.reshape(batch * seq, D_MODEL)
    for l in range(depth):
        (k_moba, kmean, qT_moba, vT_moba, k_fox, qT_fox, vT_fox, cum, *dil_qkv,
         k_mla, qT_mla, vT_mla) = _in_proj(x2, g_pre_mix[l][None], wtok[l], wtr[l], tri, tables, bfg[l], gq[l],
                                          wqT[l], gkv[l], wkn[l], wvT[l], seq)
        nd = len(DILATIONS)
        o_moba = _moba_attention(qT_moba, k_moba, vT_moba, kmean.reshape(batch, seq // MOBA_BLOCK, GROUP_W),
                                 batch, seq)
        o_fox = _dense_attention(_fox_kernel, "fox_attention", qT_fox, k_fox, vT_fox, [cum], batch, seq)
        o_mla = _dense_attention(_mla_kernel, "mla_attention", qT_mla, k_mla, vT_mla, [], batch, seq)
        dil_outs = [_dilated_attention(dil_qkv[n], dil_qkv[nd + n], dil_qkv[2 * nd + n], batch, seq, d)
                    for n, d in enumerate(DILATIONS)]
        x2 = _out_ffn(x2, o_moba, o_fox, o_mla, dil_outs, wout[l], g_post_mix[l][None], g_pre_ffn[l][None],
                      wg[l], wu[l], wd[l], g_post_ffn[l][None])
    return x2.reshape(batch, seq, D_MODEL)
```

```python
import functools

import numpy as np
import jax
import jax.numpy as jnp
from jax import lax
from jax.experimental import pallas as pl
from jax.experimental.pallas import tpu as pltpu

D_MODEL = 1024
HEAD_DIM = 64
N_HEADS = 4
GROUP_W = N_HEADS * HEAD_DIM
MOBA_BLOCK = 256
MOBA_TOPK = 3
MLA_Q_RANK = 192
MLA_KV_RANK = 128
MLA_NOPE = 64
MLA_ROPE = 32
DIL_WINDOW_STEPS = 128
DILATIONS = (1, 4, 16)
ROPE_THETA = 10000.0
RMS_EPS = 1e-6
D_FF = 2816
FF_CHUNK = 256
V_ROWS = 80
LOG2E = 1.4426950408889634
MASKED_LOGIT = -1e30

ROW_TILE = 512
ATT_TILE = 512
KEY_BLOCK = 512
DIL_TILE = 256
VMEM_LIMIT = 56 * 1024 * 1024

_C_KMOBA, _C_KFOX, _C_QDIL, _C_KDIL, _C_VDIL, _C_CQ = 0, 256, 512, 768, 1024, 1280
_C_CKV, _C_X, _C_Y, _N_TOK = 1536, 1664, 1792, 1920
_FG_LANE = 32

_NT = (((1,), (1,)), ((), ()))


def _dot(a, b):
    return jnp.dot(a, b, preferred_element_type=jnp.float32)


def _dot_nt(a, b):
    return lax.dot_general(a, b, _NT, preferred_element_type=jnp.float32)


def _rms_scale(v, n):
    return lax.rsqrt(jnp.sum(v * v, axis=-1, keepdims=True) * (1.0 / n) + RMS_EPS)


def _in_proj_kernel(x_ref, g_ref, wtok_ref, wtr_ref, tri_ref, ct64_ref, st64_ref, cT64_ref, sT64_ref,
                    ct16_ref, st16_ref, cT16_ref, sT16_ref, bfg_ref, gq_ref, wqT_ref, gkv_ref, wkn_ref, wvT_ref,
                    kmoba_ref, kmean_ref, qTmoba_ref, vTmoba_ref, kfox_ref, qTfox_ref, vTfox_ref, cbase_ref,
                    q1_ref, q4_ref, q16_ref, k1_ref, k4_ref, k16_ref, v1_ref, v4_ref, v16_ref,
                    kmla_ref, qTmla_ref, vTmla_ref, carry_ref, dscr_ref, *, tiles_per_seq):
    bf16 = jnp.bfloat16
    qdil_refs, kdil_refs, vdil_refs = (q1_ref, q4_ref, q16_ref), (k1_ref, k4_ref, k16_ref), (v1_ref, v4_ref, v16_ref)
    x = x_ref[...]
    hb = (x * _rms_scale(x, D_MODEL) * g_ref[...]).astype(bf16)
    tm = x.shape[0]

    def tok(c0, w):
        return _dot(hb, wtok_ref[:, c0:c0 + w])

    def rope_tok(z):
        c, s = ct64_ref[...], st64_ref[...]
        x1, x2 = z[:, :128], z[:, 128:]
        return jnp.concatenate([x1 * c - x2 * s, x2 * c + x1 * s], axis=1)

    def with_ones(vT):
        ones = jnp.ones((V_ROWS - HEAD_DIM, tm), bf16)
        parts = []
        for h in range(N_HEADS):
            parts += [vT[h * HEAD_DIM:(h + 1) * HEAD_DIM].astype(bf16), ones]
        return jnp.concatenate(parts, axis=0)

    def rope_tr(zT):
        c, s = cT64_ref[...], sT64_ref[...]
        x1, x2 = zT[:128], zT[128:]
        return jnp.concatenate([x1 * c - x2 * s, x2 * c + x1 * s], axis=0)

    k_moba = rope_tok(tok(_C_KMOBA, GROUP_W))
    kmoba_ref[...] = k_moba.astype(bf16)
    for blk in range(tm // MOBA_BLOCK):
        kmean_ref[0, blk:blk + 1, :] = jnp.mean(k_moba[blk * MOBA_BLOCK:(blk + 1) * MOBA_BLOCK], axis=0, keepdims=True)
    qTmoba_ref[...] = rope_tr(_dot_nt(wtr_ref[0:256, :], hb)).astype(bf16)
    def store_value_blocks(ref, vT):
        for blk in range(tm // KEY_BLOCK):
            ref[blk] = vT[:, blk * KEY_BLOCK:(blk + 1) * KEY_BLOCK]

    store_value_blocks(vTmoba_ref, with_ones(_dot_nt(wtr_ref[256:512, :], hb)))

    qTfox_ref[...] = _dot_nt(wtr_ref[512:768, :], hb).astype(bf16)
    store_value_blocks(vTfox_ref, with_ones(_dot_nt(wtr_ref[768:1024, :], hb)))

    def emit_dilated(refs, slot, z):
        refs[0][...] = z.astype(bf16)
        for c in range(GROUP_W // 128):
            dscr_ref[slot + c] = z[:, c * 128:(c + 1) * 128]
        for ref, dil in zip(refs[1:], DILATIONS[1:]):
            for rho in range(dil):
                for c in range(GROUP_W // 128):
                    rows = dscr_ref[slot + c, pl.ds(rho, tm // dil, stride=dil), :]
                    ref[:, rho * GROUP_W + c * 128:rho * GROUP_W + (c + 1) * 128] = rows.astype(bf16)

    emit_dilated(qdil_refs, 0, rope_tok(tok(_C_QDIL, GROUP_W)))
    emit_dilated(kdil_refs, 2, rope_tok(tok(_C_KDIL, GROUP_W)))
    emit_dilated(vdil_refs, 4, tok(_C_VDIL, GROUP_W))

    xblk = tok(_C_X, 128)
    yblk = tok(_C_Y, 128)
    fg = xblk + bfg_ref[...]
    logf = jnp.minimum(fg, 0.0) - jnp.log1p(jnp.exp(-jnp.abs(fg)))
    a1 = logf.astype(bf16)
    r1 = logf - a1.astype(jnp.float32)
    a2 = r1.astype(bf16)
    a3 = (r1 - a2.astype(jnp.float32)).astype(bf16)
    tri = tri_ref[...]

    @pl.when(pl.program_id(0) % tiles_per_seq == 0)
    def _():
        carry_ref[...] = jnp.zeros_like(carry_ref)

    cum = _dot(tri, a1) + _dot(tri, a2) + _dot(tri, a3) + carry_ref[...]
    carry_ref[...] = cum[tm - 1:tm, :]
    cbase_ref[0] = cum[0:1, :]

    lane = lax.broadcasted_iota(jnp.int32, cum.shape, 1)
    is_gate = (lane >= _FG_LANE) & (lane < _FG_LANE + N_HEADS)
    dec = jnp.where(is_gate, (cum[0:1, :] - cum) * LOG2E, 0.0)
    d1 = dec.astype(bf16).astype(jnp.float32)
    d2 = (dec - d1).astype(bf16).astype(jnp.float32)
    d3 = (dec - d1 - d2).astype(bf16).astype(jnp.float32)
    dcols = d1 + pltpu.roll(d2, 8, 1) + pltpu.roll(d3, 16, 1)
    kf = tok(_C_KFOX, GROUP_W)
    kfox_ref[...] = jnp.concatenate([kf[:, :128], dcols, kf[:, 128:], dcols], axis=1).astype(bf16)

    zcq = tok(_C_CQ, 256)
    cq = (zcq * _rms_scale(zcq, MLA_Q_RANK) * gq_ref[...]).astype(bf16)
    qcT = _dot_nt(wqT_ref[...], cq)
    c16, s16 = cT16_ref[...], sT16_ref[...]
    r1h, r2h = qcT[256:320], qcT[320:384]
    qT = jnp.concatenate([qcT[0:256], r1h * c16 - r2h * s16, r2h * c16 + r1h * s16], axis=0)
    qTmla_ref[...] = (qT * ((MLA_NOPE + MLA_ROPE) ** -0.5 * LOG2E)).astype(bf16)

    zckv = tok(_C_CKV, MLA_KV_RANK)
    ckv = (zckv * _rms_scale(zckv, MLA_KV_RANK) * gkv_ref[...]).astype(bf16)
    kn = _dot(ckv, wkn_ref[...])
    krope = xblk * ct16_ref[...] + yblk * st16_ref[...]
    kmla_ref[...] = jnp.concatenate([kn[:, :128], krope, kn[:, 128:], krope], axis=1).astype(bf16)
    store_value_blocks(vTmla_ref, with_ones(_dot_nt(wvT_ref[...], ckv)))


def _softmax_update(parts, vb, m_ref, acc_ref, h, valid=None, offsets=None):
    if offsets is None:
        offsets = [None] * len(parts)
    m_old = m_ref[h:h + 1, :]
    blk_max = None
    for s, off in zip(parts, offsets):
        mx = jnp.max(s, axis=0, keepdims=True)
        mx = mx if off is None else mx + off
        blk_max = mx if blk_max is None else jnp.maximum(blk_max, mx)
    if valid is not None:
        blk_max = jnp.where(valid, blk_max, -jnp.inf)
    m_new = jnp.maximum(m_old, blk_max)
    ps = []
    for s, off in zip(parts, offsets):
        shift = m_new if off is None else m_new - off
        if valid is not None:
            shift = jnp.where(valid, shift, jnp.inf)
        ps.append(jnp.exp2(s - shift).astype(jnp.bfloat16))
    p = ps[0] if len(ps) == 1 else jnp.concatenate(ps, axis=0)
    acc_ref[h] = jnp.exp2(m_old - m_new) * acc_ref[h] + _dot(vb, p)
    m_ref[h:h + 1, :] = m_new


def _own_block_mask(i, tq, tk):
    key = lax.broadcasted_iota(jnp.int32, (tk, tq), 0)
    qry = lax.broadcasted_iota(jnp.int32, (tk, tq), 1) + (i % (tk // tq)) * tq
    return key <= qry


def _init_state(m_ref, acc_ref):
    m_ref[...] = jnp.full(m_ref.shape, -jnp.inf, jnp.float32)
    acc_ref[...] = jnp.zeros(acc_ref.shape, jnp.float32)


def _finalize(acc_ref, o_ref):
    outs = [acc_ref[h, 0:HEAD_DIM, :] / acc_ref[h, HEAD_DIM:HEAD_DIM + 1, :] for h in range(N_HEADS)]
    o_ref[...] = jnp.concatenate(outs, axis=0).T.astype(o_ref.dtype)


def _vblock(vT_ref, j, h):
    return vT_ref[j, h * V_ROWS:(h + 1) * V_ROWS, :]


def _run_key_blocks(own, last, sa_ref, sb_ref, scores_into, consume):
    scores_into(sa_ref, own)
    scores_into(sb_ref, 0)
    consume(sa_ref, own, True, None)

    def pair(p, carry):
        j0 = 2 * p
        scores_into(sa_ref, jnp.minimum(j0 + 1, last))
        consume(sb_ref, j0, False, None)
        scores_into(sb_ref, jnp.minimum(j0 + 2, last))
        consume(sa_ref, jnp.minimum(j0 + 1, own), False, j0 + 1 < own)
        return carry

    lax.fori_loop(0, (own + 1) // 2, pair, 0)


def _key_rows(k_ref, j, tk):
    return k_ref[pl.ds(pl.multiple_of(j * tk, tk), tk), :]


def _fox_kernel(qT_ref, k_ref, vT_ref, cbase_ref, o_ref, qcat_ref, m_ref, acc_ref, sa_ref, sb_ref):
    i = pl.program_id(1)
    tq = qT_ref.shape[1]
    zeros64 = jnp.zeros((HEAD_DIM, tq), jnp.bfloat16)
    row = lax.broadcasted_iota(jnp.int32, (128, tq), 0)
    for h in range(N_HEADS):
        qh = qT_ref[h * HEAD_DIM:(h + 1) * HEAD_DIM, :]
        pick = (row == _FG_LANE + h) | (row == _FG_LANE + 8 + h) | (row == _FG_LANE + 16 + h)
        rows = ([qh, zeros64] if h % 2 == 0 else [zeros64, qh]) + [pick.astype(jnp.bfloat16)]
        qcat_ref[:, h * tq:(h + 1) * tq] = jnp.concatenate(rows, axis=0)
    _init_state(m_ref, acc_ref)
    tk = sa_ref.shape[0]
    assert tq == tk
    causal = _own_block_mask(i, tq, tk)

    def scores_into(s_ref, j):
        kb = _key_rows(k_ref, j, tk)
        for half in range(2):
            cols = slice(2 * half * tq, 2 * (half + 1) * tq)
            s_ref[:, cols] = _dot(kb[:, 256 * half:256 * (half + 1)], qcat_ref[:, cols])

    def consume(s_ref, j, own, valid):
        offs = (cbase_ref[i] - cbase_ref[j]) * LOG2E
        for h in range(N_HEADS):
            s = s_ref[:, h * tq:(h + 1) * tq]
            if own:
                s = jnp.where(causal, s, -jnp.inf)
            _softmax_update([s], _vblock(vT_ref, j, h), m_ref, acc_ref, h, valid,
                            [offs[:, _FG_LANE + h:_FG_LANE + h + 1]])

    _run_key_blocks(i, k_ref.shape[0] // tk - 1, sa_ref, sb_ref, scores_into, consume)
    _finalize(acc_ref, o_ref)


def _mla_kernel(qT_ref, k_ref, vT_ref, o_ref, qcat_ref, m_ref, acc_ref, sa_ref, sb_ref):
    i = pl.program_id(1)
    tq = qT_ref.shape[1]
    zeros64 = jnp.zeros((MLA_NOPE, tq), jnp.bfloat16)
    zeros96 = jnp.zeros((256 - 128 - MLA_ROPE, tq), jnp.bfloat16)
    for h in range(N_HEADS):
        qn = qT_ref[h * MLA_NOPE:(h + 1) * MLA_NOPE, :]
        qr1 = qT_ref[256 + 16 * h:256 + 16 * (h + 1), :]
        qr2 = qT_ref[320 + 16 * h:320 + 16 * (h + 1), :]
        nope = [qn, zeros64] if h % 2 == 0 else [zeros64, qn]
        qcat_ref[:, h * tq:(h + 1) * tq] = jnp.concatenate(nope + [qr1, qr2, zeros96], axis=0)
    _init_state(m_ref, acc_ref)
    tk = sa_ref.shape[0]
    causal = _own_block_mask(i, tq, tk)

    def scores_into(s_ref, j):
        kb = _key_rows(k_ref, j, tk)
        for half in range(2):
            cols = slice(2 * half * tq, 2 * (half + 1) * tq)
            s_ref[:, cols] = _dot(kb[:, 256 * half:256 * (half + 1)], qcat_ref[:, cols])

    def consume(s_ref, j, own, valid):
        for h in range(N_HEADS):
            s = s_ref[:, h * tq:(h + 1) * tq]
            if own:
                s = jnp.where(causal, s, -jnp.inf)
            _softmax_update([s], _vblock(vT_ref, j, h), m_ref, acc_ref, h, valid)

    _run_key_blocks(i * tq // tk, k_ref.shape[0] // tk - 1, sa_ref, sb_ref, scores_into, consume)
    _finalize(acc_ref, o_ref)


def _moba_kernel(qT_ref, k_ref, vT_ref, km_ref, o_ref, qcat_ref, m_ref, acc_ref, sa_ref, sb_ref, bias_ref):
    i = pl.program_id(1)
    tq = qT_ref.shape[1]
    nb = km_ref.shape[1]
    q = qT_ref[...]
    row = lax.broadcasted_iota(jnp.int32, q.shape, 0)
    km = km_ref[0].astype(jnp.bfloat16)
    blk = lax.broadcasted_iota(jnp.int32, (nb, tq), 0)
    qblk = i * (tq // MOBA_BLOCK) + lax.broadcasted_iota(jnp.int32, (1, tq), 1) // MOBA_BLOCK
    neg_inf = jnp.float32(-jnp.inf)
    for h in range(N_HEADS):
        in_head = (((row >= 32 * h) & (row < 32 * (h + 1)))
                   | ((row >= 128 + 32 * h) & (row < 128 + 32 * (h + 1))))
        qm = jnp.where(in_head, q, jnp.zeros_like(q))
        qcat_ref[:, h * tq:(h + 1) * tq] = qm
        g = jnp.where(blk < qblk, _dot(km, qm), neg_inf)
        sel = jnp.zeros((nb, tq), jnp.bool_)
        for _ in range(MOBA_TOPK):
            mx = jnp.max(g, axis=0, keepdims=True)
            cand = jnp.where((g == mx) & (mx > neg_inf), blk, nb)
            chosen = blk == jnp.min(cand, axis=0, keepdims=True)
            sel = sel | chosen
            g = jnp.where(chosen, neg_inf, g)
        bias_ref[h * nb:(h + 1) * nb, :] = jnp.where(sel, 0.0, MASKED_LOGIT)
    _init_state(m_ref, acc_ref)
    tk = sa_ref.shape[0]
    per = tk // MOBA_BLOCK
    causal = _own_block_mask(i, tq, tk)

    def scores_into(s_ref, j):
        s_ref[...] = _dot(_key_rows(k_ref, j, tk), qcat_ref[...])

    def consume(s_ref, j, own, valid):
        for h in range(N_HEADS):
            parts, gates = [], []
            for sub in range(per):
                n = per * j + sub
                gate = bias_ref[pl.ds(h * nb + n, 1), :]
                rows = slice(sub * MOBA_BLOCK, (sub + 1) * MOBA_BLOCK)
                s = s_ref[rows, h * tq:(h + 1) * tq]
                if own:
                    gate = jnp.where(n < qblk, gate, 0.0)
                    s = jnp.where(causal[rows], s, neg_inf)
                parts.append(s)
                gates.append(gate)
            _softmax_update(parts, _vblock(vT_ref, j, h), m_ref, acc_ref, h, valid, gates)

    _run_key_blocks(i * tq // tk, k_ref.shape[0] // tk - 1, sa_ref, sb_ref, scores_into, consume)
    _finalize(acc_ref, o_ref)


def _dilated_kernel(q_ref, k_ref, v_ref, o_ref, lse_ref, *, window):
    i = pl.program_id(2)
    tq = q_ref.shape[1]
    n = k_ref.shape[1]
    q = q_ref[0]
    a = i * tq
    ks = jnp.clip(a - DIL_WINDOW_STEPS, 0, n - window)
    ks = pl.multiple_of(ks, DIL_WINDOW_STEPS)
    kw = k_ref[0, pl.ds(ks, window), :]
    vw = v_ref[0, pl.ds(ks, window), :]
    jq = a + lax.broadcasted_iota(jnp.int32, (tq, window), 0)
    jk = ks + lax.broadcasted_iota(jnp.int32, (tq, window), 1)
    band = (jq - jk >= 0) & (jq - jk <= DIL_WINDOW_STEPS)
    lane = lax.broadcasted_iota(jnp.int32, (tq, GROUP_W), 1)
    o = jnp.zeros((tq, GROUP_W), jnp.float32)
    lse = jnp.zeros((tq, GROUP_W), jnp.float32)
    for h in range(N_HEADS):
        in_head = (((lane >= 32 * h) & (lane < 32 * (h + 1)))
                   | ((lane >= 128 + 32 * h) & (lane < 128 + 32 * (h + 1))))
        qm = jnp.where(in_head, q, jnp.zeros_like(q))
        s = jnp.where(band, _dot_nt(qm, kw), -jnp.inf)
        m = jnp.max(s, axis=1, keepdims=True)
        p = jnp.exp2(s - m)
        l = jnp.sum(p, axis=1, keepdims=True)
        oh = _dot(p.astype(jnp.bfloat16), vw) / l
        out_lanes = (lane >= h * HEAD_DIM) & (lane < (h + 1) * HEAD_DIM)
        o = jnp.where(out_lanes, oh, o)
        lse = jnp.where(out_lanes, m + jnp.log2(l), lse)
    o_ref[0] = o
    lse_ref[0] = lse


def _out_ffn_kernel(x_ref, omoba_ref, ofox_ref, omla_ref, od1_ref, od4_ref, od16_ref, l1_ref, l4_ref, l16_ref,
                    wout_ref, gpost_ref, gpre_ref, wg_ref, wu_ref, wd_ref, gpf_ref, out_ref, acc_ref,
                    dscr_ref):
    bf16 = jnp.bfloat16
    tm = x_ref.shape[0]

    def token_order(ref, slot, dil):
        halves = GROUP_W // 128
        for rho in range(dil):
            for c in range(halves):
                lanes = slice(rho * GROUP_W + c * 128, rho * GROUP_W + (c + 1) * 128)
                dscr_ref[slot + c, pl.ds(rho, tm // dil, stride=dil), :] = ref[:, lanes]
        return jnp.concatenate([dscr_ref[slot + c] for c in range(halves)], axis=1)

    l1, od1 = l1_ref[...], od1_ref[...]
    l4, od4 = token_order(l4_ref, 0, 4), token_order(od4_ref, 2, 4)
    l16, od16 = token_order(l16_ref, 4, 16), token_order(od16_ref, 6, 16)
    m = jnp.maximum(jnp.maximum(l1, l4), l16)
    e1, e4, e16 = jnp.exp2(l1 - m), jnp.exp2(l4 - m), jnp.exp2(l16 - m)
    odil = (e1 * od1 + e4 * od4 + e16 * od16) / (e1 + e4 + e16)
    y = (_dot(omoba_ref[...], wout_ref[0]) + _dot(ofox_ref[...], wout_ref[1])
         + _dot(omla_ref[...], wout_ref[2]) + _dot(odil.astype(bf16), wout_ref[3]))
    x1 = x_ref[...] + y * _rms_scale(y, D_MODEL) * gpost_ref[...]
    hb = (x1 * _rms_scale(x1, D_MODEL) * gpre_ref[...]).astype(bf16)
    acc_ref[...] = jnp.zeros_like(acc_ref)

    def chunk(c, _):
        g = _dot(hb, wg_ref[c])
        u = _dot(hb, wu_ref[c])
        f = (g * jax.nn.sigmoid(g) * u).astype(bf16)
        acc_ref[...] += _dot(f, wd_ref[c])
        return 0

    lax.fori_loop(0, wg_ref.shape[0], chunk, 0, unroll=True)
    f = acc_ref[...]
    out_ref[...] = x1 + f * _rms_scale(f, D_MODEL) * gpf_ref[...]


def _rope_tables(positions):
    t = positions.reshape(-1).astype(jnp.float32)

    def tab(dim):
        inv = ROPE_THETA ** (-jnp.arange(0, dim, 2, dtype=jnp.float32) / dim)
        ang = t[:, None] * inv
        return jnp.cos(ang), jnp.sin(ang)

    c64, s64 = tab(HEAD_DIM)
    c16, s16 = tab(MLA_ROPE)
    ct64, st64 = jnp.tile(c64, (1, 4)), jnp.tile(s64, (1, 4))
    pad = jnp.zeros((t.shape[0], 128 - MLA_ROPE), jnp.float32)
    ct16 = jnp.concatenate([c16, c16, pad], axis=1)
    st16 = jnp.concatenate([s16, s16, pad], axis=1)
    cT16, sT16 = jnp.tile(c16, (1, 4)).T, jnp.tile(s16, (1, 4)).T
    return ct64, st64, ct64.T, st64.T, ct16, st16, cT16, sT16


_HALF_PERM = np.array([h * 64 + half * 32 + j for half in (0, 1) for h in range(4) for j in range(32)])
_QROPE_ROWS = np.array([h * 96 + 64 + half * 16 + j for half in (0, 1) for h in range(4) for j in range(16)])
_QNOPE_ROWS = np.array([h * 96 + j for h in range(4) for j in range(64)])
_KNOPE_COLS = np.array([h * 128 + j for h in range(4) for j in range(64)])
_VMLA_COLS = np.array([h * 128 + 64 + j for h in range(4) for j in range(64)])


def _prep_layer_weights(w_in, b_forget, g_mla_q, w_mla_q_up, g_mla_kv, w_mla_kv_up, w_out, w_gate, w_up, w_down):
    bf16 = jnp.bfloat16
    depth = w_in.shape[0]
    sl = lambda a, b: w_in[:, :, a:b]
    scale = HEAD_DIM ** -0.5 * LOG2E
    moba_q, moba_k, moba_v = sl(0, 256)[..., _HALF_PERM] * scale, sl(256, 512)[..., _HALF_PERM], sl(512, 768)
    fox_q, fox_k, fox_v = sl(768, 1024) * scale, sl(1024, 1280), sl(1280, 1536)
    fg, cq, ckv, kr = sl(1536, 1540), sl(1540, 1732), sl(1732, 1860), sl(1860, 1892)
    dil_q, dil_k, dil_v = sl(1892, 2148)[..., _HALF_PERM] * scale, sl(2148, 2404)[..., _HALF_PERM], sl(2404, 2660)
    kr_rot = jnp.concatenate([-kr[..., 16:], kr[..., :16]], axis=-1)
    z = lambda n: jnp.zeros((depth, D_MODEL, n), w_in.dtype)
    wtok = jnp.concatenate([moba_k, fox_k, dil_q, dil_k, dil_v, cq, z(256 - MLA_Q_RANK), ckv,
                            kr, fg, z(128 - 36), kr_rot, z(96)], axis=-1).astype(bf16)
    wtr = jnp.swapaxes(jnp.concatenate([moba_q, moba_v, fox_q, fox_v], axis=-1), 1, 2).astype(bf16)
    bfg = jnp.zeros((depth, 1, 128), jnp.float32).at[:, 0, _FG_LANE:_FG_LANE + N_HEADS].set(b_forget)
    gq = jnp.pad(g_mla_q, ((0, 0), (0, 256 - MLA_Q_RANK)))[:, None, :]
    wq_rows = jnp.swapaxes(w_mla_q_up, 1, 2)
    wqT = jnp.concatenate([wq_rows[:, _QNOPE_ROWS], wq_rows[:, _QROPE_ROWS]], axis=1)
    wqT = jnp.pad(wqT, ((0, 0), (0, 0), (0, 256 - MLA_Q_RANK))).astype(bf16)
    gkv = g_mla_kv[:, None, :]
    wkn = w_mla_kv_up[:, :, _KNOPE_COLS].astype(bf16)
    wvT = jnp.swapaxes(w_mla_kv_up[:, :, _VMLA_COLS], 1, 2).astype(bf16)
    wout = w_out.reshape(depth, 4, GROUP_W, D_MODEL).astype(bf16)
    nc = D_FF // FF_CHUNK
    wg = jnp.swapaxes(w_gate.reshape(depth, D_MODEL, nc, FF_CHUNK), 1, 2).astype(bf16)
    wu = jnp.swapaxes(w_up.reshape(depth, D_MODEL, nc, FF_CHUNK), 1, 2).astype(bf16)
    wd = w_down.reshape(depth, nc, FF_CHUNK, D_MODEL).astype(bf16)
    return wtok, wtr, bfg, gq, wqT, gkv, wkn, wvT, wout, wg, wu, wd


def _const_spec(shape):
    return pl.BlockSpec(shape, lambda *_: (0,) * len(shape))


def _params(*sem):
    return pltpu.CompilerParams(dimension_semantics=sem, vmem_limit_bytes=VMEM_LIMIT)


def _in_proj(x2, g, wtok, wtr, tri, tables, bfg, gq, wqT, gkv, wkn, wvT, seq):
    t = x2.shape[0]
    tm = ROW_TILE
    nt = t // tm
    bf16, f32 = jnp.bfloat16, jnp.float32
    ct64, st64, cT64, sT64, ct16, st16, cT16, sT16 = tables
    tok_spec = lambda w: pl.BlockSpec((tm, w), lambda i: (i, 0))
    tr_spec = lambda r: pl.BlockSpec((r, tm), lambda i: (0, i))
    blk3 = lambda n, r, c: pl.BlockSpec((n, r, c), lambda i: (i, 0, 0))
    mb = tm // MOBA_BLOCK
    ab = tm // KEY_BLOCK
    vrows = N_HEADS * V_ROWS
    vt_shape = jax.ShapeDtypeStruct((t // KEY_BLOCK, vrows, KEY_BLOCK), bf16)
    in_specs = [tok_spec(D_MODEL), _const_spec((1, D_MODEL)), _const_spec(wtok.shape), _const_spec(wtr.shape),
                _const_spec(tri.shape), tok_spec(128), tok_spec(128), tr_spec(128), tr_spec(128),
                tok_spec(128), tok_spec(128), tr_spec(64), tr_spec(64), _const_spec((1, 128)),
                _const_spec((1, 256)), _const_spec(wqT.shape), _const_spec((1, 128)), _const_spec(wkn.shape),
                _const_spec(wvT.shape)]
    out_shape = [
        jax.ShapeDtypeStruct((t, GROUP_W), bf16),
        jax.ShapeDtypeStruct((nt, mb, GROUP_W), f32),
        jax.ShapeDtypeStruct((GROUP_W, t), bf16),
        vt_shape,
        jax.ShapeDtypeStruct((t, 512), bf16),
        jax.ShapeDtypeStruct((GROUP_W, t), bf16),
        vt_shape,
        jax.ShapeDtypeStruct((nt, 1, 128), f32),
        *[jax.ShapeDtypeStruct((t // d, d * GROUP_W), bf16) for d in DILATIONS] * 3,
        jax.ShapeDtypeStruct((t, 512), bf16),
        jax.ShapeDtypeStruct((384, t), bf16),
        vt_shape,
    ]
    vt_spec = blk3(ab, vrows, KEY_BLOCK)
    out_specs = [tok_spec(GROUP_W), blk3(1, mb, GROUP_W), tr_spec(GROUP_W), vt_spec,
                 tok_spec(512), tr_spec(GROUP_W), vt_spec, blk3(1, 1, 128),
                 *[pl.BlockSpec((tm // d, d * GROUP_W), lambda i: (i, 0)) for d in DILATIONS] * 3,
                 tok_spec(512), tr_spec(384), vt_spec]
    return pl.pallas_call(
        functools.partial(_in_proj_kernel, tiles_per_seq=seq // tm),
        grid=(nt,), in_specs=in_specs, out_specs=out_specs, out_shape=out_shape,
        scratch_shapes=[pltpu.VMEM((1, 128), f32), pltpu.VMEM((3 * GROUP_W // 128, tm, 128), f32)],
        compiler_params=_params("arbitrary"), name="in_proj",
    )(x2, g, wtok, wtr, tri, ct64, st64, cT64, sT64, ct16, st16, cT16, sT16, bfg, gq, wqT, gkv, wkn, wvT)


def _attention_scratch(tq, tk=KEY_BLOCK):
    return [pltpu.VMEM((GROUP_W, N_HEADS * tq), jnp.bfloat16),
            pltpu.VMEM((8, tq), jnp.float32),
            pltpu.VMEM((N_HEADS, V_ROWS, tq), jnp.float32),
            pltpu.VMEM((tk, N_HEADS * tq), jnp.float32),
            pltpu.VMEM((tk, N_HEADS * tq), jnp.float32)]


def _dense_attention(body, name, qT, k, vT, extra, batch, seq):
    tq = ATT_TILE
    nq = seq // tq
    t = batch * seq
    once = pl.Buffered(1)
    in_specs = [pl.BlockSpec((qT.shape[0], tq), lambda b, i: (0, b * nq + i)),
                pl.BlockSpec((seq, k.shape[1]), lambda b, i: (b, 0), pipeline_mode=once),
                pl.BlockSpec((seq // KEY_BLOCK, N_HEADS * V_ROWS, KEY_BLOCK), lambda b, i: (b, 0, 0),
                             pipeline_mode=once)]
    in_specs += [pl.BlockSpec((seq // KEY_BLOCK, 1, 128), lambda b, i: (b, 0, 0)) for _ in extra]
    return pl.pallas_call(
        body, grid=(batch, nq), in_specs=in_specs,
        out_specs=pl.BlockSpec((tq, GROUP_W), lambda b, i: (b * nq + i, 0)),
        out_shape=jax.ShapeDtypeStruct((t, GROUP_W), jnp.bfloat16),
        scratch_shapes=_attention_scratch(tq),
        compiler_params=_params("arbitrary", "arbitrary"), name=name,
    )(qT, k, vT, *extra)


def _moba_attention(qT, k, vT, kmean, batch, seq):
    tq = ATT_TILE
    nq = seq // tq
    nb = seq // MOBA_BLOCK
    t = batch * seq
    return pl.pallas_call(
        _moba_kernel, grid=(batch, nq),
        in_specs=[pl.BlockSpec((GROUP_W, tq), lambda b, i: (0, b * nq + i)),
                  pl.BlockSpec((seq, GROUP_W), lambda b, i: (b, 0), pipeline_mode=pl.Buffered(1)),
                  pl.BlockSpec((seq // KEY_BLOCK, N_HEADS * V_ROWS, KEY_BLOCK), lambda b, i: (b, 0, 0),
                               pipeline_mode=pl.Buffered(1)),
                  pl.BlockSpec((1, nb, GROUP_W), lambda b, i: (b, 0, 0))],
        out_specs=pl.BlockSpec((tq, GROUP_W), lambda b, i: (b * nq + i, 0)),
        out_shape=jax.ShapeDtypeStruct((t, GROUP_W), jnp.bfloat16),
        scratch_shapes=_attention_scratch(tq) + [pltpu.VMEM((N_HEADS * nb, tq), jnp.float32)],
        compiler_params=_params("arbitrary", "arbitrary"), name="moba_attention",
    )(qT, k, vT, kmean)


def _dilated_attention(q, k, v, batch, seq, dil):
    n = seq // dil
    tq = min(DIL_TILE, n)
    window = min(tq + DIL_WINDOW_STEPS, n)
    view = lambda a: a.reshape(batch, n, dil * GROUP_W)
    qspec = pl.BlockSpec((1, tq, GROUP_W), lambda b, r, i: (b, i, r))
    kspec = pl.BlockSpec((1, n, GROUP_W), lambda b, r, i: (b, 0, r))
    shape = jax.ShapeDtypeStruct((batch, n, dil * GROUP_W), jnp.float32)
    o, lse = pl.pallas_call(
        functools.partial(_dilated_kernel, window=window), grid=(batch, dil, n // tq),
        in_specs=[qspec, kspec, kspec], out_specs=[qspec, qspec], out_shape=[shape, shape],
        compiler_params=_params("arbitrary", "arbitrary", "arbitrary"), name=f"dilated_attention_d{dil}",
    )(view(q), view(k), view(v))
    return o.reshape(batch * n, dil * GROUP_W), lse.reshape(batch * n, dil * GROUP_W)


def _out_ffn(x2, omoba, ofox, omla, dil_outs, wout, gpost, gpre, wg, wu, wd, gpf):
    t = x2.shape[0]
    tm = ROW_TILE
    row = lambda w: pl.BlockSpec((tm, w), lambda i: (i, 0))
    (od1, l1), (od4, l4), (od16, l16) = dil_outs
    dil_specs = [pl.BlockSpec((tm // d, d * GROUP_W), lambda i: (i, 0)) for d in DILATIONS]
    in_specs = ([row(D_MODEL)] + [row(GROUP_W)] * 3 + dil_specs * 2
                + [_const_spec(wout.shape), _const_spec((1, D_MODEL)), _const_spec((1, D_MODEL)),
                   _const_spec(wg.shape), _const_spec(wu.shape), _const_spec(wd.shape), _const_spec((1, D_MODEL))])
    return pl.pallas_call(
        _out_ffn_kernel, grid=(t // tm,), in_specs=in_specs, out_specs=row(D_MODEL),
        out_shape=jax.ShapeDtypeStruct((t, D_MODEL), jnp.float32),
        scratch_shapes=[pltpu.VMEM((tm, D_MODEL), jnp.float32), pltpu.VMEM((4 * GROUP_W // 128, tm, 128), jnp.float32)],
        compiler_params=_params("arbitrary"), name="out_ffn",
    )(x2, omoba, ofox, omla, od1, od4, od16, l1, l4, l16, wout, gpost, gpre, wg, wu, wd, gpf)


def kernel(x, positions, w_in, b_forget, g_mla_q, w_mla_q_up, g_mla_kv, w_mla_kv_up, w_out, g_pre_mix, g_post_mix, w_gate, w_up, w_down, g_pre_ffn, g_post_ffn):
    batch, seq, _ = x.shape
    depth = w_in.shape[0]
    assert seq % ROW_TILE == 0 and seq % (DILATIONS[-1] * DIL_WINDOW_STEPS) == 0
    assert ROW_TILE == KEY_BLOCK
    tables = _rope_tables(positions)
    wtok, wtr, bfg, gq, wqT, gkv, wkn, wvT, wout, wg, wu, wd = _prep_layer_weights(
        w_in, b_forget, g_mla_q, w_mla_q_up, g_mla_kv, w_mla_kv_up, w_out, w_gate, w_up, w_down)
    tri = jnp.tril(jnp.ones((ROW_TILE, ROW_TILE), jnp.bfloat16))
    x2 = x.reshape(batch * seq, D_MODEL)
    for l in range(depth):
        (k_moba, kmean, qT_moba, vT_moba, k_fox, qT_fox, vT_fox, cbase, *dil_qkv,
         k_mla, qT_mla, vT_mla) = _in_proj(x2, g_pre_mix[l][None], wtok[l], wtr[l], tri, tables, bfg[l], gq[l],
                                          wqT[l], gkv[l], wkn[l], wvT[l], seq)
        nd = len(DILATIONS)
        o_moba = _moba_attention(qT_moba, k_moba, vT_moba, kmean.reshape(batch, seq // MOBA_BLOCK, GROUP_W),
                                 batch, seq)
        o_fox = _dense_attention(_fox_kernel, "fox_attention", qT_fox, k_fox, vT_fox, [cbase], batch, seq)
        o_mla = _dense_attention(_mla_kernel, "mla_attention", qT_mla, k_mla, vT_mla, [], batch, seq)
        dil_outs = [_dilated_attention(dil_qkv[n], dil_qkv[nd + n], dil_qkv[2 * nd + n], batch, seq, d)
                    for n, d in enumerate(DILATIONS)]
        x2 = _out_ffn(x2, o_moba, o_fox, o_mla, dil_outs, wout[l], g_post_mix[l][None], g_pre_ffn[l][None],
                      wg[l], wu[l], wd[l], g_post_ffn[l][None])
    return x2.reshape(batch, seq, D_MODEL)
```

```python
import functools

import numpy as np
import jax
import jax.numpy as jnp
from jax import lax
from jax.experimental import pallas as pl
from jax.experimental.pallas import tpu as pltpu

D_MODEL = 1024
HEAD_DIM = 64
N_HEADS = 4
GROUP_W = N_HEADS * HEAD_DIM
MOBA_BLOCK = 256
MOBA_TOPK = 3
MLA_Q_RANK = 192
MLA_KV_RANK = 128
MLA_NOPE = 64
MLA_ROPE = 32
DIL_WINDOW_STEPS = 128
DILATIONS = (1, 4, 16)
ROPE_THETA = 10000.0
RMS_EPS = 1e-6
D_FF = 2816
FF_CHUNK = 256
V_ROWS = 80
LOG2E = 1.4426950408889634
MASKED_LOGIT = -1e30

ROW_TILE = 512
ATT_TILE = 512
KEY_BLOCK = 512
SLAB = MOBA_BLOCK
DIL_TILE = 256
VMEM_LIMIT = 56 * 1024 * 1024

_C_KMOBA, _C_KFOX, _C_QDIL, _C_KDIL, _C_VDIL, _C_CQ = 0, 256, 512, 768, 1024, 1280
_C_CKV, _C_X, _C_Y, _N_TOK = 1536, 1664, 1792, 1920
_FG_LANE = 32

_NT = (((1,), (1,)), ((), ()))


def _dot(a, b):
    return jnp.dot(a, b, preferred_element_type=jnp.float32)


def _dot_nt(a, b):
    return lax.dot_general(a, b, _NT, preferred_element_type=jnp.float32)


def _rms_scale(v, n):
    return lax.rsqrt(jnp.sum(v * v, axis=-1, keepdims=True) * (1.0 / n) + RMS_EPS)


def _in_proj_kernel(x_ref, g_ref, wtok_ref, wtr_ref, tri_ref, ct64_ref, st64_ref, cT64_ref, sT64_ref,
                    ct16_ref, st16_ref, cT16_ref, sT16_ref, bfg_ref, gq_ref, wqT_ref, gkv_ref, wkn_ref, wvT_ref,
                    kmoba_ref, kmean_ref, qTmoba_ref, vTmoba_ref, kfox_ref, qTfox_ref, vTfox_ref, cbase_ref,
                    q1_ref, q4_ref, q16_ref, k1_ref, k4_ref, k16_ref, v1_ref, v4_ref, v16_ref,
                    kmla_ref, qTmla_ref, vTmla_ref, carry_ref, dscr_ref, *, tiles_per_seq):
    bf16 = jnp.bfloat16
    qdil_refs, kdil_refs, vdil_refs = (q1_ref, q4_ref, q16_ref), (k1_ref, k4_ref, k16_ref), (v1_ref, v4_ref, v16_ref)
    x = x_ref[...]
    hb = (x * _rms_scale(x, D_MODEL) * g_ref[...]).astype(bf16)
    tm = x.shape[0]

    def tok(c0, w):
        return _dot(hb, wtok_ref[:, c0:c0 + w])

    def rope_tok(z):
        c, s = ct64_ref[...], st64_ref[...]
        x1, x2 = z[:, :128], z[:, 128:]
        return jnp.concatenate([x1 * c - x2 * s, x2 * c + x1 * s], axis=1)

    def with_ones(vT):
        ones = jnp.ones((V_ROWS - HEAD_DIM, tm), bf16)
        parts = []
        for h in range(N_HEADS):
            parts += [vT[h * HEAD_DIM:(h + 1) * HEAD_DIM].astype(bf16), ones]
        return jnp.concatenate(parts, axis=0)

    def rope_tr(zT):
        c, s = cT64_ref[...], sT64_ref[...]
        x1, x2 = zT[:128], zT[128:]
        return jnp.concatenate([x1 * c - x2 * s, x2 * c + x1 * s], axis=0)

    k_moba = rope_tok(tok(_C_KMOBA, GROUP_W))
    kmoba_ref[...] = k_moba.astype(bf16)
    for blk in range(tm // MOBA_BLOCK):
        kmean_ref[0, blk:blk + 1, :] = jnp.mean(k_moba[blk * MOBA_BLOCK:(blk + 1) * MOBA_BLOCK], axis=0, keepdims=True)
    qTmoba_ref[...] = rope_tr(_dot_nt(wtr_ref[0:256, :], hb)).astype(bf16)
    def store_value_blocks(ref, vT):
        for blk in range(tm // KEY_BLOCK):
            ref[blk] = vT[:, blk * KEY_BLOCK:(blk + 1) * KEY_BLOCK]

    store_value_blocks(vTmoba_ref, with_ones(_dot_nt(wtr_ref[256:512, :], hb)))

    qTfox_ref[...] = _dot_nt(wtr_ref[512:768, :], hb).astype(bf16)
    store_value_blocks(vTfox_ref, with_ones(_dot_nt(wtr_ref[768:1024, :], hb)))

    def emit_dilated(refs, slot, z):
        refs[0][...] = z.astype(bf16)
        for c in range(GROUP_W // 128):
            dscr_ref[slot + c] = z[:, c * 128:(c + 1) * 128]
        for ref, dil in zip(refs[1:], DILATIONS[1:]):
            for rho in range(dil):
                for c in range(GROUP_W // 128):
                    rows = dscr_ref[slot + c, pl.ds(rho, tm // dil, stride=dil), :]
                    ref[:, rho * GROUP_W + c * 128:rho * GROUP_W + (c + 1) * 128] = rows.astype(bf16)

    emit_dilated(qdil_refs, 0, rope_tok(tok(_C_QDIL, GROUP_W)))
    emit_dilated(kdil_refs, 2, rope_tok(tok(_C_KDIL, GROUP_W)))
    emit_dilated(vdil_refs, 4, tok(_C_VDIL, GROUP_W))

    xblk = tok(_C_X, 128)
    yblk = tok(_C_Y, 128)
    fg = xblk + bfg_ref[...]
    logf = jnp.minimum(fg, 0.0) - jnp.log1p(jnp.exp(-jnp.abs(fg)))
    a1 = logf.astype(bf16)
    r1 = logf - a1.astype(jnp.float32)
    a2 = r1.astype(bf16)
    a3 = (r1 - a2.astype(jnp.float32)).astype(bf16)
    tri = tri_ref[...]

    @pl.when(pl.program_id(0) % tiles_per_seq == 0)
    def _():
        carry_ref[...] = jnp.zeros_like(carry_ref)

    cum = _dot(tri, a1) + _dot(tri, a2) + _dot(tri, a3) + carry_ref[...]
    carry_ref[...] = cum[tm - 1:tm, :]
    cbase_ref[0] = cum[0:1, :]

    lane = lax.broadcasted_iota(jnp.int32, cum.shape, 1)
    is_gate = (lane >= _FG_LANE) & (lane < _FG_LANE + N_HEADS)
    dec = jnp.where(is_gate, (cum[0:1, :] - cum) * LOG2E, 0.0)
    d1 = dec.astype(bf16).astype(jnp.float32)
    d2 = (dec - d1).astype(bf16).astype(jnp.float32)
    d3 = (dec - d1 - d2).astype(bf16).astype(jnp.float32)
    dcols = d1 + pltpu.roll(d2, 8, 1) + pltpu.roll(d3, 16, 1)
    kf = tok(_C_KFOX, GROUP_W)
    kfox_ref[...] = jnp.concatenate([kf[:, :128], dcols, kf[:, 128:], dcols], axis=1).astype(bf16)

    zcq = tok(_C_CQ, 256)
    cq = (zcq * _rms_scale(zcq, MLA_Q_RANK) * gq_ref[...]).astype(bf16)
    qcT = _dot_nt(wqT_ref[...], cq)
    c16, s16 = cT16_ref[...], sT16_ref[...]
    r1h, r2h = qcT[256:320], qcT[320:384]
    qT = jnp.concatenate([qcT[0:256], r1h * c16 - r2h * s16, r2h * c16 + r1h * s16], axis=0)
    qTmla_ref[...] = (qT * ((MLA_NOPE + MLA_ROPE) ** -0.5 * LOG2E)).astype(bf16)

    zckv = tok(_C_CKV, MLA_KV_RANK)
    ckv = (zckv * _rms_scale(zckv, MLA_KV_RANK) * gkv_ref[...]).astype(bf16)
    kn = _dot(ckv, wkn_ref[...])
    krope = xblk * ct16_ref[...] + yblk * st16_ref[...]
    kmla_ref[...] = jnp.concatenate([kn[:, :128], krope, kn[:, 128:], krope], axis=1).astype(bf16)
    store_value_blocks(vTmla_ref, with_ones(_dot_nt(wvT_ref[...], ckv)))


def _softmax_update(parts, vb, m_ref, acc_ref, h, offsets=None, maxes=None):
    if offsets is None:
        offsets = [None] * len(parts)
    if maxes is None:
        maxes = [jnp.max(s, axis=0, keepdims=True) for s in parts]
    m_old = m_ref[h:h + 1, :]
    blk_max = None
    for mx, off in zip(maxes, offsets):
        mx = mx if off is None else mx + off
        blk_max = mx if blk_max is None else jnp.maximum(blk_max, mx)
    m_new = jnp.maximum(m_old, blk_max)
    ps = []
    for s, off in zip(parts, offsets):
        shift = m_new if off is None else m_new - off
        ps.append(jnp.exp2(s - shift).astype(jnp.bfloat16))
    p = ps[0] if len(ps) == 1 else jnp.concatenate(ps, axis=0)
    acc_ref[h] = jnp.exp2(m_old - m_new) * acc_ref[h] + _dot(vb, p)
    m_ref[h:h + 1, :] = m_new


def _store_scores(buf, cols, s):
    s_ref, mx_ref = buf
    s_ref[:, cols] = s
    for n in range(s.shape[0] // SLAB):
        mx_ref[n:n + 1, cols] = jnp.max(s[n * SLAB:(n + 1) * SLAB], axis=0, keepdims=True)


def _slabs(buf, h, tq):
    s_ref, mx_ref = buf
    cols = slice(h * tq, (h + 1) * tq)
    n_slabs = s_ref.shape[0] // SLAB
    return ([s_ref[n * SLAB:(n + 1) * SLAB, cols] for n in range(n_slabs)],
            [mx_ref[n:n + 1, cols] for n in range(n_slabs)])


def _own_block_mask(i, tq, tk):
    key = lax.broadcasted_iota(jnp.int32, (tk, tq), 0)
    qry = lax.broadcasted_iota(jnp.int32, (tk, tq), 1) + (i % (tk // tq)) * tq
    return key <= qry


def _init_state(m_ref, acc_ref):
    m_ref[...] = jnp.full(m_ref.shape, -jnp.inf, jnp.float32)
    acc_ref[...] = jnp.zeros(acc_ref.shape, jnp.float32)


def _finalize(acc_ref, o_ref):
    outs = [acc_ref[h, 0:HEAD_DIM, :] / acc_ref[h, HEAD_DIM:HEAD_DIM + 1, :] for h in range(N_HEADS)]
    o_ref[...] = jnp.concatenate(outs, axis=0).T.astype(o_ref.dtype)


def _vblock(vT_ref, j, h):
    return vT_ref[j, h * V_ROWS:(h + 1) * V_ROWS, :]


def _run_key_blocks(own, last, buf_a, buf_b, scores_into, consume):
    scores_into(buf_a, own)
    scores_into(buf_b, 0)
    consume(buf_a, own, True)

    def pair(p, carry):
        j0 = 2 * p
        scores_into(buf_a, j0 + 1)
        consume(buf_b, j0, False)
        scores_into(buf_b, jnp.minimum(j0 + 2, last))
        consume(buf_a, j0 + 1, False)
        return carry

    lax.fori_loop(0, own // 2, pair, 0)

    @pl.when(own % 2 == 1)
    def _():
        consume(buf_b, own - 1, False)


def _key_rows(k_ref, j, tk):
    return k_ref[pl.ds(pl.multiple_of(j * tk, tk), tk), :]


def _causal_slabs(parts, causal):
    return [jnp.where(causal[n * SLAB:(n + 1) * SLAB], s, -jnp.inf) for n, s in enumerate(parts)]


def _two_tile_scores(buf, k_ref, qcat_ref, j, tk, tq):
    kb = _key_rows(k_ref, j, tk)
    for half in range(2):
        cols = slice(2 * half * tq, 2 * (half + 1) * tq)
        _store_scores(buf, cols, _dot(kb[:, 256 * half:256 * (half + 1)], qcat_ref[:, cols]))


def _fox_kernel(qT_ref, k_ref, vT_ref, cbase_ref, o_ref, qcat_ref, m_ref, acc_ref, sa_ref, sb_ref, mxa_ref, mxb_ref):
    i = pl.program_id(1)
    tq = qT_ref.shape[1]
    zeros64 = jnp.zeros((HEAD_DIM, tq), jnp.bfloat16)
    row = lax.broadcasted_iota(jnp.int32, (128, tq), 0)
    for h in range(N_HEADS):
        qh = qT_ref[h * HEAD_DIM:(h + 1) * HEAD_DIM, :]
        pick = (row == _FG_LANE + h) | (row == _FG_LANE + 8 + h) | (row == _FG_LANE + 16 + h)
        rows = ([qh, zeros64] if h % 2 == 0 else [zeros64, qh]) + [pick.astype(jnp.bfloat16)]
        qcat_ref[:, h * tq:(h + 1) * tq] = jnp.concatenate(rows, axis=0)
    _init_state(m_ref, acc_ref)
    tk = sa_ref.shape[0]
    assert tq == tk
    causal = _own_block_mask(i, tq, tk)

    def scores_into(buf, j):
        _two_tile_scores(buf, k_ref, qcat_ref, j, tk, tq)

    def consume(buf, j, own):
        offs = (cbase_ref[i] - cbase_ref[j]) * LOG2E
        for h in range(N_HEADS):
            parts, maxes = _slabs(buf, h, tq)
            if own:
                parts, maxes = _causal_slabs(parts, causal), None
            off = offs[:, _FG_LANE + h:_FG_LANE + h + 1]
            _softmax_update(parts, _vblock(vT_ref, j, h), m_ref, acc_ref, h, [off] * len(parts), maxes)

    _run_key_blocks(i, k_ref.shape[0] // tk - 1, (sa_ref, mxa_ref), (sb_ref, mxb_ref), scores_into, consume)
    _finalize(acc_ref, o_ref)


def _mla_kernel(qT_ref, k_ref, vT_ref, o_ref, qcat_ref, m_ref, acc_ref, sa_ref, sb_ref, mxa_ref, mxb_ref):
    i = pl.program_id(1)
    tq = qT_ref.shape[1]
    zeros64 = jnp.zeros((MLA_NOPE, tq), jnp.bfloat16)
    zeros96 = jnp.zeros((256 - 128 - MLA_ROPE, tq), jnp.bfloat16)
    for h in range(N_HEADS):
        qn = qT_ref[h * MLA_NOPE:(h + 1) * MLA_NOPE, :]
        qr1 = qT_ref[256 + 16 * h:256 + 16 * (h + 1), :]
        qr2 = qT_ref[320 + 16 * h:320 + 16 * (h + 1), :]
        nope = [qn, zeros64] if h % 2 == 0 else [zeros64, qn]
        qcat_ref[:, h * tq:(h + 1) * tq] = jnp.concatenate(nope + [qr1, qr2, zeros96], axis=0)
    _init_state(m_ref, acc_ref)
    tk = sa_ref.shape[0]
    causal = _own_block_mask(i, tq, tk)

    def scores_into(buf, j):
        _two_tile_scores(buf, k_ref, qcat_ref, j, tk, tq)

    def consume(buf, j, own):
        for h in range(N_HEADS):
            parts, maxes = _slabs(buf, h, tq)
            if own:
                parts, maxes = _causal_slabs(parts, causal), None
            _softmax_update(parts, _vblock(vT_ref, j, h), m_ref, acc_ref, h, None, maxes)

    _run_key_blocks(i * tq // tk, k_ref.shape[0] // tk - 1, (sa_ref, mxa_ref), (sb_ref, mxb_ref), scores_into,
                    consume)
    _finalize(acc_ref, o_ref)


def _moba_kernel(qT_ref, k_ref, vT_ref, km_ref, o_ref, qcat_ref, m_ref, acc_ref, sa_ref, sb_ref, mxa_ref, mxb_ref,
                 bias_ref):
    i = pl.program_id(1)
    tq = qT_ref.shape[1]
    nb = km_ref.shape[1]
    q = qT_ref[...]
    row = lax.broadcasted_iota(jnp.int32, q.shape, 0)
    km = km_ref[0].astype(jnp.bfloat16)
    blk = lax.broadcasted_iota(jnp.int32, (nb, tq), 0)
    qblk = i * (tq // MOBA_BLOCK) + lax.broadcasted_iota(jnp.int32, (1, tq), 1) // MOBA_BLOCK
    neg_inf = jnp.float32(-jnp.inf)
    for h in range(N_HEADS):
        in_head = (((row >= 32 * h) & (row < 32 * (h + 1)))
                   | ((row >= 128 + 32 * h) & (row < 128 + 32 * (h + 1))))
        qm = jnp.where(in_head, q, jnp.zeros_like(q))
        qcat_ref[:, h * tq:(h + 1) * tq] = qm
        g = jnp.where(blk < qblk, _dot(km, qm), neg_inf)
        sel = jnp.zeros((nb, tq), jnp.bool_)
        for _ in range(MOBA_TOPK):
            mx = jnp.max(g, axis=0, keepdims=True)
            cand = jnp.where((g == mx) & (mx > neg_inf), blk, nb)
            chosen = blk == jnp.min(cand, axis=0, keepdims=True)
            sel = sel | chosen
            g = jnp.where(chosen, neg_inf, g)
        bias_ref[h * nb:(h + 1) * nb, :] = jnp.where(sel, 0.0, MASKED_LOGIT)
    _init_state(m_ref, acc_ref)
    tk = sa_ref.shape[0]
    per = tk // SLAB
    causal = _own_block_mask(i, tq, tk)

    def scores_into(buf, j):
        _store_scores(buf, slice(None), _dot(_key_rows(k_ref, j, tk), qcat_ref[...]))

    def consume(buf, j, own):
        for h in range(N_HEADS):
            parts, maxes = _slabs(buf, h, tq)
            gates = [bias_ref[pl.ds(h * nb + per * j + n, 1), :] for n in range(per)]
            if own:
                gates = [jnp.where(per * j + n < qblk, g, 0.0) for n, g in enumerate(gates)]
                parts, maxes = _causal_slabs(parts, causal), None
            _softmax_update(parts, _vblock(vT_ref, j, h), m_ref, acc_ref, h, gates, maxes)

    _run_key_blocks(i * tq // tk, k_ref.shape[0] // tk - 1, (sa_ref, mxa_ref), (sb_ref, mxb_ref), scores_into,
                    consume)
    _finalize(acc_ref, o_ref)


def _dilated_kernel(q_ref, k_ref, v_ref, o_ref, lse_ref, *, window):
    i = pl.program_id(2)
    tq = q_ref.shape[1]
    n = k_ref.shape[1]
    q = q_ref[0]
    a = i * tq
    ks = jnp.clip(a - DIL_WINDOW_STEPS, 0, n - window)
    ks = pl.multiple_of(ks, DIL_WINDOW_STEPS)
    kw = k_ref[0, pl.ds(ks, window), :]
    vw = v_ref[0, pl.ds(ks, window), :]
    jq = a + lax.broadcasted_iota(jnp.int32, (tq, window), 0)
    jk = ks + lax.broadcasted_iota(jnp.int32, (tq, window), 1)
    band = (jq - jk >= 0) & (jq - jk <= DIL_WINDOW_STEPS)
    lane = lax.broadcasted_iota(jnp.int32, (tq, GROUP_W), 1)
    o = jnp.zeros((tq, GROUP_W), jnp.float32)
    lse = jnp.zeros((tq, GROUP_W), jnp.float32)
    for h in range(N_HEADS):
        in_head = (((lane >= 32 * h) & (lane < 32 * (h + 1)))
                   | ((lane >= 128 + 32 * h) & (lane < 128 + 32 * (h + 1))))
        qm = jnp.where(in_head, q, jnp.zeros_like(q))
        s = jnp.where(band, _dot_nt(qm, kw), -jnp.inf)
        m = jnp.max(s, axis=1, keepdims=True)
        p = jnp.exp2(s - m)
        l = jnp.sum(p, axis=1, keepdims=True)
        oh = _dot(p.astype(jnp.bfloat16), vw) / l
        out_lanes = (lane >= h * HEAD_DIM) & (lane < (h + 1) * HEAD_DIM)
        o = jnp.where(out_lanes, oh, o)
        lse = jnp.where(out_lanes, m + jnp.log2(l), lse)
    o_ref[0] = o
    lse_ref[0] = lse


def _out_ffn_kernel(x_ref, omoba_ref, ofox_ref, omla_ref, od1_ref, od4_ref, od16_ref, l1_ref, l4_ref, l16_ref,
                    wout_ref, gpost_ref, gpre_ref, wg_ref, wu_ref, wd_ref, gpf_ref, out_ref, acc_ref,
                    dscr_ref):
    bf16 = jnp.bfloat16
    tm = x_ref.shape[0]

    def token_order(ref, slot, dil):
        halves = GROUP_W // 128
        for rho in range(dil):
            for c in range(halves):
                lanes = slice(rho * GROUP_W + c * 128, rho * GROUP_W + (c + 1) * 128)
                dscr_ref[slot + c, pl.ds(rho, tm // dil, stride=dil), :] = ref[:, lanes]
        return jnp.concatenate([dscr_ref[slot + c] for c in range(halves)], axis=1)

    l1, od1 = l1_ref[...], od1_ref[...]
    l4, od4 = token_order(l4_ref, 0, 4), token_order(od4_ref, 2, 4)
    l16, od16 = token_order(l16_ref, 4, 16), token_order(od16_ref, 6, 16)
    m = jnp.maximum(jnp.maximum(l1, l4), l16)
    e1, e4, e16 = jnp.exp2(l1 - m), jnp.exp2(l4 - m), jnp.exp2(l16 - m)
    odil = (e1 * od1 + e4 * od4 + e16 * od16) / (e1 + e4 + e16)
    y = (_dot(omoba_ref[...], wout_ref[0]) + _dot(ofox_ref[...], wout_ref[1])
         + _dot(omla_ref[...], wout_ref[2]) + _dot(odil.astype(bf16), wout_ref[3]))
    x1 = x_ref[...] + y * _rms_scale(y, D_MODEL) * gpost_ref[...]
    hb = (x1 * _rms_scale(x1, D_MODEL) * gpre_ref[...]).astype(bf16)
    acc_ref[...] = jnp.zeros_like(acc_ref)

    def chunk(c, _):
        g = _dot(hb, wg_ref[c])
        u = _dot(hb, wu_ref[c])
        f = (g * jax.nn.sigmoid(g) * u).astype(bf16)
        acc_ref[...] += _dot(f, wd_ref[c])
        return 0

    lax.fori_loop(0, wg_ref.shape[0], chunk, 0, unroll=True)
    f = acc_ref[...]
    out_ref[...] = x1 + f * _rms_scale(f, D_MODEL) * gpf_ref[...]


def _rope_tables(positions):
    t = positions.reshape(-1).astype(jnp.float32)

    def tab(dim):
        inv = ROPE_THETA ** (-jnp.arange(0, dim, 2, dtype=jnp.float32) / dim)
        ang = t[:, None] * inv
        return jnp.cos(ang), jnp.sin(ang)

    c64, s64 = tab(HEAD_DIM)
    c16, s16 = tab(MLA_ROPE)
    ct64, st64 = jnp.tile(c64, (1, 4)), jnp.tile(s64, (1, 4))
    pad = jnp.zeros((t.shape[0], 128 - MLA_ROPE), jnp.float32)
    ct16 = jnp.concatenate([c16, c16, pad], axis=1)
    st16 = jnp.concatenate([s16, s16, pad], axis=1)
    cT16, sT16 = jnp.tile(c16, (1, 4)).T, jnp.tile(s16, (1, 4)).T
    return ct64, st64, ct64.T, st64.T, ct16, st16, cT16, sT16


_HALF_PERM = np.array([h * 64 + half * 32 + j for half in (0, 1) for h in range(4) for j in range(32)])
_QROPE_ROWS = np.array([h * 96 + 64 + half * 16 + j for half in (0, 1) for h in range(4) for j in range(16)])
_QNOPE_ROWS = np.array([h * 96 + j for h in range(4) for j in range(64)])
_KNOPE_COLS = np.array([h * 128 + j for h in range(4) for j in range(64)])
_VMLA_COLS = np.array([h * 128 + 64 + j for h in range(4) for j in range(64)])


def _prep_layer_weights(w_in, b_forget, g_mla_q, w_mla_q_up, g_mla_kv, w_mla_kv_up, w_out, w_gate, w_up, w_down):
    bf16 = jnp.bfloat16
    depth = w_in.shape[0]
    sl = lambda a, b: w_in[:, :, a:b]
    scale = HEAD_DIM ** -0.5 * LOG2E
    moba_q, moba_k, moba_v = sl(0, 256)[..., _HALF_PERM] * scale, sl(256, 512)[..., _HALF_PERM], sl(512, 768)
    fox_q, fox_k, fox_v = sl(768, 1024) * scale, sl(1024, 1280), sl(1280, 1536)
    fg, cq, ckv, kr = sl(1536, 1540), sl(1540, 1732), sl(1732, 1860), sl(1860, 1892)
    dil_q, dil_k, dil_v = sl(1892, 2148)[..., _HALF_PERM] * scale, sl(2148, 2404)[..., _HALF_PERM], sl(2404, 2660)
    kr_rot = jnp.concatenate([-kr[..., 16:], kr[..., :16]], axis=-1)
    z = lambda n: jnp.zeros((depth, D_MODEL, n), w_in.dtype)
    wtok = jnp.concatenate([moba_k, fox_k, dil_q, dil_k, dil_v, cq, z(256 - MLA_Q_RANK), ckv,
                            kr, fg, z(128 - 36), kr_rot, z(96)], axis=-1).astype(bf16)
    wtr = jnp.swapaxes(jnp.concatenate([moba_q, moba_v, fox_q, fox_v], axis=-1), 1, 2).astype(bf16)
    bfg = jnp.zeros((depth, 1, 128), jnp.float32).at[:, 0, _FG_LANE:_FG_LANE + N_HEADS].set(b_forget)
    gq = jnp.pad(g_mla_q, ((0, 0), (0, 256 - MLA_Q_RANK)))[:, None, :]
    wq_rows = jnp.swapaxes(w_mla_q_up, 1, 2)
    wqT = jnp.concatenate([wq_rows[:, _QNOPE_ROWS], wq_rows[:, _QROPE_ROWS]], axis=1)
    wqT = jnp.pad(wqT, ((0, 0), (0, 0), (0, 256 - MLA_Q_RANK))).astype(bf16)
    gkv = g_mla_kv[:, None, :]
    wkn = w_mla_kv_up[:, :, _KNOPE_COLS].astype(bf16)
    wvT = jnp.swapaxes(w_mla_kv_up[:, :, _VMLA_COLS], 1, 2).astype(bf16)
    wout = w_out.reshape(depth, 4, GROUP_W, D_MODEL).astype(bf16)
    nc = D_FF // FF_CHUNK
    wg = jnp.swapaxes(w_gate.reshape(depth, D_MODEL, nc, FF_CHUNK), 1, 2).astype(bf16)
    wu = jnp.swapaxes(w_up.reshape(depth, D_MODEL, nc, FF_CHUNK), 1, 2).astype(bf16)
    wd = w_down.reshape(depth, nc, FF_CHUNK, D_MODEL).astype(bf16)
    return wtok, wtr, bfg, gq, wqT, gkv, wkn, wvT, wout, wg, wu, wd


def _const_spec(shape):
    return pl.BlockSpec(shape, lambda *_: (0,) * len(shape))


def _params(*sem):
    return pltpu.CompilerParams(dimension_semantics=sem, vmem_limit_bytes=VMEM_LIMIT)


def _in_proj(x2, g, wtok, wtr, tri, tables, bfg, gq, wqT, gkv, wkn, wvT, seq):
    t = x2.shape[0]
    tm = ROW_TILE
    nt = t // tm
    bf16, f32 = jnp.bfloat16, jnp.float32
    ct64, st64, cT64, sT64, ct16, st16, cT16, sT16 = tables
    tok_spec = lambda w: pl.BlockSpec((tm, w), lambda i: (i, 0))
    tr_spec = lambda r: pl.BlockSpec((r, tm), lambda i: (0, i))
    blk3 = lambda n, r, c: pl.BlockSpec((n, r, c), lambda i: (i, 0, 0))
    mb = tm // MOBA_BLOCK
    ab = tm // KEY_BLOCK
    vrows = N_HEADS * V_ROWS
    vt_shape = jax.ShapeDtypeStruct((t // KEY_BLOCK, vrows, KEY_BLOCK), bf16)
    in_specs = [tok_spec(D_MODEL), _const_spec((1, D_MODEL)), _const_spec(wtok.shape), _const_spec(wtr.shape),
                _const_spec(tri.shape), tok_spec(128), tok_spec(128), tr_spec(128), tr_spec(128),
                tok_spec(128), tok_spec(128), tr_spec(64), tr_spec(64), _const_spec((1, 128)),
                _const_spec((1, 256)), _const_spec(wqT.shape), _const_spec((1, 128)), _const_spec(wkn.shape),
                _const_spec(wvT.shape)]
    out_shape = [
        jax.ShapeDtypeStruct((t, GROUP_W), bf16),
        jax.ShapeDtypeStruct((nt, mb, GROUP_W), f32),
        jax.ShapeDtypeStruct((GROUP_W, t), bf16),
        vt_shape,
        jax.ShapeDtypeStruct((t, 512), bf16),
        jax.ShapeDtypeStruct((GROUP_W, t), bf16),
        vt_shape,
        jax.ShapeDtypeStruct((nt, 1, 128), f32),
        *[jax.ShapeDtypeStruct((t // d, d * GROUP_W), bf16) for d in DILATIONS] * 3,
        jax.ShapeDtypeStruct((t, 512), bf16),
        jax.ShapeDtypeStruct((384, t), bf16),
        vt_shape,
    ]
    vt_spec = blk3(ab, vrows, KEY_BLOCK)
    out_specs = [tok_spec(GROUP_W), blk3(1, mb, GROUP_W), tr_spec(GROUP_W), vt_spec,
                 tok_spec(512), tr_spec(GROUP_W), vt_spec, blk3(1, 1, 128),
                 *[pl.BlockSpec((tm // d, d * GROUP_W), lambda i: (i, 0)) for d in DILATIONS] * 3,
                 tok_spec(512), tr_spec(384), vt_spec]
    return pl.pallas_call(
        functools.partial(_in_proj_kernel, tiles_per_seq=seq // tm),
        grid=(nt,), in_specs=in_specs, out_specs=out_specs, out_shape=out_shape,
        scratch_shapes=[pltpu.VMEM((1, 128), f32), pltpu.VMEM((3 * GROUP_W // 128, tm, 128), f32)],
        compiler_params=_params("arbitrary"), name="in_proj",
    )(x2, g, wtok, wtr, tri, ct64, st64, cT64, sT64, ct16, st16, cT16, sT16, bfg, gq, wqT, gkv, wkn, wvT)


def _attention_scratch(tq, tk=KEY_BLOCK):
    return [pltpu.VMEM((GROUP_W, N_HEADS * tq), jnp.bfloat16),
            pltpu.VMEM((8, tq), jnp.float32),
            pltpu.VMEM((N_HEADS, V_ROWS, tq), jnp.float32),
            pltpu.VMEM((tk, N_HEADS * tq), jnp.float32),
            pltpu.VMEM((tk, N_HEADS * tq), jnp.float32),
            pltpu.VMEM((8, N_HEADS * tq), jnp.float32),
            pltpu.VMEM((8, N_HEADS * tq), jnp.float32)]


def _dense_attention(body, name, qT, k, vT, extra, batch, seq):
    tq = ATT_TILE
    nq = seq // tq
    t = batch * seq
    once = pl.Buffered(1)
    in_specs = [pl.BlockSpec((qT.shape[0], tq), lambda b, i: (0, b * nq + i)),
                pl.BlockSpec((seq, k.shape[1]), lambda b, i: (b, 0), pipeline_mode=once),
                pl.BlockSpec((seq // KEY_BLOCK, N_HEADS * V_ROWS, KEY_BLOCK), lambda b, i: (b, 0, 0),
                             pipeline_mode=once)]
    in_specs += [pl.BlockSpec((seq // KEY_BLOCK, 1, 128), lambda b, i: (b, 0, 0)) for _ in extra]
    return pl.pallas_call(
        body, grid=(batch, nq), in_specs=in_specs,
        out_specs=pl.BlockSpec((tq, GROUP_W), lambda b, i: (b * nq + i, 0)),
        out_shape=jax.ShapeDtypeStruct((t, GROUP_W), jnp.bfloat16),
        scratch_shapes=_attention_scratch(tq),
        compiler_params=_params("arbitrary", "arbitrary"), name=name,
    )(qT, k, vT, *extra)


def _moba_attention(qT, k, vT, kmean, batch, seq):
    tq = ATT_TILE
    nq = seq // tq
    nb = seq // MOBA_BLOCK
    t = batch * seq
    return pl.pallas_call(
        _moba_kernel, grid=(batch, nq),
        in_specs=[pl.BlockSpec((GROUP_W, tq), lambda b, i: (0, b * nq + i)),
                  pl.BlockSpec((seq, GROUP_W), lambda b, i: (b, 0), pipeline_mode=pl.Buffered(1)),
                  pl.BlockSpec((seq // KEY_BLOCK, N_HEADS * V_ROWS, KEY_BLOCK), lambda b, i: (b, 0, 0),
                               pipeline_mode=pl.Buffered(1)),
                  pl.BlockSpec((1, nb, GROUP_W), lambda b, i: (b, 0, 0))],
        out_specs=pl.BlockSpec((tq, GROUP_W), lambda b, i: (b * nq + i, 0)),
        out_shape=jax.ShapeDtypeStruct((t, GROUP_W), jnp.bfloat16),
        scratch_shapes=_attention_scratch(tq) + [pltpu.VMEM((N_HEADS * nb, tq), jnp.float32)],
        compiler_params=_params("arbitrary", "arbitrary"), name="moba_attention",
    )(qT, k, vT, kmean)


def _dilated_attention(q, k, v, batch, seq, dil):
    n = seq // dil
    tq = min(DIL_TILE, n)
    window = min(tq + DIL_WINDOW_STEPS, n)
    view = lambda a: a.reshape(batch, n, dil * GROUP_W)
    qspec = pl.BlockSpec((1, tq, GROUP_W), lambda b, r, i: (b, i, r))
    kspec = pl.BlockSpec((1, n, GROUP_W), lambda b, r, i: (b, 0, r))
    shape = jax.ShapeDtypeStruct((batch, n, dil * GROUP_W), jnp.float32)
    o, lse = pl.pallas_call(
        functools.partial(_dilated_kernel, window=window), grid=(batch, dil, n // tq),
        in_specs=[qspec, kspec, kspec], out_specs=[qspec, qspec], out_shape=[shape, shape],
        compiler_params=_params("arbitrary", "arbitrary", "arbitrary"), name=f"dilated_attention_d{dil}",
    )(view(q), view(k), view(v))
    return o.reshape(batch * n, dil * GROUP_W), lse.reshape(batch * n, dil * GROUP_W)


def _out_ffn(x2, omoba, ofox, omla, dil_outs, wout, gpost, gpre, wg, wu, wd, gpf):
    t = x2.shape[0]
    tm = ROW_TILE
    row = lambda w: pl.BlockSpec((tm, w), lambda i: (i, 0))
    (od1, l1), (od4, l4), (od16, l16) = dil_outs
    dil_specs = [pl.BlockSpec((tm // d, d * GROUP_W), lambda i: (i, 0)) for d in DILATIONS]
    in_specs = ([row(D_MODEL)] + [row(GROUP_W)] * 3 + dil_specs * 2
                + [_const_spec(wout.shape), _const_spec((1, D_MODEL)), _const_spec((1, D_MODEL)),
                   _const_spec(wg.shape), _const_spec(wu.shape), _const_spec(wd.shape), _const_spec((1, D_MODEL))])
    return pl.pallas_call(
        _out_ffn_kernel, grid=(t // tm,), in_specs=in_specs, out_specs=row(D_MODEL),
        out_shape=jax.ShapeDtypeStruct((t, D_MODEL), jnp.float32),
        scratch_shapes=[pltpu.VMEM((tm, D_MODEL), jnp.float32), pltpu.VMEM((4 * GROUP_W // 128, tm, 128), jnp.float32)],
        compiler_params=_params("arbitrary"), name="out_ffn",
    )(x2, omoba, ofox, omla, od1, od4, od16, l1, l4, l16, wout, gpost, gpre, wg, wu, wd, gpf)


def kernel(x, positions, w_in, b_forget, g_mla_q, w_mla_q_up, g_mla_kv, w_mla_kv_up, w_out, g_pre_mix, g_post_mix, w_gate, w_up, w_down, g_pre_ffn, g_post_ffn):
    batch, seq, _ = x.shape
    depth = w_in.shape[0]
    assert seq % ROW_TILE == 0 and seq % (DILATIONS[-1] * DIL_WINDOW_STEPS) == 0
    assert ROW_TILE == KEY_BLOCK
    tables = _rope_tables(positions)
    wtok, wtr, bfg, gq, wqT, gkv, wkn, wvT, wout, wg, wu, wd = _prep_layer_weights(
        w_in, b_forget, g_mla_q, w_mla_q_up, g_mla_kv, w_mla_kv_up, w_out, w_gate, w_up, w_down)
    tri = jnp.tril(jnp.ones((ROW_TILE, ROW_TILE), jnp.bfloat16))
    x2 = x.reshape(batch * seq, D_MODEL)
    for l in range(depth):
        (k_moba, kmean, qT_moba, vT_moba, k_fox, qT_fox, vT_fox, cbase, *dil_qkv,
         k_mla, qT_mla, vT_mla) = _in_proj(x2, g_pre_mix[l][None], wtok[l], wtr[l], tri, tables, bfg[l], gq[l],
                                          wqT[l], gkv[l], wkn[l], wvT[l], seq)
        nd = len(DILATIONS)
        o_moba = _moba_attention(qT_moba, k_moba, vT_moba, kmean.reshape(batch, seq // MOBA_BLOCK, GROUP_W),
                                 batch, seq)
        o_fox = _dense_attention(_fox_kernel, "fox_attention", qT_fox, k_fox, vT_fox, [cbase], batch, seq)
        o_mla = _dense_attention(_mla_kernel, "mla_attention", qT_mla, k_mla, vT_mla, [], batch, seq)
        dil_outs = [_dilated_attention(dil_qkv[n], dil_qkv[nd + n], dil_qkv[2 * nd + n], batch, seq, d)
                    for n, d in enumerate(DILATIONS)]
        x2 = _out_ffn(x2, o_moba, o_fox, o_mla, dil_outs, wout[l], g_post_mix[l][None], g_pre_ffn[l][None],
                      wg[l], wu[l], wd[l], g_post_ffn[l][None])
    return x2.reshape(batch, seq, D_MODEL)
```

```python
import functools

import numpy as np
import jax
import jax.numpy as jnp
from jax import lax
from jax.experimental import pallas as pl
from jax.experimental.pallas import tpu as pltpu

D_MODEL = 1024
HEAD_DIM = 64
N_HEADS = 4
GROUP_W = N_HEADS * HEAD_DIM
MOBA_BLOCK = 256
MOBA_TOPK = 3
MLA_Q_RANK = 192
MLA_KV_RANK = 128
MLA_NOPE = 64
MLA_ROPE = 32
DIL_WINDOW_STEPS = 128
DILATIONS = (1, 4, 16)
ROPE_THETA = 10000.0
RMS_EPS = 1e-6
D_FF = 2816
FF_CHUNK = 256
V_ROWS = 80
LOG2E = 1.4426950408889634
MASKED_LOGIT = -1e30

ROW_TILE = 512
ATT_TILE = 512
KEY_BLOCK = 512
SLAB = MOBA_BLOCK
DIL_TILE = 256
VMEM_LIMIT = 56 * 1024 * 1024

_C_KMOBA, _C_KFOX, _C_QDIL, _C_KDIL, _C_VDIL, _C_CQ = 0, 256, 512, 768, 1024, 1280
_C_CKV, _C_X, _C_Y, _N_TOK = 1536, 1664, 1792, 1920
_FG_LANE = 32

_NT = (((1,), (1,)), ((), ()))


def _dot(a, b):
    return jnp.dot(a, b, preferred_element_type=jnp.float32)


def _dot_nt(a, b):
    return lax.dot_general(a, b, _NT, preferred_element_type=jnp.float32)


def _rms_scale(v, n):
    return lax.rsqrt(jnp.sum(v * v, axis=-1, keepdims=True) * (1.0 / n) + RMS_EPS)


def _in_proj_kernel(x_ref, g_ref, wtok_ref, wtr_ref, tri_ref, ct64_ref, st64_ref, cT64_ref, sT64_ref,
                    ct16_ref, st16_ref, cT16_ref, sT16_ref, bfg_ref, gq_ref, wqT_ref, gkv_ref, wkn_ref, wvT_ref,
                    kmoba_ref, kmean_ref, qTmoba_ref, vTmoba_ref, kfox_ref, qTfox_ref, vTfox_ref, cbase_ref,
                    q1_ref, q4_ref, q16_ref, k1_ref, k4_ref, k16_ref, v1_ref, v4_ref, v16_ref,
                    kmla_ref, qTmla_ref, vTmla_ref, carry_ref, dscr_ref, *, tiles_per_seq):
    bf16 = jnp.bfloat16
    qdil_refs, kdil_refs, vdil_refs = (q1_ref, q4_ref, q16_ref), (k1_ref, k4_ref, k16_ref), (v1_ref, v4_ref, v16_ref)
    x = x_ref[...]
    hb = (x * _rms_scale(x, D_MODEL) * g_ref[...]).astype(bf16)
    tm = x.shape[0]

    def tok(c0, w):
        return _dot(hb, wtok_ref[:, c0:c0 + w])

    def rope_tok(z):
        c, s = ct64_ref[...], st64_ref[...]
        x1, x2 = z[:, :128], z[:, 128:]
        return jnp.concatenate([x1 * c - x2 * s, x2 * c + x1 * s], axis=1)

    def with_ones(vT):
        ones = jnp.ones((V_ROWS - HEAD_DIM, tm), bf16)
        parts = []
        for h in range(N_HEADS):
            parts += [vT[h * HEAD_DIM:(h + 1) * HEAD_DIM].astype(bf16), ones]
        return jnp.concatenate(parts, axis=0)

    def rope_tr(zT):
        c, s = cT64_ref[...], sT64_ref[...]
        x1, x2 = zT[:128], zT[128:]
        return jnp.concatenate([x1 * c - x2 * s, x2 * c + x1 * s], axis=0)

    k_moba = rope_tok(tok(_C_KMOBA, GROUP_W))
    kmoba_ref[...] = k_moba.astype(bf16)
    for blk in range(tm // MOBA_BLOCK):
        kmean_ref[0, blk:blk + 1, :] = jnp.mean(k_moba[blk * MOBA_BLOCK:(blk + 1) * MOBA_BLOCK], axis=0, keepdims=True)
    qTmoba_ref[...] = rope_tr(_dot_nt(wtr_ref[0:256, :], hb)).astype(bf16)
    def store_value_blocks(ref, vT):
        for blk in range(tm // KEY_BLOCK):
            ref[blk] = vT[:, blk * KEY_BLOCK:(blk + 1) * KEY_BLOCK]

    store_value_blocks(vTmoba_ref, with_ones(_dot_nt(wtr_ref[256:512, :], hb)))

    qTfox_ref[...] = _dot_nt(wtr_ref[512:768, :], hb).astype(bf16)
    store_value_blocks(vTfox_ref, with_ones(_dot_nt(wtr_ref[768:1024, :], hb)))

    def emit_dilated(refs, slot, z):
        refs[0][...] = z.astype(bf16)
        for c in range(GROUP_W // 128):
            dscr_ref[slot + c] = z[:, c * 128:(c + 1) * 128]
        for ref, dil in zip(refs[1:], DILATIONS[1:]):
            for rho in range(dil):
                for c in range(GROUP_W // 128):
                    rows = dscr_ref[slot + c, pl.ds(rho, tm // dil, stride=dil), :]
                    ref[:, rho * GROUP_W + c * 128:rho * GROUP_W + (c + 1) * 128] = rows.astype(bf16)

    emit_dilated(qdil_refs, 0, rope_tok(tok(_C_QDIL, GROUP_W)))
    emit_dilated(kdil_refs, 2, rope_tok(tok(_C_KDIL, GROUP_W)))
    emit_dilated(vdil_refs, 4, tok(_C_VDIL, GROUP_W))

    xblk = tok(_C_X, 128)
    yblk = tok(_C_Y, 128)
    fg = xblk + bfg_ref[...]
    logf = jnp.minimum(fg, 0.0) - jnp.log1p(jnp.exp(-jnp.abs(fg)))
    a1 = logf.astype(bf16)
    r1 = logf - a1.astype(jnp.float32)
    a2 = r1.astype(bf16)
    a3 = (r1 - a2.astype(jnp.float32)).astype(bf16)
    tri = tri_ref[...]

    @pl.when(pl.program_id(0) % tiles_per_seq == 0)
    def _():
        carry_ref[...] = jnp.zeros_like(carry_ref)

    cum = _dot(tri, a1) + _dot(tri, a2) + _dot(tri, a3) + carry_ref[...]
    carry_ref[...] = cum[tm - 1:tm, :]
    cbase_ref[0] = cum[0:1, :]

    lane = lax.broadcasted_iota(jnp.int32, cum.shape, 1)
    is_gate = (lane >= _FG_LANE) & (lane < _FG_LANE + N_HEADS)
    dec = jnp.where(is_gate, (cum[0:1, :] - cum) * LOG2E, 0.0)
    d1 = dec.astype(bf16).astype(jnp.float32)
    d2 = (dec - d1).astype(bf16).astype(jnp.float32)
    d3 = (dec - d1 - d2).astype(bf16).astype(jnp.float32)
    dcols = d1 + pltpu.roll(d2, 8, 1) + pltpu.roll(d3, 16, 1)
    kf = tok(_C_KFOX, GROUP_W)
    kfox_ref[...] = jnp.concatenate([kf[:, :128], dcols, kf[:, 128:], dcols], axis=1).astype(bf16)

    zcq = tok(_C_CQ, 256)
    cq = (zcq * _rms_scale(zcq, MLA_Q_RANK) * gq_ref[...]).astype(bf16)
    qcT = _dot_nt(wqT_ref[...], cq)
    c16, s16 = cT16_ref[...], sT16_ref[...]
    r1h, r2h = qcT[256:320], qcT[320:384]
    qT = jnp.concatenate([qcT[0:256], r1h * c16 - r2h * s16, r2h * c16 + r1h * s16], axis=0)
    qTmla_ref[...] = (qT * ((MLA_NOPE + MLA_ROPE) ** -0.5 * LOG2E)).astype(bf16)

    zckv = tok(_C_CKV, MLA_KV_RANK)
    ckv = (zckv * _rms_scale(zckv, MLA_KV_RANK) * gkv_ref[...]).astype(bf16)
    kn = _dot(ckv, wkn_ref[...])
    krope = xblk * ct16_ref[...] + yblk * st16_ref[...]
    kmla_ref[...] = jnp.concatenate([kn[:, :128], krope, kn[:, 128:], krope], axis=1).astype(bf16)
    store_value_blocks(vTmla_ref, with_ones(_dot_nt(wvT_ref[...], ckv)))


def _softmax_update(parts, vb, m_ref, acc_ref, h, offsets=None, maxes=None):
    if offsets is None:
        offsets = [None] * len(parts)
    if maxes is None:
        maxes = [jnp.max(s, axis=0, keepdims=True) for s in parts]
    m_old = m_ref[h:h + 1, :]
    blk_max = None
    for mx, off in zip(maxes, offsets):
        mx = mx if off is None else mx + off
        blk_max = mx if blk_max is None else jnp.maximum(blk_max, mx)
    m_new = jnp.maximum(m_old, blk_max)
    ps = []
    for s, off in zip(parts, offsets):
        shift = m_new if off is None else m_new - off
        ps.append(jnp.exp2(s - shift).astype(jnp.bfloat16))
    p = ps[0] if len(ps) == 1 else jnp.concatenate(ps, axis=0)
    acc_ref[h] = jnp.exp2(m_old - m_new) * acc_ref[h] + _dot(vb, p)
    m_ref[h:h + 1, :] = m_new


def _store_scores(buf, cols, s):
    s_ref, mx_ref = buf
    s_ref[:, cols] = s
    for n in range(s.shape[0] // SLAB):
        mx_ref[n:n + 1, cols] = jnp.max(s[n * SLAB:(n + 1) * SLAB], axis=0, keepdims=True)


def _slabs(buf, h, tq):
    s_ref, mx_ref = buf
    cols = slice(h * tq, (h + 1) * tq)
    n_slabs = s_ref.shape[0] // SLAB
    return ([s_ref[n * SLAB:(n + 1) * SLAB, cols] for n in range(n_slabs)],
            [mx_ref[n:n + 1, cols] for n in range(n_slabs)])


def _own_block_mask(i, tq, tk):
    key = lax.broadcasted_iota(jnp.int32, (tk, tq), 0)
    qry = lax.broadcasted_iota(jnp.int32, (tk, tq), 1) + (i % (tk // tq)) * tq
    return key <= qry


def _init_state(m_ref, acc_ref):
    m_ref[...] = jnp.full(m_ref.shape, -jnp.inf, jnp.float32)
    acc_ref[...] = jnp.zeros(acc_ref.shape, jnp.float32)


def _finalize(acc_ref, o_ref):
    outs = [acc_ref[h, 0:HEAD_DIM, :] / acc_ref[h, HEAD_DIM:HEAD_DIM + 1, :] for h in range(N_HEADS)]
    o_ref[...] = jnp.concatenate(outs, axis=0).T.astype(o_ref.dtype)


def _vblock(vT_ref, j, h):
    return vT_ref[j, h * V_ROWS:(h + 1) * V_ROWS, :]


def _run_key_blocks(own, last, buf_a, buf_b, scores_into, consume):
    scores_into(buf_a, own)
    scores_into(buf_b, 0)
    consume(buf_a, own, True)

    def pair(p, carry):
        j0 = 2 * p
        scores_into(buf_a, j0 + 1)
        consume(buf_b, j0, False)
        scores_into(buf_b, jnp.minimum(j0 + 2, last))
        consume(buf_a, j0 + 1, False)
        return carry

    lax.fori_loop(0, own // 2, pair, 0)

    @pl.when(own % 2 == 1)
    def _():
        consume(buf_b, own - 1, False)


def _key_rows(k_ref, j, tk):
    return k_ref[pl.ds(pl.multiple_of(j * tk, tk), tk), :]


def _causal_slabs(parts, causal):
    return [jnp.where(causal[n * SLAB:(n + 1) * SLAB], s, -jnp.inf) for n, s in enumerate(parts)]


def _two_tile_scores(buf, k_ref, qcat_ref, j, tk, tq):
    kb = _key_rows(k_ref, j, tk)
    for half in range(2):
        cols = slice(2 * half * tq, 2 * (half + 1) * tq)
        _store_scores(buf, cols, _dot(kb[:, 256 * half:256 * (half + 1)], qcat_ref[:, cols]))


def _fox_kernel(qT_ref, k_ref, vT_ref, cbase_ref, o_ref, qcat_ref, m_ref, acc_ref, sa_ref, sb_ref, mxa_ref, mxb_ref):
    i = pl.program_id(1)
    tq = qT_ref.shape[1]
    zeros64 = jnp.zeros((HEAD_DIM, tq), jnp.bfloat16)
    row = lax.broadcasted_iota(jnp.int32, (128, tq), 0)
    for h in range(N_HEADS):
        qh = qT_ref[h * HEAD_DIM:(h + 1) * HEAD_DIM, :]
        pick = (row == _FG_LANE + h) | (row == _FG_LANE + 8 + h) | (row == _FG_LANE + 16 + h)
        rows = ([qh, zeros64] if h % 2 == 0 else [zeros64, qh]) + [pick.astype(jnp.bfloat16)]
        qcat_ref[:, h * tq:(h + 1) * tq] = jnp.concatenate(rows, axis=0)
    _init_state(m_ref, acc_ref)
    tk = sa_ref.shape[0]
    assert tq == tk
    causal = _own_block_mask(i, tq, tk)

    def scores_into(buf, j):
        _two_tile_scores(buf, k_ref, qcat_ref, j, tk, tq)

    def consume(buf, j, own):
        offs = (cbase_ref[i] - cbase_ref[j]) * LOG2E
        for h in range(N_HEADS):
            parts, maxes = _slabs(buf, h, tq)
            if own:
                parts, maxes = _causal_slabs(parts, causal), None
            off = offs[:, _FG_LANE + h:_FG_LANE + h + 1]
            _softmax_update(parts, _vblock(vT_ref, j, h), m_ref, acc_ref, h, [off] * len(parts), maxes)

    _run_key_blocks(i, k_ref.shape[0] // tk - 1, (sa_ref, mxa_ref), (sb_ref, mxb_ref), scores_into, consume)
    _finalize(acc_ref, o_ref)


def _mla_kernel(qT_ref, k_ref, vT_ref, o_ref, qcat_ref, m_ref, acc_ref, sa_ref, sb_ref, mxa_ref, mxb_ref):
    i = pl.program_id(1)
    tq = qT_ref.shape[1]
    zeros64 = jnp.zeros((MLA_NOPE, tq), jnp.bfloat16)
    zeros96 = jnp.zeros((256 - 128 - MLA_ROPE, tq), jnp.bfloat16)
    for h in range(N_HEADS):
        qn = qT_ref[h * MLA_NOPE:(h + 1) * MLA_NOPE, :]
        qr1 = qT_ref[256 + 16 * h:256 + 16 * (h + 1), :]
        qr2 = qT_ref[320 + 16 * h:320 + 16 * (h + 1), :]
        nope = [qn, zeros64] if h % 2 == 0 else [zeros64, qn]
        qcat_ref[:, h * tq:(h + 1) * tq] = jnp.concatenate(nope + [qr1, qr2, zeros96], axis=0)
    _init_state(m_ref, acc_ref)
    tk = sa_ref.shape[0]
    causal = _own_block_mask(i, tq, tk)

    def scores_into(buf, j):
        _two_tile_scores(buf, k_ref, qcat_ref, j, tk, tq)

    def consume(buf, j, own):
        for h in range(N_HEADS):
            parts, maxes = _slabs(buf, h, tq)
            if own:
                parts, maxes = _causal_slabs(parts, causal), None
            _softmax_update(parts, _vblock(vT_ref, j, h), m_ref, acc_ref, h, None, maxes)

    _run_key_blocks(i * tq // tk, k_ref.shape[0] // tk - 1, (sa_ref, mxa_ref), (sb_ref, mxb_ref), scores_into,
                    consume)
    _finalize(acc_ref, o_ref)


def _moba_kernel(qT_ref, k_ref, vT_ref, km_ref, o_ref, qcat_ref, m_ref, acc_ref, sa_ref, sb_ref, mxa_ref, mxb_ref,
                 bias_ref):
    i = pl.program_id(1)
    tq = qT_ref.shape[1]
    nb = km_ref.shape[1]
    q = qT_ref[...]
    row = lax.broadcasted_iota(jnp.int32, q.shape, 0)
    km = km_ref[0].astype(jnp.bfloat16)
    blk = lax.broadcasted_iota(jnp.int32, (nb, tq), 0)
    qblk = i * (tq // MOBA_BLOCK) + lax.broadcasted_iota(jnp.int32, (1, tq), 1) // MOBA_BLOCK
    neg_inf = jnp.float32(-jnp.inf)
    for h in range(N_HEADS):
        in_head = (((row >= 32 * h) & (row < 32 * (h + 1)))
                   | ((row >= 128 + 32 * h) & (row < 128 + 32 * (h + 1))))
        qm = jnp.where(in_head, q, jnp.zeros_like(q))
        qcat_ref[:, h * tq:(h + 1) * tq] = qm
        g = jnp.where(blk < qblk, _dot(km, qm), neg_inf)
        sel = jnp.zeros((nb, tq), jnp.bool_)
        for _ in range(MOBA_TOPK):
            mx = jnp.max(g, axis=0, keepdims=True)
            cand = jnp.where((g == mx) & (mx > neg_inf), blk, nb)
            chosen = blk == jnp.min(cand, axis=0, keepdims=True)
            sel = sel | chosen
            g = jnp.where(chosen, neg_inf, g)
        bias_ref[h * nb:(h + 1) * nb, :] = jnp.where(sel, 0.0, MASKED_LOGIT)
    _init_state(m_ref, acc_ref)
    tk = sa_ref.shape[0]
    per = tk // SLAB
    causal = _own_block_mask(i, tq, tk)

    def scores_into(buf, j):
        _store_scores(buf, slice(None), _dot(_key_rows(k_ref, j, tk), qcat_ref[...]))

    def consume(buf, j, own):
        for h in range(N_HEADS):
            parts, maxes = _slabs(buf, h, tq)
            gates = [bias_ref[pl.ds(h * nb + per * j + n, 1), :] for n in range(per)]
            if own:
                gates = [jnp.where(per * j + n < qblk, g, 0.0) for n, g in enumerate(gates)]
                parts, maxes = _causal_slabs(parts, causal), None
            _softmax_update(parts, _vblock(vT_ref, j, h), m_ref, acc_ref, h, gates, maxes)

    _run_key_blocks(i * tq // tk, k_ref.shape[0] // tk - 1, (sa_ref, mxa_ref), (sb_ref, mxb_ref), scores_into,
                    consume)
    _finalize(acc_ref, o_ref)


def _dilated_kernel(q_ref, k_ref, v_ref, o_ref, lse_ref, *, window):
    i = pl.program_id(2)
    tq = q_ref.shape[1]
    n = k_ref.shape[1]
    q = q_ref[0]
    a = i * tq
    ks = jnp.clip(a - DIL_WINDOW_STEPS, 0, n - window)
    ks = pl.multiple_of(ks, DIL_WINDOW_STEPS)
    kw = k_ref[0, pl.ds(ks, window), :]
    vw = v_ref[0, pl.ds(ks, window), :]
    jq = a + lax.broadcasted_iota(jnp.int32, (tq, window), 0)
    jk = ks + lax.broadcasted_iota(jnp.int32, (tq, window), 1)
    band = (jq - jk >= 0) & (jq - jk <= DIL_WINDOW_STEPS)
    lane = lax.broadcasted_iota(jnp.int32, (tq, GROUP_W), 1)
    o = jnp.zeros((tq, GROUP_W), jnp.float32)
    lse = jnp.zeros((tq, GROUP_W), jnp.float32)
    for h in range(N_HEADS):
        in_head = (((lane >= 32 * h) & (lane < 32 * (h + 1)))
                   | ((lane >= 128 + 32 * h) & (lane < 128 + 32 * (h + 1))))
        qm = jnp.where(in_head, q, jnp.zeros_like(q))
        s = jnp.where(band, _dot_nt(qm, kw), -jnp.inf)
        m = jnp.max(s, axis=1, keepdims=True)
        p = jnp.exp2(s - m)
        l = jnp.sum(p, axis=1, keepdims=True)
        oh = _dot(p.astype(jnp.bfloat16), vw) / l
        out_lanes = (lane >= h * HEAD_DIM) & (lane < (h + 1) * HEAD_DIM)
        o = jnp.where(out_lanes, oh, o)
        lse = jnp.where(out_lanes, m + jnp.log2(l), lse)
    o_ref[0] = o
    lse_ref[0] = lse


def _out_ffn_kernel(x_ref, omoba_ref, ofox_ref, omla_ref, od1_ref, od4_ref, od16_ref, l1_ref, l4_ref, l16_ref,
                    wout_ref, gpost_ref, gpre_ref, wg_ref, wu_ref, wd_ref, gpf_ref, out_ref, acc_ref,
                    dscr_ref):
    bf16 = jnp.bfloat16
    tm = x_ref.shape[0]

    def token_order(ref, slot, dil):
        halves = GROUP_W // 128
        for rho in range(dil):
            for c in range(halves):
                lanes = slice(rho * GROUP_W + c * 128, rho * GROUP_W + (c + 1) * 128)
                dscr_ref[slot + c, pl.ds(rho, tm // dil, stride=dil), :] = ref[:, lanes]
        return jnp.concatenate([dscr_ref[slot + c] for c in range(halves)], axis=1)

    l1, od1 = l1_ref[...], od1_ref[...]
    l4, od4 = token_order(l4_ref, 0, 4), token_order(od4_ref, 2, 4)
    l16, od16 = token_order(l16_ref, 4, 16), token_order(od16_ref, 6, 16)
    m = jnp.maximum(jnp.maximum(l1, l4), l16)
    e1, e4, e16 = jnp.exp2(l1 - m), jnp.exp2(l4 - m), jnp.exp2(l16 - m)
    odil = (e1 * od1 + e4 * od4 + e16 * od16) / (e1 + e4 + e16)
    y = (_dot(omoba_ref[...], wout_ref[0]) + _dot(ofox_ref[...], wout_ref[1])
         + _dot(omla_ref[...], wout_ref[2]) + _dot(odil.astype(bf16), wout_ref[3]))
    x1 = x_ref[...] + y * _rms_scale(y, D_MODEL) * gpost_ref[...]
    hb = (x1 * _rms_scale(x1, D_MODEL) * gpre_ref[...]).astype(bf16)
    acc_ref[...] = jnp.zeros_like(acc_ref)
    for c in range(D_FF // FF_CHUNK):
        cols = slice(c * FF_CHUNK, (c + 1) * FF_CHUNK)
        g = _dot(hb, wg_ref[:, cols])
        u = _dot(hb, wu_ref[:, cols])
        f = (g * jax.nn.sigmoid(g) * u).astype(bf16)
        acc_ref[...] += _dot(f, wd_ref[cols, :])
    f = acc_ref[...]
    out_ref[...] = x1 + f * _rms_scale(f, D_MODEL) * gpf_ref[...]


def _rope_tables(positions):
    t = positions.reshape(-1).astype(jnp.float32)

    def tab(dim):
        inv = ROPE_THETA ** (-jnp.arange(0, dim, 2, dtype=jnp.float32) / dim)
        ang = t[:, None] * inv
        return jnp.cos(ang), jnp.sin(ang)

    c64, s64 = tab(HEAD_DIM)
    c16, s16 = tab(MLA_ROPE)
    ct64, st64 = jnp.tile(c64, (1, 4)), jnp.tile(s64, (1, 4))
    pad = jnp.zeros((t.shape[0], 128 - MLA_ROPE), jnp.float32)
    ct16 = jnp.concatenate([c16, c16, pad], axis=1)
    st16 = jnp.concatenate([s16, s16, pad], axis=1)
    cT16, sT16 = jnp.tile(c16, (1, 4)).T, jnp.tile(s16, (1, 4)).T
    return ct64, st64, ct64.T, st64.T, ct16, st16, cT16, sT16


_HALF_PERM = np.array([h * 64 + half * 32 + j for half in (0, 1) for h in range(4) for j in range(32)])
_QROPE_ROWS = np.array([h * 96 + 64 + half * 16 + j for half in (0, 1) for h in range(4) for j in range(16)])
_QNOPE_ROWS = np.array([h * 96 + j for h in range(4) for j in range(64)])
_KNOPE_COLS = np.array([h * 128 + j for h in range(4) for j in range(64)])
_VMLA_COLS = np.array([h * 128 + 64 + j for h in range(4) for j in range(64)])


def _prep_layer_weights(w_in, b_forget, g_mla_q, w_mla_q_up, g_mla_kv, w_mla_kv_up, w_out, w_gate, w_up, w_down):
    bf16 = jnp.bfloat16
    depth = w_in.shape[0]
    sl = lambda a, b: w_in[:, :, a:b]
    scale = HEAD_DIM ** -0.5 * LOG2E
    moba_q, moba_k, moba_v = sl(0, 256)[..., _HALF_PERM] * scale, sl(256, 512)[..., _HALF_PERM], sl(512, 768)
    fox_q, fox_k, fox_v = sl(768, 1024) * scale, sl(1024, 1280), sl(1280, 1536)
    fg, cq, ckv, kr = sl(1536, 1540), sl(1540, 1732), sl(1732, 1860), sl(1860, 1892)
    dil_q, dil_k, dil_v = sl(1892, 2148)[..., _HALF_PERM] * scale, sl(2148, 2404)[..., _HALF_PERM], sl(2404, 2660)
    kr_rot = jnp.concatenate([-kr[..., 16:], kr[..., :16]], axis=-1)
    z = lambda n: jnp.zeros((depth, D_MODEL, n), w_in.dtype)
    wtok = jnp.concatenate([moba_k, fox_k, dil_q, dil_k, dil_v, cq, z(256 - MLA_Q_RANK), ckv,
                            kr, fg, z(128 - 36), kr_rot, z(96)], axis=-1).astype(bf16)
    wtr = jnp.swapaxes(jnp.concatenate([moba_q, moba_v, fox_q, fox_v], axis=-1), 1, 2).astype(bf16)
    bfg = jnp.zeros((depth, 1, 128), jnp.float32).at[:, 0, _FG_LANE:_FG_LANE + N_HEADS].set(b_forget)
    gq = jnp.pad(g_mla_q, ((0, 0), (0, 256 - MLA_Q_RANK)))[:, None, :]
    wq_rows = jnp.swapaxes(w_mla_q_up, 1, 2)
    wqT = jnp.concatenate([wq_rows[:, _QNOPE_ROWS], wq_rows[:, _QROPE_ROWS]], axis=1)
    wqT = jnp.pad(wqT, ((0, 0), (0, 0), (0, 256 - MLA_Q_RANK))).astype(bf16)
    gkv = g_mla_kv[:, None, :]
    wkn = w_mla_kv_up[:, :, _KNOPE_COLS].astype(bf16)
    wvT = jnp.swapaxes(w_mla_kv_up[:, :, _VMLA_COLS], 1, 2).astype(bf16)
    wout = w_out.reshape(depth, 4, GROUP_W, D_MODEL).astype(bf16)
    wg, wu, wd = w_gate.astype(bf16), w_up.astype(bf16), w_down.astype(bf16)
    return wtok, wtr, bfg, gq, wqT, gkv, wkn, wvT, wout, wg, wu, wd


def _const_spec(shape):
    return pl.BlockSpec(shape, lambda *_: (0,) * len(shape))


def _params(*sem):
    return pltpu.CompilerParams(dimension_semantics=sem, vmem_limit_bytes=VMEM_LIMIT)


def _in_proj(x2, g, wtok, wtr, tri, tables, bfg, gq, wqT, gkv, wkn, wvT, seq):
    t = x2.shape[0]
    tm = ROW_TILE
    nt = t // tm
    bf16, f32 = jnp.bfloat16, jnp.float32
    ct64, st64, cT64, sT64, ct16, st16, cT16, sT16 = tables
    tok_spec = lambda w: pl.BlockSpec((tm, w), lambda i: (i, 0))
    tr_spec = lambda r: pl.BlockSpec((r, tm), lambda i: (0, i))
    blk3 = lambda n, r, c: pl.BlockSpec((n, r, c), lambda i: (i, 0, 0))
    mb = tm // MOBA_BLOCK
    ab = tm // KEY_BLOCK
    vrows = N_HEADS * V_ROWS
    vt_shape = jax.ShapeDtypeStruct((t // KEY_BLOCK, vrows, KEY_BLOCK), bf16)
    in_specs = [tok_spec(D_MODEL), _const_spec((1, D_MODEL)), _const_spec(wtok.shape), _const_spec(wtr.shape),
                _const_spec(tri.shape), tok_spec(128), tok_spec(128), tr_spec(128), tr_spec(128),
                tok_spec(128), tok_spec(128), tr_spec(64), tr_spec(64), _const_spec((1, 128)),
                _const_spec((1, 256)), _const_spec(wqT.shape), _const_spec((1, 128)), _const_spec(wkn.shape),
                _const_spec(wvT.shape)]
    out_shape = [
        jax.ShapeDtypeStruct((t, GROUP_W), bf16),
        jax.ShapeDtypeStruct((nt, mb, GROUP_W), f32),
        jax.ShapeDtypeStruct((GROUP_W, t), bf16),
        vt_shape,
        jax.ShapeDtypeStruct((t, 512), bf16),
        jax.ShapeDtypeStruct((GROUP_W, t), bf16),
        vt_shape,
        jax.ShapeDtypeStruct((nt, 1, 128), f32),
        *[jax.ShapeDtypeStruct((t // d, d * GROUP_W), bf16) for d in DILATIONS] * 3,
        jax.ShapeDtypeStruct((t, 512), bf16),
        jax.ShapeDtypeStruct((384, t), bf16),
        vt_shape,
    ]
    vt_spec = blk3(ab, vrows, KEY_BLOCK)
    out_specs = [tok_spec(GROUP_W), blk3(1, mb, GROUP_W), tr_spec(GROUP_W), vt_spec,
                 tok_spec(512), tr_spec(GROUP_W), vt_spec, blk3(1, 1, 128),
                 *[pl.BlockSpec((tm // d, d * GROUP_W), lambda i: (i, 0)) for d in DILATIONS] * 3,
                 tok_spec(512), tr_spec(384), vt_spec]
    return pl.pallas_call(
        functools.partial(_in_proj_kernel, tiles_per_seq=seq // tm),
        grid=(nt,), in_specs=in_specs, out_specs=out_specs, out_shape=out_shape,
        scratch_shapes=[pltpu.VMEM((1, 128), f32), pltpu.VMEM((3 * GROUP_W // 128, tm, 128), f32)],
        compiler_params=_params("arbitrary"), name="in_proj",
    )(x2, g, wtok, wtr, tri, ct64, st64, cT64, sT64, ct16, st16, cT16, sT16, bfg, gq, wqT, gkv, wkn, wvT)


def _attention_scratch(tq, tk=KEY_BLOCK):
    return [pltpu.VMEM((GROUP_W, N_HEADS * tq), jnp.bfloat16),
            pltpu.VMEM((8, tq), jnp.float32),
            pltpu.VMEM((N_HEADS, V_ROWS, tq), jnp.float32),
            pltpu.VMEM((tk, N_HEADS * tq), jnp.float32),
            pltpu.VMEM((tk, N_HEADS * tq), jnp.float32),
            pltpu.VMEM((8, N_HEADS * tq), jnp.float32),
            pltpu.VMEM((8, N_HEADS * tq), jnp.float32)]


def _dense_attention(body, name, qT, k, vT, extra, batch, seq):
    tq = ATT_TILE
    nq = seq // tq
    t = batch * seq
    once = pl.Buffered(1)
    in_specs = [pl.BlockSpec((qT.shape[0], tq), lambda b, i: (0, b * nq + i)),
                pl.BlockSpec((seq, k.shape[1]), lambda b, i: (b, 0), pipeline_mode=once),
                pl.BlockSpec((seq // KEY_BLOCK, N_HEADS * V_ROWS, KEY_BLOCK), lambda b, i: (b, 0, 0),
                             pipeline_mode=once)]
    in_specs += [pl.BlockSpec((seq // KEY_BLOCK, 1, 128), lambda b, i: (b, 0, 0)) for _ in extra]
    return pl.pallas_call(
        body, grid=(batch, nq), in_specs=in_specs,
        out_specs=pl.BlockSpec((tq, GROUP_W), lambda b, i: (b * nq + i, 0)),
        out_shape=jax.ShapeDtypeStruct((t, GROUP_W), jnp.bfloat16),
        scratch_shapes=_attention_scratch(tq),
        compiler_params=_params("arbitrary", "arbitrary"), name=name,
    )(qT, k, vT, *extra)


def _moba_attention(qT, k, vT, kmean, batch, seq):
    tq = ATT_TILE
    nq = seq // tq
    nb = seq // MOBA_BLOCK
    t = batch * seq
    return pl.pallas_call(
        _moba_kernel, grid=(batch, nq),
        in_specs=[pl.BlockSpec((GROUP_W, tq), lambda b, i: (0, b * nq + i)),
                  pl.BlockSpec((seq, GROUP_W), lambda b, i: (b, 0), pipeline_mode=pl.Buffered(1)),
                  pl.BlockSpec((seq // KEY_BLOCK, N_HEADS * V_ROWS, KEY_BLOCK), lambda b, i: (b, 0, 0),
                               pipeline_mode=pl.Buffered(1)),
                  pl.BlockSpec((1, nb, GROUP_W), lambda b, i: (b, 0, 0))],
        out_specs=pl.BlockSpec((tq, GROUP_W), lambda b, i: (b * nq + i, 0)),
        out_shape=jax.ShapeDtypeStruct((t, GROUP_W), jnp.bfloat16),
        scratch_shapes=_attention_scratch(tq) + [pltpu.VMEM((N_HEADS * nb, tq), jnp.float32)],
        compiler_params=_params("arbitrary", "arbitrary"), name="moba_attention",
    )(qT, k, vT, kmean)


def _dilated_attention(q, k, v, batch, seq, dil):
    n = seq // dil
    tq = n if n <= 2 * DIL_TILE else DIL_TILE
    window = min(tq + DIL_WINDOW_STEPS, n)
    view = lambda a: a.reshape(batch, n, dil * GROUP_W)
    qspec = pl.BlockSpec((1, tq, GROUP_W), lambda b, r, i: (b, i, r))
    kspec = pl.BlockSpec((1, n, GROUP_W), lambda b, r, i: (b, 0, r))
    shape = jax.ShapeDtypeStruct((batch, n, dil * GROUP_W), jnp.float32)
    o, lse = pl.pallas_call(
        functools.partial(_dilated_kernel, window=window), grid=(batch, dil, n // tq),
        in_specs=[qspec, kspec, kspec], out_specs=[qspec, qspec], out_shape=[shape, shape],
        compiler_params=_params("arbitrary", "arbitrary", "arbitrary"), name=f"dilated_attention_d{dil}",
    )(view(q), view(k), view(v))
    return o.reshape(batch * n, dil * GROUP_W), lse.reshape(batch * n, dil * GROUP_W)


def _out_ffn(x2, omoba, ofox, omla, dil_outs, wout, gpost, gpre, wg, wu, wd, gpf):
    t = x2.shape[0]
    tm = ROW_TILE
    row = lambda w: pl.BlockSpec((tm, w), lambda i: (i, 0))
    (od1, l1), (od4, l4), (od16, l16) = dil_outs
    dil_specs = [pl.BlockSpec((tm // d, d * GROUP_W), lambda i: (i, 0)) for d in DILATIONS]
    in_specs = ([row(D_MODEL)] + [row(GROUP_W)] * 3 + dil_specs * 2
                + [_const_spec(wout.shape), _const_spec((1, D_MODEL)), _const_spec((1, D_MODEL)),
                   _const_spec(wg.shape), _const_spec(wu.shape), _const_spec(wd.shape), _const_spec((1, D_MODEL))])
    return pl.pallas_call(
        _out_ffn_kernel, grid=(t // tm,), in_specs=in_specs, out_specs=row(D_MODEL),
        out_shape=jax.ShapeDtypeStruct((t, D_MODEL), jnp.float32),
        scratch_shapes=[pltpu.VMEM((tm, D_MODEL), jnp.float32), pltpu.VMEM((4 * GROUP_W // 128, tm, 128), jnp.float32)],
        compiler_params=_params("arbitrary"), name="out_ffn",
    )(x2, omoba, ofox, omla, od1, od4, od16, l1, l4, l16, wout, gpost, gpre, wg, wu, wd, gpf)


def kernel(x, positions, w_in, b_forget, g_mla_q, w_mla_q_up, g_mla_kv, w_mla_kv_up, w_out, g_pre_mix, g_post_mix, w_gate, w_up, w_down, g_pre_ffn, g_post_ffn):
    batch, seq, _ = x.shape
    depth = w_in.shape[0]
    assert seq % ROW_TILE == 0 and seq % (DILATIONS[-1] * DIL_WINDOW_STEPS) == 0
    assert ROW_TILE == KEY_BLOCK
    tables = _rope_tables(positions)
    wtok, wtr, bfg, gq, wqT, gkv, wkn, wvT, wout, wg, wu, wd = _prep_layer_weights(
        w_in, b_forget, g_mla_q, w_mla_q_up, g_mla_kv, w_mla_kv_up, w_out, w_gate, w_up, w_down)
    tri = jnp.tril(jnp.ones((ROW_TILE, ROW_TILE), jnp.bfloat16))
    x2 = x.reshape(batch * seq, D_MODEL)
    for l in range(depth):
        (k_moba, kmean, qT_moba, vT_moba, k_fox, qT_fox, vT_fox, cbase, *dil_qkv,
         k_mla, qT_mla, vT_mla) = _in_proj(x2, g_pre_mix[l][None], wtok[l], wtr[l], tri, tables, bfg[l], gq[l],
                                          wqT[l], gkv[l], wkn[l], wvT[l], seq)
        nd = len(DILATIONS)
        o_moba = _moba_attention(qT_moba, k_moba, vT_moba, kmean.reshape(batch, seq // MOBA_BLOCK, GROUP_W),
                                 batch, seq)
        o_fox = _dense_attention(_fox_kernel, "fox_attention", qT_fox, k_fox, vT_fox, [cbase], batch, seq)
        o_mla = _dense_attention(_mla_kernel, "mla_attention", qT_mla, k_mla, vT_mla, [], batch, seq)
        dil_outs = [_dilated_attention(dil_qkv[n], dil_qkv[nd + n], dil_qkv[2 * nd + n], batch, seq, d)
                    for n, d in enumerate(DILATIONS)]
        x2 = _out_ffn(x2, o_moba, o_fox, o_mla, dil_outs, wout[l], g_post_mix[l][None], g_pre_ffn[l][None],
                      wg[l], wu[l], wd[l], g_post_ffn[l][None])
    return x2.reshape(batch, seq, D_MODEL)
```

```python
import functools

import numpy as np
import jax
import jax.numpy as jnp
from jax import lax
from jax.experimental import pallas as pl
from jax.experimental.pallas import tpu as pltpu

D_MODEL = 1024
HEAD_DIM = 64
N_HEADS = 4
GROUP_W = N_HEADS * HEAD_DIM
MOBA_BLOCK = 256
MOBA_TOPK = 3
MLA_Q_RANK = 192
MLA_KV_RANK = 128
MLA_NOPE = 64
MLA_ROPE = 32
DIL_WINDOW_STEPS = 128
DILATIONS = (1, 4, 16)
ROPE_THETA = 10000.0
RMS_EPS = 1e-6
D_FF = 2816
FF_CHUNK = 256
V_ROWS = 80
LOG2E = 1.4426950408889634
MASKED_LOGIT = -1e30

ROW_TILE = 512
ATT_TILE = 512
KEY_BLOCK = 512
SLAB = MOBA_BLOCK
DIL_TILE = 256
VMEM_LIMIT = 56 * 1024 * 1024

_C_KMOBA, _C_KFOX, _C_QDIL, _C_KDIL, _C_VDIL, _C_CQ = 0, 256, 512, 768, 1024, 1280
_C_CKV, _C_X, _C_Y, _N_TOK = 1536, 1664, 1792, 1920
_FG_LANE = 32

_NT = (((1,), (1,)), ((), ()))


def _dot(a, b):
    return jnp.dot(a, b, preferred_element_type=jnp.float32)


def _dot_nt(a, b):
    return lax.dot_general(a, b, _NT, preferred_element_type=jnp.float32)


def _rms_scale(v, n):
    return lax.rsqrt(jnp.sum(v * v, axis=-1, keepdims=True) * (1.0 / n) + RMS_EPS)


def _in_proj_kernel(x_ref, g_ref, wtok_ref, wtr_ref, tri_ref, ct64_ref, st64_ref, cT64_ref, sT64_ref,
                    ct16_ref, st16_ref, cT16_ref, sT16_ref, bfg_ref, gq_ref, wqT_ref, gkv_ref, wkn_ref, wvT_ref,
                    kmoba_ref, kmean_ref, qTmoba_ref, vTmoba_ref, kfox_ref, qTfox_ref, vTfox_ref, cbase_ref,
                    q1_ref, q4_ref, q16_ref, k1_ref, k4_ref, k16_ref, v1_ref, v4_ref, v16_ref,
                    kmla_ref, qTmla_ref, vTmla_ref, carry_ref, dscr_ref, *, tiles_per_seq):
    bf16 = jnp.bfloat16
    qdil_refs, kdil_refs, vdil_refs = (q1_ref, q4_ref, q16_ref), (k1_ref, k4_ref, k16_ref), (v1_ref, v4_ref, v16_ref)
    x = x_ref[...]
    hb = (x * _rms_scale(x, D_MODEL) * g_ref[...]).astype(bf16)
    tm = x.shape[0]

    def tok(c0, w):
        return _dot(hb, wtok_ref[:, c0:c0 + w])

    def rope_tok(z):
        c, s = ct64_ref[...], st64_ref[...]
        x1, x2 = z[:, :128], z[:, 128:]
        return jnp.concatenate([x1 * c - x2 * s, x2 * c + x1 * s], axis=1)

    def with_ones(vT):
        ones = jnp.ones((V_ROWS - HEAD_DIM, tm), bf16)
        parts = []
        for h in range(N_HEADS):
            parts += [vT[h * HEAD_DIM:(h + 1) * HEAD_DIM].astype(bf16), ones]
        return jnp.concatenate(parts, axis=0)

    def rope_tr(zT):
        c, s = cT64_ref[...], sT64_ref[...]
        x1, x2 = zT[:128], zT[128:]
        return jnp.concatenate([x1 * c - x2 * s, x2 * c + x1 * s], axis=0)

    k_moba = rope_tok(tok(_C_KMOBA, GROUP_W))
    kmoba_ref[...] = k_moba.astype(bf16)
    for blk in range(tm // MOBA_BLOCK):
        kmean_ref[0, blk:blk + 1, :] = jnp.mean(k_moba[blk * MOBA_BLOCK:(blk + 1) * MOBA_BLOCK], axis=0, keepdims=True)
    qTmoba_ref[...] = rope_tr(_dot_nt(wtr_ref[0:256, :], hb)).astype(bf16)
    def store_value_blocks(ref, vT):
        for blk in range(tm // KEY_BLOCK):
            ref[blk] = vT[:, blk * KEY_BLOCK:(blk + 1) * KEY_BLOCK]

    store_value_blocks(vTmoba_ref, with_ones(_dot_nt(wtr_ref[256:512, :], hb)))

    qTfox_ref[...] = _dot_nt(wtr_ref[512:768, :], hb).astype(bf16)
    store_value_blocks(vTfox_ref, with_ones(_dot_nt(wtr_ref[768:1024, :], hb)))

    def emit_dilated(refs, slot, z):
        refs[0][...] = z.astype(bf16)
        for c in range(GROUP_W // 128):
            dscr_ref[slot + c] = z[:, c * 128:(c + 1) * 128]
        for ref, dil in zip(refs[1:], DILATIONS[1:]):
            for rho in range(dil):
                for c in range(GROUP_W // 128):
                    rows = dscr_ref[slot + c, pl.ds(rho, tm // dil, stride=dil), :]
                    ref[:, rho * GROUP_W + c * 128:rho * GROUP_W + (c + 1) * 128] = rows.astype(bf16)

    emit_dilated(qdil_refs, 0, rope_tok(tok(_C_QDIL, GROUP_W)))
    emit_dilated(kdil_refs, 2, rope_tok(tok(_C_KDIL, GROUP_W)))
    emit_dilated(vdil_refs, 4, tok(_C_VDIL, GROUP_W))

    xy = tok(_C_X, 256)
    xblk = xy[:, :128]
    yblk = xy[:, 128:]
    fg = xblk + bfg_ref[...]
    logf = jnp.minimum(fg, 0.0) - jnp.log1p(jnp.exp(-jnp.abs(fg)))
    lane = lax.broadcasted_iota(jnp.int32, logf.shape, 1)
    is_gate = (lane >= _FG_LANE) & (lane < _FG_LANE + N_HEADS)

    def three_pieces(v):
        v = jnp.where(is_gate, v, 0.0)
        p1 = v.astype(bf16).astype(jnp.float32)
        p2 = (v - p1).astype(bf16).astype(jnp.float32)
        p3 = (v - p1 - p2).astype(bf16).astype(jnp.float32)
        return p1 + pltpu.roll(p2, 8, 1) + pltpu.roll(p3, 16, 1)

    @pl.when(pl.program_id(0) % tiles_per_seq == 0)
    def _():
        carry_ref[...] = jnp.zeros_like(carry_ref)

    part = _dot(tri_ref[...], three_pieces(logf).astype(bf16))
    cum = part + pltpu.roll(part, 128 - 8, 1) + pltpu.roll(part, 128 - 16, 1) + carry_ref[...]
    carry_ref[...] = cum[tm - 1:tm, :]
    cbase_ref[0] = cum[0:1, :]

    dcols = three_pieces((cum[0:1, :] - cum) * LOG2E)
    kf = tok(_C_KFOX, GROUP_W)
    kfox_ref[...] = jnp.concatenate([kf[:, :128], dcols, kf[:, 128:], dcols], axis=1).astype(bf16)

    zcq = tok(_C_CQ, 256)
    cq = (zcq * _rms_scale(zcq, MLA_Q_RANK) * gq_ref[...]).astype(bf16)
    qcT = _dot_nt(wqT_ref[...], cq)
    c16, s16 = cT16_ref[...], sT16_ref[...]
    r1h, r2h = qcT[256:320], qcT[320:384]
    qT = jnp.concatenate([qcT[0:256], r1h * c16 - r2h * s16, r2h * c16 + r1h * s16], axis=0)
    qTmla_ref[...] = (qT * ((MLA_NOPE + MLA_ROPE) ** -0.5 * LOG2E)).astype(bf16)

    zckv = tok(_C_CKV, MLA_KV_RANK)
    ckv = (zckv * _rms_scale(zckv, MLA_KV_RANK) * gkv_ref[...]).astype(bf16)
    kn = _dot(ckv, wkn_ref[...])
    krope = xblk * ct16_ref[...] + yblk * st16_ref[...]
    kmla_ref[...] = jnp.concatenate([kn[:, :128], krope, kn[:, 128:], krope], axis=1).astype(bf16)
    store_value_blocks(vTmla_ref, with_ones(_dot_nt(wvT_ref[...], ckv)))


def _softmax_update(parts, vb, m_ref, acc_ref, h, offsets=None, maxes=None):
    if offsets is None:
        offsets = [None] * len(parts)
    if maxes is None:
        maxes = [jnp.max(s, axis=0, keepdims=True) for s in parts]
    m_old = m_ref[h:h + 1, :]
    blk_max = None
    for mx, off in zip(maxes, offsets):
        mx = mx if off is None else mx + off
        blk_max = mx if blk_max is None else jnp.maximum(blk_max, mx)
    m_new = jnp.maximum(m_old, blk_max)
    ps = []
    for s, off in zip(parts, offsets):
        shift = m_new if off is None else m_new - off
        ps.append(jnp.exp2(s - shift).astype(jnp.bfloat16))
    p = ps[0] if len(ps) == 1 else jnp.concatenate(ps, axis=0)
    acc_ref[h] = jnp.exp2(m_old - m_new) * acc_ref[h] + _dot(vb, p)
    m_ref[h:h + 1, :] = m_new


def _store_scores(buf, cols, s):
    s_ref, mx_ref = buf
    s_ref[:, cols] = s
    for n in range(s.shape[0] // SLAB):
        mx_ref[n:n + 1, cols] = jnp.max(s[n * SLAB:(n + 1) * SLAB], axis=0, keepdims=True)


def _slabs(buf, h, tq):
    s_ref, mx_ref = buf
    cols = slice(h * tq, (h + 1) * tq)
    n_slabs = s_ref.shape[0] // SLAB
    return ([s_ref[n * SLAB:(n + 1) * SLAB, cols] for n in range(n_slabs)],
            [mx_ref[n:n + 1, cols] for n in range(n_slabs)])


def _own_block_mask(i, tq, tk):
    key = lax.broadcasted_iota(jnp.int32, (tk, tq), 0)
    qry = lax.broadcasted_iota(jnp.int32, (tk, tq), 1) + (i % (tk // tq)) * tq
    return key <= qry


def _init_state(m_ref, acc_ref):
    m_ref[...] = jnp.full(m_ref.shape, -jnp.inf, jnp.float32)
    acc_ref[...] = jnp.zeros(acc_ref.shape, jnp.float32)


def _finalize(acc_ref, o_ref):
    outs = [acc_ref[h, 0:HEAD_DIM, :] / acc_ref[h, HEAD_DIM:HEAD_DIM + 1, :] for h in range(N_HEADS)]
    o_ref[...] = jnp.concatenate(outs, axis=0).T.astype(o_ref.dtype)


def _vblock(vT_ref, j, h):
    return vT_ref[j, h * V_ROWS:(h + 1) * V_ROWS, :]


def _run_key_blocks(own, last, buf_a, buf_b, scores_into, consume):
    scores_into(buf_a, own)
    scores_into(buf_b, 0)
    consume(buf_a, own, True)

    def pair(p, carry):
        j0 = 2 * p
        scores_into(buf_a, j0 + 1)
        consume(buf_b, j0, False)
        scores_into(buf_b, jnp.minimum(j0 + 2, last))
        consume(buf_a, j0 + 1, False)
        return carry

    lax.fori_loop(0, own // 2, pair, 0)

    @pl.when(own % 2 == 1)
    def _():
        consume(buf_b, own - 1, False)


def _key_rows(k_ref, j, tk):
    return k_ref[pl.ds(pl.multiple_of(j * tk, tk), tk), :]


def _causal_slabs(parts, causal):
    return [jnp.where(causal[n * SLAB:(n + 1) * SLAB], s, -jnp.inf) for n, s in enumerate(parts)]


def _two_tile_scores(buf, k_ref, qcat_ref, j, tk, tq):
    kb = _key_rows(k_ref, j, tk)
    for half in range(2):
        cols = slice(2 * half * tq, 2 * (half + 1) * tq)
        _store_scores(buf, cols, _dot(kb[:, 256 * half:256 * (half + 1)], qcat_ref[:, cols]))


def _fox_kernel(qT_ref, k_ref, vT_ref, cbase_ref, o_ref, qcat_ref, m_ref, acc_ref, sa_ref, sb_ref, mxa_ref, mxb_ref):
    i = pl.program_id(1)
    tq = qT_ref.shape[1]
    zeros64 = jnp.zeros((HEAD_DIM, tq), jnp.bfloat16)
    row = lax.broadcasted_iota(jnp.int32, (128, tq), 0)
    for h in range(N_HEADS):
        qh = qT_ref[h * HEAD_DIM:(h + 1) * HEAD_DIM, :]
        pick = (row == _FG_LANE + h) | (row == _FG_LANE + 8 + h) | (row == _FG_LANE + 16 + h)
        rows = ([qh, zeros64] if h % 2 == 0 else [zeros64, qh]) + [pick.astype(jnp.bfloat16)]
        qcat_ref[:, h * tq:(h + 1) * tq] = jnp.concatenate(rows, axis=0)
    _init_state(m_ref, acc_ref)
    tk = sa_ref.shape[0]
    assert tq == tk
    causal = _own_block_mask(i, tq, tk)

    def scores_into(buf, j):
        _two_tile_scores(buf, k_ref, qcat_ref, j, tk, tq)

    def consume(buf, j, own):
        offs = (cbase_ref[i] - cbase_ref[j]) * LOG2E
        for h in range(N_HEADS):
            parts, maxes = _slabs(buf, h, tq)
            if own:
                parts, maxes = _causal_slabs(parts, causal), None
            off = offs[:, _FG_LANE + h:_FG_LANE + h + 1]
            _softmax_update(parts, _vblock(vT_ref, j, h), m_ref, acc_ref, h, [off] * len(parts), maxes)

    _run_key_blocks(i, k_ref.shape[0] // tk - 1, (sa_ref, mxa_ref), (sb_ref, mxb_ref), scores_into, consume)
    _finalize(acc_ref, o_ref)


def _mla_kernel(qT_ref, k_ref, vT_ref, o_ref, qcat_ref, m_ref, acc_ref, sa_ref, sb_ref, mxa_ref, mxb_ref):
    i = pl.program_id(1)
    tq = qT_ref.shape[1]
    zeros64 = jnp.zeros((MLA_NOPE, tq), jnp.bfloat16)
    zeros96 = jnp.zeros((256 - 128 - MLA_ROPE, tq), jnp.bfloat16)
    for h in range(N_HEADS):
        qn = qT_ref[h * MLA_NOPE:(h + 1) * MLA_NOPE, :]
        qr1 = qT_ref[256 + 16 * h:256 + 16 * (h + 1), :]
        qr2 = qT_ref[320 + 16 * h:320 + 16 * (h + 1), :]
        nope = [qn, zeros64] if h % 2 == 0 else [zeros64, qn]
        qcat_ref[:, h * tq:(h + 1) * tq] = jnp.concatenate(nope + [qr1, qr2, zeros96], axis=0)
    _init_state(m_ref, acc_ref)
    tk = sa_ref.shape[0]
    causal = _own_block_mask(i, tq, tk)

    def scores_into(buf, j):
        _two_tile_scores(buf, k_ref, qcat_ref, j, tk, tq)

    def consume(buf, j, own):
        for h in range(N_HEADS):
            parts, maxes = _slabs(buf, h, tq)
            if own:
                parts, maxes = _causal_slabs(parts, causal), None
            _softmax_update(parts, _vblock(vT_ref, j, h), m_ref, acc_ref, h, None, maxes)

    _run_key_blocks(i * tq // tk, k_ref.shape[0] // tk - 1, (sa_ref, mxa_ref), (sb_ref, mxb_ref), scores_into,
                    consume)
    _finalize(acc_ref, o_ref)


def _moba_kernel(qT_ref, k_ref, vT_ref, km_ref, o_ref, qcat_ref, m_ref, acc_ref, sa_ref, sb_ref, mxa_ref, mxb_ref,
                 bias_ref):
    i = pl.program_id(1)
    tq = qT_ref.shape[1]
    nb = km_ref.shape[1]
    q = qT_ref[...]
    row = lax.broadcasted_iota(jnp.int32, q.shape, 0)
    km = km_ref[0].astype(jnp.bfloat16)
    blk = lax.broadcasted_iota(jnp.int32, (nb, tq), 0)
    qblk = i * (tq // MOBA_BLOCK) + lax.broadcasted_iota(jnp.int32, (1, tq), 1) // MOBA_BLOCK
    neg_inf = jnp.float32(-jnp.inf)
    for h in range(N_HEADS):
        in_head = (((row >= 32 * h) & (row < 32 * (h + 1)))
                   | ((row >= 128 + 32 * h) & (row < 128 + 32 * (h + 1))))
        qm = jnp.where(in_head, q, jnp.zeros_like(q))
        qcat_ref[:, h * tq:(h + 1) * tq] = qm
        g = jnp.where(blk < qblk, _dot(km, qm), neg_inf)
        sel = jnp.zeros((nb, tq), jnp.bool_)
        for _ in range(MOBA_TOPK):
            mx = jnp.max(g, axis=0, keepdims=True)
            cand = jnp.where((g == mx) & (mx > neg_inf), blk, nb)
            chosen = blk == jnp.min(cand, axis=0, keepdims=True)
            sel = sel | chosen
            g = jnp.where(chosen, neg_inf, g)
        bias_ref[h * nb:(h + 1) * nb, :] = jnp.where(sel, 0.0, MASKED_LOGIT)
    _init_state(m_ref, acc_ref)
    tk = sa_ref.shape[0]
    per = tk // SLAB
    causal = _own_block_mask(i, tq, tk)

    def scores_into(buf, j):
        _store_scores(buf, slice(None), _dot(_key_rows(k_ref, j, tk), qcat_ref[...]))

    def consume(buf, j, own):
        for h in range(N_HEADS):
            parts, maxes = _slabs(buf, h, tq)
            gates = [bias_ref[pl.ds(h * nb + per * j + n, 1), :] for n in range(per)]
            if own:
                gates = [jnp.where(per * j + n < qblk, g, 0.0) for n, g in enumerate(gates)]
                parts, maxes = _causal_slabs(parts, causal), None
            _softmax_update(parts, _vblock(vT_ref, j, h), m_ref, acc_ref, h, gates, maxes)

    _run_key_blocks(i * tq // tk, k_ref.shape[0] // tk - 1, (sa_ref, mxa_ref), (sb_ref, mxb_ref), scores_into,
                    consume)
    _finalize(acc_ref, o_ref)


def _dilated_kernel(q_ref, k_ref, v_ref, o_ref, lse_ref, *, window):
    i = pl.program_id(2)
    tq = q_ref.shape[1]
    n = k_ref.shape[1]
    q = q_ref[0]
    a = i * tq
    ks = jnp.clip(a - DIL_WINDOW_STEPS, 0, n - window)
    ks = pl.multiple_of(ks, DIL_WINDOW_STEPS)
    kw = k_ref[0, pl.ds(ks, window), :]
    vw = v_ref[0, pl.ds(ks, window), :]
    jq = a + lax.broadcasted_iota(jnp.int32, (tq, window), 0)
    jk = ks + lax.broadcasted_iota(jnp.int32, (tq, window), 1)
    band = (jq - jk >= 0) & (jq - jk <= DIL_WINDOW_STEPS)
    lane = lax.broadcasted_iota(jnp.int32, (tq, GROUP_W), 1)
    o = jnp.zeros((tq, GROUP_W), jnp.float32)
    lse = jnp.zeros((tq, GROUP_W), jnp.float32)
    for h in range(N_HEADS):
        in_head = (((lane >= 32 * h) & (lane < 32 * (h + 1)))
                   | ((lane >= 128 + 32 * h) & (lane < 128 + 32 * (h + 1))))
        qm = jnp.where(in_head, q, jnp.zeros_like(q))
        s = jnp.where(band, _dot_nt(qm, kw), -jnp.inf)
        m = jnp.max(s, axis=1, keepdims=True)
        p = jnp.exp2(s - m)
        l = jnp.sum(p, axis=1, keepdims=True)
        oh = _dot(p.astype(jnp.bfloat16), vw) / l
        out_lanes = (lane >= h * HEAD_DIM) & (lane < (h + 1) * HEAD_DIM)
        o = jnp.where(out_lanes, oh, o)
        lse = jnp.where(out_lanes, m + jnp.log2(l), lse)
    o_ref[0] = o
    lse_ref[0] = lse


def _out_ffn_kernel(x_ref, omoba_ref, ofox_ref, omla_ref, od1_ref, od4_ref, od16_ref, l1_ref, l4_ref, l16_ref,
                    wout_ref, gpost_ref, gpre_ref, wg_ref, wu_ref, wd_ref, gpf_ref, out_ref, acc_ref,
                    dscr_ref):
    bf16 = jnp.bfloat16
    tm = x_ref.shape[0]

    def token_order(ref, slot, dil):
        halves = GROUP_W // 128
        for rho in range(dil):
            for c in range(halves):
                lanes = slice(rho * GROUP_W + c * 128, rho * GROUP_W + (c + 1) * 128)
                dscr_ref[slot + c, pl.ds(rho, tm // dil, stride=dil), :] = ref[:, lanes]
        return jnp.concatenate([dscr_ref[slot + c] for c in range(halves)], axis=1)

    l1, od1 = l1_ref[...], od1_ref[...]
    l4, od4 = token_order(l4_ref, 0, 4), token_order(od4_ref, 2, 4)
    l16, od16 = token_order(l16_ref, 4, 16), token_order(od16_ref, 6, 16)
    m = jnp.maximum(jnp.maximum(l1, l4), l16)
    e1, e4, e16 = jnp.exp2(l1 - m), jnp.exp2(l4 - m), jnp.exp2(l16 - m)
    odil = (e1 * od1 + e4 * od4 + e16 * od16) / (e1 + e4 + e16)
    y = (_dot(omoba_ref[...], wout_ref[0]) + _dot(ofox_ref[...], wout_ref[1])
         + _dot(omla_ref[...], wout_ref[2]) + _dot(odil.astype(bf16), wout_ref[3]))
    x1 = x_ref[...] + y * _rms_scale(y, D_MODEL) * gpost_ref[...]
    hb = (x1 * _rms_scale(x1, D_MODEL) * gpre_ref[...]).astype(bf16)
    acc_ref[...] = jnp.zeros_like(acc_ref)
    for c in range(D_FF // FF_CHUNK):
        cols = slice(c * FF_CHUNK, (c + 1) * FF_CHUNK)
        g = _dot(hb, wg_ref[:, cols])
        u = _dot(hb, wu_ref[:, cols])
        f = (g * jax.nn.sigmoid(g) * u).astype(bf16)
        acc_ref[...] += _dot(f, wd_ref[cols, :])
    f = acc_ref[...]
    out_ref[...] = x1 + f * _rms_scale(f, D_MODEL) * gpf_ref[...]


def _rope_tables(positions):
    t = positions.reshape(-1).astype(jnp.float32)

    def tab(dim):
        inv = ROPE_THETA ** (-jnp.arange(0, dim, 2, dtype=jnp.float32) / dim)
        ang = t[:, None] * inv
        return jnp.cos(ang), jnp.sin(ang)

    c64, s64 = tab(HEAD_DIM)
    c16, s16 = tab(MLA_ROPE)
    ct64, st64 = jnp.tile(c64, (1, 4)), jnp.tile(s64, (1, 4))
    pad = jnp.zeros((t.shape[0], 128 - MLA_ROPE), jnp.float32)
    ct16 = jnp.concatenate([c16, c16, pad], axis=1)
    st16 = jnp.concatenate([s16, s16, pad], axis=1)
    cT16, sT16 = jnp.tile(c16, (1, 4)).T, jnp.tile(s16, (1, 4)).T
    return ct64, st64, ct64.T, st64.T, ct16, st16, cT16, sT16


_HALF_PERM = np.array([h * 64 + half * 32 + j for half in (0, 1) for h in range(4) for j in range(32)])
_QROPE_ROWS = np.array([h * 96 + 64 + half * 16 + j for half in (0, 1) for h in range(4) for j in range(16)])
_QNOPE_ROWS = np.array([h * 96 + j for h in range(4) for j in range(64)])
_KNOPE_COLS = np.array([h * 128 + j for h in range(4) for j in range(64)])
_VMLA_COLS = np.array([h * 128 + 64 + j for h in range(4) for j in range(64)])


def _prep_layer_weights(w_in, b_forget, g_mla_q, w_mla_q_up, g_mla_kv, w_mla_kv_up, w_out, w_gate, w_up, w_down):
    bf16 = jnp.bfloat16
    depth = w_in.shape[0]
    sl = lambda a, b: w_in[:, :, a:b]
    scale = HEAD_DIM ** -0.5 * LOG2E
    moba_q, moba_k, moba_v = sl(0, 256)[..., _HALF_PERM] * scale, sl(256, 512)[..., _HALF_PERM], sl(512, 768)
    fox_q, fox_k, fox_v = sl(768, 1024) * scale, sl(1024, 1280), sl(1280, 1536)
    fg, cq, ckv, kr = sl(1536, 1540), sl(1540, 1732), sl(1732, 1860), sl(1860, 1892)
    dil_q, dil_k, dil_v = sl(1892, 2148)[..., _HALF_PERM] * scale, sl(2148, 2404)[..., _HALF_PERM], sl(2404, 2660)
    kr_rot = jnp.concatenate([-kr[..., 16:], kr[..., :16]], axis=-1)
    z = lambda n: jnp.zeros((depth, D_MODEL, n), w_in.dtype)
    wtok = jnp.concatenate([moba_k, fox_k, dil_q, dil_k, dil_v, cq, z(256 - MLA_Q_RANK), ckv,
                            kr, fg, z(128 - 36), kr_rot, z(96)], axis=-1).astype(bf16)
    wtr = jnp.swapaxes(jnp.concatenate([moba_q, moba_v, fox_q, fox_v], axis=-1), 1, 2).astype(bf16)
    bfg = jnp.zeros((depth, 1, 128), jnp.float32).at[:, 0, _FG_LANE:_FG_LANE + N_HEADS].set(b_forget)
    gq = jnp.pad(g_mla_q, ((0, 0), (0, 256 - MLA_Q_RANK)))[:, None, :]
    wq_rows = jnp.swapaxes(w_mla_q_up, 1, 2)
    wqT = jnp.concatenate([wq_rows[:, _QNOPE_ROWS], wq_rows[:, _QROPE_ROWS]], axis=1)
    wqT = jnp.pad(wqT, ((0, 0), (0, 0), (0, 256 - MLA_Q_RANK))).astype(bf16)
    gkv = g_mla_kv[:, None, :]
    wkn = w_mla_kv_up[:, :, _KNOPE_COLS].astype(bf16)
    wvT = jnp.swapaxes(w_mla_kv_up[:, :, _VMLA_COLS], 1, 2).astype(bf16)
    wout = w_out.reshape(depth, 4, GROUP_W, D_MODEL).astype(bf16)
    wg, wu, wd = w_gate.astype(bf16), w_up.astype(bf16), w_down.astype(bf16)
    return wtok, wtr, bfg, gq, wqT, gkv, wkn, wvT, wout, wg, wu, wd


def _const_spec(shape):
    return pl.BlockSpec(shape, lambda *_: (0,) * len(shape))


def _params(*sem):
    return pltpu.CompilerParams(dimension_semantics=sem, vmem_limit_bytes=VMEM_LIMIT)


def _in_proj(x2, g, wtok, wtr, tri, tables, bfg, gq, wqT, gkv, wkn, wvT, seq):
    t = x2.shape[0]
    tm = ROW_TILE
    nt = t // tm
    bf16, f32 = jnp.bfloat16, jnp.float32
    ct64, st64, cT64, sT64, ct16, st16, cT16, sT16 = tables
    tok_spec = lambda w: pl.BlockSpec((tm, w), lambda i: (i, 0))
    tr_spec = lambda r: pl.BlockSpec((r, tm), lambda i: (0, i))
    blk3 = lambda n, r, c: pl.BlockSpec((n, r, c), lambda i: (i, 0, 0))
    mb = tm // MOBA_BLOCK
    ab = tm // KEY_BLOCK
    vrows = N_HEADS * V_ROWS
    vt_shape = jax.ShapeDtypeStruct((t // KEY_BLOCK, vrows, KEY_BLOCK), bf16)
    in_specs = [tok_spec(D_MODEL), _const_spec((1, D_MODEL)), _const_spec(wtok.shape), _const_spec(wtr.shape),
                _const_spec(tri.shape), tok_spec(128), tok_spec(128), tr_spec(128), tr_spec(128),
                tok_spec(128), tok_spec(128), tr_spec(64), tr_spec(64), _const_spec((1, 128)),
                _const_spec((1, 256)), _const_spec(wqT.shape), _const_spec((1, 128)), _const_spec(wkn.shape),
                _const_spec(wvT.shape)]
    out_shape = [
        jax.ShapeDtypeStruct((t, GROUP_W), bf16),
        jax.ShapeDtypeStruct((nt, mb, GROUP_W), f32),
        jax.ShapeDtypeStruct((GROUP_W, t), bf16),
        vt_shape,
        jax.ShapeDtypeStruct((t, 512), bf16),
        jax.ShapeDtypeStruct((GROUP_W, t), bf16),
        vt_shape,
        jax.ShapeDtypeStruct((nt, 1, 128), f32),
        *[jax.ShapeDtypeStruct((t // d, d * GROUP_W), bf16) for d in DILATIONS] * 3,
        jax.ShapeDtypeStruct((t, 512), bf16),
        jax.ShapeDtypeStruct((384, t), bf16),
        vt_shape,
    ]
    vt_spec = blk3(ab, vrows, KEY_BLOCK)
    out_specs = [tok_spec(GROUP_W), blk3(1, mb, GROUP_W), tr_spec(GROUP_W), vt_spec,
                 tok_spec(512), tr_spec(GROUP_W), vt_spec, blk3(1, 1, 128),
                 *[pl.BlockSpec((tm // d, d * GROUP_W), lambda i: (i, 0)) for d in DILATIONS] * 3,
                 tok_spec(512), tr_spec(384), vt_spec]
    return pl.pallas_call(
        functools.partial(_in_proj_kernel, tiles_per_seq=seq // tm),
        grid=(nt,), in_specs=in_specs, out_specs=out_specs, out_shape=out_shape,
        scratch_shapes=[pltpu.VMEM((1, 128), f32), pltpu.VMEM((3 * GROUP_W // 128, tm, 128), f32)],
        compiler_params=_params("arbitrary"), name="in_proj",
    )(x2, g, wtok, wtr, tri, ct64, st64, cT64, sT64, ct16, st16, cT16, sT16, bfg, gq, wqT, gkv, wkn, wvT)


def _attention_scratch(tq, tk=KEY_BLOCK):
    return [pltpu.VMEM((GROUP_W, N_HEADS * tq), jnp.bfloat16),
            pltpu.VMEM((8, tq), jnp.float32),
            pltpu.VMEM((N_HEADS, V_ROWS, tq), jnp.float32),
            pltpu.VMEM((tk, N_HEADS * tq), jnp.float32),
            pltpu.VMEM((tk, N_HEADS * tq), jnp.float32),
            pltpu.VMEM((8, N_HEADS * tq), jnp.float32),
            pltpu.VMEM((8, N_HEADS * tq), jnp.float32)]


def _dense_attention(body, name, qT, k, vT, extra, batch, seq):
    tq = ATT_TILE
    nq = seq // tq
    t = batch * seq
    in_specs = [pl.BlockSpec((qT.shape[0], tq), lambda b, i: (0, b * nq + i)),
                pl.BlockSpec((seq, k.shape[1]), lambda b, i: (b, 0)),
                pl.BlockSpec((seq // KEY_BLOCK, N_HEADS * V_ROWS, KEY_BLOCK), lambda b, i: (b, 0, 0))]
    in_specs += [pl.BlockSpec((seq // KEY_BLOCK, 1, 128), lambda b, i: (b, 0, 0)) for _ in extra]
    return pl.pallas_call(
        body, grid=(batch, nq), in_specs=in_specs,
        out_specs=pl.BlockSpec((tq, GROUP_W), lambda b, i: (b * nq + i, 0)),
        out_shape=jax.ShapeDtypeStruct((t, GROUP_W), jnp.bfloat16),
        scratch_shapes=_attention_scratch(tq),
        compiler_params=_params("arbitrary", "arbitrary"), name=name,
    )(qT, k, vT, *extra)


def _moba_attention(qT, k, vT, kmean, batch, seq):
    tq = ATT_TILE
    nq = seq // tq
    nb = seq // MOBA_BLOCK
    t = batch * seq
    return pl.pallas_call(
        _moba_kernel, grid=(batch, nq),
        in_specs=[pl.BlockSpec((GROUP_W, tq), lambda b, i: (0, b * nq + i)),
                  pl.BlockSpec((seq, GROUP_W), lambda b, i: (b, 0)),
                  pl.BlockSpec((seq // KEY_BLOCK, N_HEADS * V_ROWS, KEY_BLOCK), lambda b, i: (b, 0, 0)),
                  pl.BlockSpec((1, nb, GROUP_W), lambda b, i: (b, 0, 0))],
        out_specs=pl.BlockSpec((tq, GROUP_W), lambda b, i: (b * nq + i, 0)),
        out_shape=jax.ShapeDtypeStruct((t, GROUP_W), jnp.bfloat16),
        scratch_shapes=_attention_scratch(tq) + [pltpu.VMEM((N_HEADS * nb, tq), jnp.float32)],
        compiler_params=_params("arbitrary", "arbitrary"), name="moba_attention",
    )(qT, k, vT, kmean)


def _dilated_attention(q, k, v, batch, seq, dil):
    n = seq // dil
    tq = n if n <= 2 * DIL_TILE else DIL_TILE
    window = min(tq + DIL_WINDOW_STEPS, n)
    view = lambda a: a.reshape(batch, n, dil * GROUP_W)
    qspec = pl.BlockSpec((1, tq, GROUP_W), lambda b, r, i: (b, i, r))
    kspec = pl.BlockSpec((1, n, GROUP_W), lambda b, r, i: (b, 0, r))
    shape = jax.ShapeDtypeStruct((batch, n, dil * GROUP_W), jnp.float32)
    o, lse = pl.pallas_call(
        functools.partial(_dilated_kernel, window=window), grid=(batch, dil, n // tq),
        in_specs=[qspec, kspec, kspec], out_specs=[qspec, qspec], out_shape=[shape, shape],
        compiler_params=_params("arbitrary", "arbitrary", "arbitrary"), name=f"dilated_attention_d{dil}",
    )(view(q), view(k), view(v))
    return o.reshape(batch * n, dil * GROUP_W), lse.reshape(batch * n, dil * GROUP_W)


def _out_ffn(x2, omoba, ofox, omla, dil_outs, wout, gpost, gpre, wg, wu, wd, gpf):
    t = x2.shape[0]
    tm = ROW_TILE
    row = lambda w: pl.BlockSpec((tm, w), lambda i: (i, 0))
    (od1, l1), (od4, l4), (od16, l16) = dil_outs
    dil_specs = [pl.BlockSpec((tm // d, d * GROUP_W), lambda i: (i, 0)) for d in DILATIONS]
    in_specs = ([row(D_MODEL)] + [row(GROUP_W)] * 3 + dil_specs * 2
                + [_const_spec(wout.shape), _const_spec((1, D_MODEL)), _const_spec((1, D_MODEL)),
                   _const_spec(wg.shape), _const_spec(wu.shape), _const_spec(wd.shape), _const_spec((1, D_MODEL))])
    return pl.pallas_call(
        _out_ffn_kernel, grid=(t // tm,), in_specs=in_specs, out_specs=row(D_MODEL),
        out_shape=jax.ShapeDtypeStruct((t, D_MODEL), jnp.float32),
        scratch_shapes=[pltpu.VMEM((tm, D_MODEL), jnp.float32), pltpu.VMEM((4 * GROUP_W // 128, tm, 128), jnp.float32)],
        compiler_params=_params("arbitrary"), name="out_ffn",
    )(x2, omoba, ofox, omla, od1, od4, od16, l1, l4, l16, wout, gpost, gpre, wg, wu, wd, gpf)


def kernel(x, positions, w_in, b_forget, g_mla_q, w_mla_q_up, g_mla_kv, w_mla_kv_up, w_out, g_pre_mix, g_post_mix, w_gate, w_up, w_down, g_pre_ffn, g_post_ffn):
    batch, seq, _ = x.shape
    depth = w_in.shape[0]
    assert seq % ROW_TILE == 0 and seq % (DILATIONS[-1] * DIL_WINDOW_STEPS) == 0
    assert ROW_TILE == KEY_BLOCK
    tables = _rope_tables(positions)
    wtok, wtr, bfg, gq, wqT, gkv, wkn, wvT, wout, wg, wu, wd = _prep_layer_weights(
        w_in, b_forget, g_mla_q, w_mla_q_up, g_mla_kv, w_mla_kv_up, w_out, w_gate, w_up, w_down)
    tri = jnp.tril(jnp.ones((ROW_TILE, ROW_TILE), jnp.bfloat16))
    x2 = x.reshape(batch * seq, D_MODEL)
    for l in range(depth):
        (k_moba, kmean, qT_moba, vT_moba, k_fox, qT_fox, vT_fox, cbase, *dil_qkv,
         k_mla, qT_mla, vT_mla) = _in_proj(x2, g_pre_mix[l][None], wtok[l], wtr[l], tri, tables, bfg[l], gq[l],
                                          wqT[l], gkv[l], wkn[l], wvT[l], seq)
        nd = len(DILATIONS)
        o_moba = _moba_attention(qT_moba, k_moba, vT_moba, kmean.reshape(batch, seq // MOBA_BLOCK, GROUP_W),
                                 batch, seq)
        o_fox = _dense_attention(_fox_kernel, "fox_attention", qT_fox, k_fox, vT_fox, [cbase], batch, seq)
        o_mla = _dense_attention(_mla_kernel, "mla_attention", qT_mla, k_mla, vT_mla, [], batch, seq)
        dil_outs = [_dilated_attention(dil_qkv[n], dil_qkv[nd + n], dil_qkv[2 * nd + n], batch, seq, d)
                    for n, d in enumerate(DILATIONS)]
        x2 = _out_ffn(x2, o_moba, o_fox, o_mla, dil_outs, wout[l], g_post_mix[l][None], g_pre_ffn[l][None],
                      wg[l], wu[l], wd[l], g_post_ffn[l][None])
    return x2.reshape(batch, seq, D_MODEL)
```

```python
import functools

import numpy as np
import jax
import jax.numpy as jnp
from jax import lax
from jax.experimental import pallas as pl
from jax.experimental.pallas import tpu as pltpu

D_MODEL = 1024
HEAD_DIM = 64
N_HEADS = 4
GROUP_W = N_HEADS * HEAD_DIM
MOBA_BLOCK = 256
MOBA_TOPK = 3
MLA_Q_RANK = 192
MLA_KV_RANK = 128
MLA_NOPE = 64
MLA_ROPE = 32
DIL_WINDOW_STEPS = 128
DILATIONS = (1, 4, 16)
ROPE_THETA = 10000.0
RMS_EPS = 1e-6
D_FF = 2816
FF_CHUNK = 256
V_ROWS = 80
LOG2E = 1.4426950408889634
MASKED_LOGIT = -1e30

ROW_TILE = 512
ATT_TILE = 512
KEY_BLOCK = 512
SLAB = MOBA_BLOCK
DIL_TILE = 256
VMEM_LIMIT = 56 * 1024 * 1024

_C_KMOBA, _C_KFOX, _C_QDIL, _C_KDIL, _C_VDIL, _C_CQ = 0, 256, 512, 768, 1024, 1280
_C_CKV, _C_X, _C_Y, _N_TOK = 1536, 1664, 1792, 1920
_FG_LANE = 32

_NT = (((1,), (1,)), ((), ()))


def _dot(a, b):
    return jnp.dot(a, b, preferred_element_type=jnp.float32)


def _dot_nt(a, b):
    return lax.dot_general(a, b, _NT, preferred_element_type=jnp.float32)


def _rms_scale(v, n):
    return lax.rsqrt(jnp.sum(v * v, axis=-1, keepdims=True) * (1.0 / n) + RMS_EPS)


def _in_proj_kernel(x_ref, g_ref, wtok_ref, wtr_ref, tri_ref, ct64_ref, st64_ref, cT64_ref, sT64_ref,
                    ct16_ref, st16_ref, cT16_ref, sT16_ref, bfg_ref, gq_ref, wqT_ref, gkv_ref, wkn_ref, wvT_ref,
                    kmoba_ref, kmean_ref, qTmoba_ref, vTmoba_ref, kfox_ref, qTfox_ref, vTfox_ref, cbase_ref,
                    q1_ref, q4_ref, q16_ref, k1_ref, k4_ref, k16_ref, v1_ref, v4_ref, v16_ref,
                    kmla_ref, qTmla_ref, vTmla_ref, carry_ref, dscr_ref, *, tiles_per_seq):
    bf16 = jnp.bfloat16
    qdil_refs, kdil_refs, vdil_refs = (q1_ref, q4_ref, q16_ref), (k1_ref, k4_ref, k16_ref), (v1_ref, v4_ref, v16_ref)
    x = x_ref[...]
    hb = (x * _rms_scale(x, D_MODEL) * g_ref[...]).astype(bf16)
    tm = x.shape[0]

    def tok(c0, w):
        return _dot(hb, wtok_ref[:, c0:c0 + w])

    def rope_tok(z):
        c, s = ct64_ref[...], st64_ref[...]
        x1, x2 = z[:, :128], z[:, 128:]
        return jnp.concatenate([x1 * c - x2 * s, x2 * c + x1 * s], axis=1)

    def with_ones(vT):
        ones = jnp.ones((V_ROWS - HEAD_DIM, tm), bf16)
        parts = []
        for h in range(N_HEADS):
            parts += [vT[h * HEAD_DIM:(h + 1) * HEAD_DIM].astype(bf16), ones]
        return jnp.concatenate(parts, axis=0)

    def rope_tr(zT):
        c, s = cT64_ref[...], sT64_ref[...]
        x1, x2 = zT[:128], zT[128:]
        return jnp.concatenate([x1 * c - x2 * s, x2 * c + x1 * s], axis=0)

    k_moba = rope_tok(tok(_C_KMOBA, GROUP_W))
    kmoba_ref[...] = k_moba.astype(bf16)
    for blk in range(tm // MOBA_BLOCK):
        kmean_ref[0, blk:blk + 1, :] = jnp.mean(k_moba[blk * MOBA_BLOCK:(blk + 1) * MOBA_BLOCK], axis=0, keepdims=True)
    qTmoba_ref[...] = rope_tr(_dot_nt(wtr_ref[0:256, :], hb)).astype(bf16)
    def store_value_blocks(ref, vT):
        for blk in range(tm // KEY_BLOCK):
            ref[blk] = vT[:, blk * KEY_BLOCK:(blk + 1) * KEY_BLOCK]

    store_value_blocks(vTmoba_ref, with_ones(_dot_nt(wtr_ref[256:512, :], hb)))

    qTfox_ref[...] = _dot_nt(wtr_ref[512:768, :], hb).astype(bf16)
    store_value_blocks(vTfox_ref, with_ones(_dot_nt(wtr_ref[768:1024, :], hb)))

    def emit_dilated(refs, slot, z):
        refs[0][...] = z.astype(bf16)
        for c in range(GROUP_W // 128):
            dscr_ref[slot + c] = z[:, c * 128:(c + 1) * 128]
        for ref, dil in zip(refs[1:], DILATIONS[1:]):
            for rho in range(dil):
                for c in range(GROUP_W // 128):
                    rows = dscr_ref[slot + c, pl.ds(rho, tm // dil, stride=dil), :]
                    ref[:, rho * GROUP_W + c * 128:rho * GROUP_W + (c + 1) * 128] = rows.astype(bf16)

    emit_dilated(qdil_refs, 0, rope_tok(tok(_C_QDIL, GROUP_W)))
    emit_dilated(kdil_refs, 2, rope_tok(tok(_C_KDIL, GROUP_W)))
    emit_dilated(vdil_refs, 4, tok(_C_VDIL, GROUP_W))

    xy = tok(_C_X, 256)
    xblk = xy[:, :128]
    yblk = xy[:, 128:]
    fg = xblk + bfg_ref[...]
    logf = jnp.minimum(fg, 0.0) - jnp.log1p(jnp.exp(-jnp.abs(fg)))
    lane = lax.broadcasted_iota(jnp.int32, logf.shape, 1)
    is_gate = (lane >= _FG_LANE) & (lane < _FG_LANE + N_HEADS)

    def three_pieces(v):
        v = jnp.where(is_gate, v, 0.0)
        p1 = v.astype(bf16).astype(jnp.float32)
        p2 = (v - p1).astype(bf16).astype(jnp.float32)
        p3 = (v - p1 - p2).astype(bf16).astype(jnp.float32)
        return p1 + pltpu.roll(p2, 8, 1) + pltpu.roll(p3, 16, 1)

    @pl.when(pl.program_id(0) % tiles_per_seq == 0)
    def _():
        carry_ref[...] = jnp.zeros_like(carry_ref)

    part = _dot(tri_ref[...], three_pieces(logf).astype(bf16))
    cum = part + pltpu.roll(part, 128 - 8, 1) + pltpu.roll(part, 128 - 16, 1) + carry_ref[...]
    carry_ref[...] = cum[tm - 1:tm, :]
    cbase_ref[0] = cum[0:1, :]

    dcols = three_pieces((cum[0:1, :] - cum) * LOG2E)
    kf = tok(_C_KFOX, GROUP_W)
    kfox_ref[...] = jnp.concatenate([kf[:, :128], dcols, kf[:, 128:], dcols], axis=1).astype(bf16)

    zcq = tok(_C_CQ, 256)
    cq = (zcq * _rms_scale(zcq, MLA_Q_RANK) * gq_ref[...]).astype(bf16)
    qcT = _dot_nt(wqT_ref[...], cq)
    c16, s16 = cT16_ref[...], sT16_ref[...]
    r1h, r2h = qcT[256:320], qcT[320:384]
    qT = jnp.concatenate([qcT[0:256], r1h * c16 - r2h * s16, r2h * c16 + r1h * s16], axis=0)
    qTmla_ref[...] = (qT * ((MLA_NOPE + MLA_ROPE) ** -0.5 * LOG2E)).astype(bf16)

    zckv = tok(_C_CKV, MLA_KV_RANK)
    ckv = (zckv * _rms_scale(zckv, MLA_KV_RANK) * gkv_ref[...]).astype(bf16)
    kn = _dot(ckv, wkn_ref[...])
    krope = xblk * ct16_ref[...] + yblk * st16_ref[...]
    kmla_ref[...] = jnp.concatenate([kn[:, :128], krope, kn[:, 128:], krope], axis=1).astype(bf16)
    store_value_blocks(vTmla_ref, with_ones(_dot_nt(wvT_ref[...], ckv)))


def _softmax_update(parts, vb, m_ref, acc_ref, h, offsets=None, maxes=None):
    if offsets is None:
        offsets = [None] * len(parts)
    if maxes is None:
        maxes = [jnp.max(s, axis=0, keepdims=True) for s in parts]
    m_old = m_ref[h:h + 1, :]
    blk_max = None
    for mx, off in zip(maxes, offsets):
        mx = mx if off is None else mx + off
        blk_max = mx if blk_max is None else jnp.maximum(blk_max, mx)
    m_new = jnp.maximum(m_old, blk_max)
    ps = []
    for s, off in zip(parts, offsets):
        shift = m_new if off is None else m_new - off
        ps.append(jnp.exp2(s - shift).astype(jnp.bfloat16))
    p = ps[0] if len(ps) == 1 else jnp.concatenate(ps, axis=0)
    acc_ref[h] = jnp.exp2(m_old - m_new) * acc_ref[h] + _dot(vb, p)
    m_ref[h:h + 1, :] = m_new


def _store_scores(buf, first_head, s):
    s_ref, mx_ref = buf
    tq = s_ref.shape[2]
    for k in range(s.shape[1] // tq):
        sh = s[:, k * tq:(k + 1) * tq]
        s_ref[first_head + k] = sh
        for n in range(sh.shape[0] // SLAB):
            mx_ref[first_head + k, n:n + 1, :] = jnp.max(sh[n * SLAB:(n + 1) * SLAB], axis=0, keepdims=True)


def _slabs(buf, h, tq):
    s_ref, mx_ref = buf
    n_slabs = s_ref.shape[1] // SLAB
    return ([s_ref[h, n * SLAB:(n + 1) * SLAB, :] for n in range(n_slabs)],
            [mx_ref[h, n:n + 1, :] for n in range(n_slabs)])


def _own_block_mask(i, tq, tk):
    key = lax.broadcasted_iota(jnp.int32, (tk, tq), 0)
    qry = lax.broadcasted_iota(jnp.int32, (tk, tq), 1) + (i % (tk // tq)) * tq
    return key <= qry


def _init_state(m_ref, acc_ref):
    m_ref[...] = jnp.full(m_ref.shape, -jnp.inf, jnp.float32)
    acc_ref[...] = jnp.zeros(acc_ref.shape, jnp.float32)


def _finalize(acc_ref, o_ref):
    outs = [acc_ref[h, 0:HEAD_DIM, :] / acc_ref[h, HEAD_DIM:HEAD_DIM + 1, :] for h in range(N_HEADS)]
    o_ref[...] = jnp.concatenate(outs, axis=0).T.astype(o_ref.dtype)


def _vblock(vT_ref, j, h):
    return vT_ref[j, h * V_ROWS:(h + 1) * V_ROWS, :]


def _run_key_blocks(own, last, buf_a, buf_b, scores_into, consume):
    def step(buf_next, j_next, buf_cur, j_cur, is_own):
        for h in range(N_HEADS):
            scores_into(buf_next, j_next, h)
            consume(buf_cur, j_cur, is_own, (h,))

    for h in range(N_HEADS):
        scores_into(buf_a, own, h)
    step(buf_b, 0, buf_a, own, True)

    def pair(p, carry):
        j0 = 2 * p
        step(buf_a, j0 + 1, buf_b, j0, False)
        step(buf_b, jnp.minimum(j0 + 2, last), buf_a, j0 + 1, False)
        return carry

    lax.fori_loop(0, own // 2, pair, 0)

    @pl.when(own % 2 == 1)
    def _():
        consume(buf_b, own - 1, False, range(N_HEADS))


def _key_rows(k_ref, j, tk):
    return k_ref[pl.ds(pl.multiple_of(j * tk, tk), tk), :]


def _causal_slabs(parts, causal):
    return [jnp.where(causal[n * SLAB:(n + 1) * SLAB], s, -jnp.inf) for n, s in enumerate(parts)]


def _two_tile_scores(buf, k_ref, qcat_ref, j, h, tk, tq):
    kb = k_ref[pl.ds(pl.multiple_of(j * tk, tk), tk), 256 * (h // 2):256 * (h // 2 + 1)]
    _store_scores(buf, h, _dot(kb, qcat_ref[:, h * tq:(h + 1) * tq]))


def _fox_kernel(qT_ref, k_ref, vT_ref, cbase_ref, o_ref, qcat_ref, m_ref, acc_ref, sa_ref, sb_ref, mxa_ref, mxb_ref):
    i = pl.program_id(1)
    tq = qT_ref.shape[1]
    zeros64 = jnp.zeros((HEAD_DIM, tq), jnp.bfloat16)
    row = lax.broadcasted_iota(jnp.int32, (128, tq), 0)
    for h in range(N_HEADS):
        qh = qT_ref[h * HEAD_DIM:(h + 1) * HEAD_DIM, :]
        pick = (row == _FG_LANE + h) | (row == _FG_LANE + 8 + h) | (row == _FG_LANE + 16 + h)
        rows = ([qh, zeros64] if h % 2 == 0 else [zeros64, qh]) + [pick.astype(jnp.bfloat16)]
        qcat_ref[:, h * tq:(h + 1) * tq] = jnp.concatenate(rows, axis=0)
    _init_state(m_ref, acc_ref)
    tk = sa_ref.shape[1]
    assert tq == tk
    causal = _own_block_mask(i, tq, tk)

    def scores_into(buf, j, h):
        _two_tile_scores(buf, k_ref, qcat_ref, j, h, tk, tq)

    def consume(buf, j, own, heads):
        offs = (cbase_ref[i] - cbase_ref[j]) * LOG2E
        for h in heads:
            parts, maxes = _slabs(buf, h, tq)
            if own:
                parts, maxes = _causal_slabs(parts, causal), None
            off = offs[:, _FG_LANE + h:_FG_LANE + h + 1]
            _softmax_update(parts, _vblock(vT_ref, j, h), m_ref, acc_ref, h, [off] * len(parts), maxes)

    _run_key_blocks(i, k_ref.shape[0] // tk - 1, (sa_ref, mxa_ref), (sb_ref, mxb_ref), scores_into, consume)
    _finalize(acc_ref, o_ref)


def _mla_kernel(qT_ref, k_ref, vT_ref, o_ref, qcat_ref, m_ref, acc_ref, sa_ref, sb_ref, mxa_ref, mxb_ref):
    i = pl.program_id(1)
    tq = qT_ref.shape[1]
    zeros64 = jnp.zeros((MLA_NOPE, tq), jnp.bfloat16)
    zeros96 = jnp.zeros((256 - 128 - MLA_ROPE, tq), jnp.bfloat16)
    for h in range(N_HEADS):
        qn = qT_ref[h * MLA_NOPE:(h + 1) * MLA_NOPE, :]
        qr1 = qT_ref[256 + 16 * h:256 + 16 * (h + 1), :]
        qr2 = qT_ref[320 + 16 * h:320 + 16 * (h + 1), :]
        nope = [qn, zeros64] if h % 2 == 0 else [zeros64, qn]
        qcat_ref[:, h * tq:(h + 1) * tq] = jnp.concatenate(nope + [qr1, qr2, zeros96], axis=0)
    _init_state(m_ref, acc_ref)
    tk = sa_ref.shape[1]
    causal = _own_block_mask(i, tq, tk)

    def scores_into(buf, j, h):
        _two_tile_scores(buf, k_ref, qcat_ref, j, h, tk, tq)

    def consume(buf, j, own, heads):
        for h in heads:
            parts, maxes = _slabs(buf, h, tq)
            if own:
                parts, maxes = _causal_slabs(parts, causal), None
            _softmax_update(parts, _vblock(vT_ref, j, h), m_ref, acc_ref, h, None, maxes)

    _run_key_blocks(i * tq // tk, k_ref.shape[0] // tk - 1, (sa_ref, mxa_ref), (sb_ref, mxb_ref), scores_into,
                    consume)
    _finalize(acc_ref, o_ref)


def _moba_kernel(qT_ref, k_ref, vT_ref, km_ref, o_ref, qcat_ref, m_ref, acc_ref, sa_ref, sb_ref, mxa_ref, mxb_ref,
                 bias_ref):
    i = pl.program_id(1)
    tq = qT_ref.shape[1]
    nb = km_ref.shape[1]
    q = qT_ref[...]
    row = lax.broadcasted_iota(jnp.int32, q.shape, 0)
    km = km_ref[0].astype(jnp.bfloat16)
    blk = lax.broadcasted_iota(jnp.int32, (nb, tq), 0)
    qblk = i * (tq // MOBA_BLOCK) + lax.broadcasted_iota(jnp.int32, (1, tq), 1) // MOBA_BLOCK
    neg_inf = jnp.float32(-jnp.inf)
    for h in range(N_HEADS):
        in_head = (((row >= 32 * h) & (row < 32 * (h + 1)))
                   | ((row >= 128 + 32 * h) & (row < 128 + 32 * (h + 1))))
        qm = jnp.where(in_head, q, jnp.zeros_like(q))
        qcat_ref[:, h * tq:(h + 1) * tq] = qm
        g = jnp.where(blk < qblk, _dot(km, qm), neg_inf)
        sel = jnp.zeros((nb, tq), jnp.bool_)
        for _ in range(MOBA_TOPK):
            mx = jnp.max(g, axis=0, keepdims=True)
            cand = jnp.where((g == mx) & (mx > neg_inf), blk, nb)
            chosen = blk == jnp.min(cand, axis=0, keepdims=True)
            sel = sel | chosen
            g = jnp.where(chosen, neg_inf, g)
        bias_ref[h * nb:(h + 1) * nb, :] = jnp.where(sel, 0.0, MASKED_LOGIT)
    _init_state(m_ref, acc_ref)
    tk = sa_ref.shape[1]
    per = tk // SLAB
    causal = _own_block_mask(i, tq, tk)

    def scores_into(buf, j, h):
        _store_scores(buf, h, _dot(_key_rows(k_ref, j, tk), qcat_ref[:, h * tq:(h + 1) * tq]))

    def consume(buf, j, own, heads):
        for h in heads:
            parts, maxes = _slabs(buf, h, tq)
            gates = [bias_ref[pl.ds(h * nb + per * j + n, 1), :] for n in range(per)]
            if own:
                gates = [jnp.where(per * j + n < qblk, g, 0.0) for n, g in enumerate(gates)]
                parts, maxes = _causal_slabs(parts, causal), None
            _softmax_update(parts, _vblock(vT_ref, j, h), m_ref, acc_ref, h, gates, maxes)

    _run_key_blocks(i * tq // tk, k_ref.shape[0] // tk - 1, (sa_ref, mxa_ref), (sb_ref, mxb_ref), scores_into,
                    consume)
    _finalize(acc_ref, o_ref)


def _dilated_kernel(q_ref, k_ref, v_ref, o_ref, lse_ref, *, window):
    i = pl.program_id(2)
    tq = q_ref.shape[1]
    n = k_ref.shape[1]
    q = q_ref[0]
    a = i * tq
    ks = jnp.clip(a - DIL_WINDOW_STEPS, 0, n - window)
    ks = pl.multiple_of(ks, DIL_WINDOW_STEPS)
    kw = k_ref[0, pl.ds(ks, window), :]
    vw = v_ref[0, pl.ds(ks, window), :]
    jq = a + lax.broadcasted_iota(jnp.int32, (tq, window), 0)
    jk = ks + lax.broadcasted_iota(jnp.int32, (tq, window), 1)
    band = (jq - jk >= 0) & (jq - jk <= DIL_WINDOW_STEPS)
    lane = lax.broadcasted_iota(jnp.int32, (tq, GROUP_W), 1)
    o = jnp.zeros((tq, GROUP_W), jnp.float32)
    lse = jnp.zeros((tq, GROUP_W), jnp.float32)
    for h in range(N_HEADS):
        in_head = (((lane >= 32 * h) & (lane < 32 * (h + 1)))
                   | ((lane >= 128 + 32 * h) & (lane < 128 + 32 * (h + 1))))
        qm = jnp.where(in_head, q, jnp.zeros_like(q))
        s = jnp.where(band, _dot_nt(qm, kw), -jnp.inf)
        m = jnp.max(s, axis=1, keepdims=True)
        p = jnp.exp2(s - m)
        l = jnp.sum(p, axis=1, keepdims=True)
        oh = _dot(p.astype(jnp.bfloat16), vw) / l
        out_lanes = (lane >= h * HEAD_DIM) & (lane < (h + 1) * HEAD_DIM)
        o = jnp.where(out_lanes, oh, o)
        lse = jnp.where(out_lanes, m + jnp.log2(l), lse)
    o_ref[0] = o
    lse_ref[0] = lse


def _out_ffn_kernel(x_ref, omoba_ref, ofox_ref, omla_ref, od1_ref, od4_ref, od16_ref, l1_ref, l4_ref, l16_ref,
                    wout_ref, gpost_ref, gpre_ref, wg_ref, wu_ref, wd_ref, gpf_ref, out_ref, acc_ref,
                    dscr_ref):
    bf16 = jnp.bfloat16
    tm = x_ref.shape[0]

    def token_order(ref, slot, dil):
        halves = GROUP_W // 128
        for rho in range(dil):
            for c in range(halves):
                lanes = slice(rho * GROUP_W + c * 128, rho * GROUP_W + (c + 1) * 128)
                dscr_ref[slot + c, pl.ds(rho, tm // dil, stride=dil), :] = ref[:, lanes]
        return jnp.concatenate([dscr_ref[slot + c] for c in range(halves)], axis=1)

    l1, od1 = l1_ref[...], od1_ref[...]
    l4, od4 = token_order(l4_ref, 0, 4), token_order(od4_ref, 2, 4)
    l16, od16 = token_order(l16_ref, 4, 16), token_order(od16_ref, 6, 16)
    m = jnp.maximum(jnp.maximum(l1, l4), l16)
    e1, e4, e16 = jnp.exp2(l1 - m), jnp.exp2(l4 - m), jnp.exp2(l16 - m)
    odil = (e1 * od1 + e4 * od4 + e16 * od16) / (e1 + e4 + e16)
    y = (_dot(omoba_ref[...], wout_ref[0]) + _dot(ofox_ref[...], wout_ref[1])
         + _dot(omla_ref[...], wout_ref[2]) + _dot(odil.astype(bf16), wout_ref[3]))
    x1 = x_ref[...] + y * _rms_scale(y, D_MODEL) * gpost_ref[...]
    hb = (x1 * _rms_scale(x1, D_MODEL) * gpre_ref[...]).astype(bf16)
    acc_ref[...] = jnp.zeros_like(acc_ref)
    for c in range(D_FF // FF_CHUNK):
        cols = slice(c * FF_CHUNK, (c + 1) * FF_CHUNK)
        g = _dot(hb, wg_ref[:, cols])
        u = _dot(hb, wu_ref[:, cols])
        f = (g * jax.nn.sigmoid(g) * u).astype(bf16)
        acc_ref[...] += _dot(f, wd_ref[cols, :])
    f = acc_ref[...]
    out_ref[...] = x1 + f * _rms_scale(f, D_MODEL) * gpf_ref[...]


def _rope_tables(positions):
    t = positions.reshape(-1).astype(jnp.float32)

    def tab(dim):
        inv = ROPE_THETA ** (-jnp.arange(0, dim, 2, dtype=jnp.float32) / dim)
        ang = t[:, None] * inv
        return jnp.cos(ang), jnp.sin(ang)

    c64, s64 = tab(HEAD_DIM)
    c16, s16 = tab(MLA_ROPE)
    ct64, st64 = jnp.tile(c64, (1, 4)), jnp.tile(s64, (1, 4))
    pad = jnp.zeros((t.shape[0], 128 - MLA_ROPE), jnp.float32)
    ct16 = jnp.concatenate([c16, c16, pad], axis=1)
    st16 = jnp.concatenate([s16, s16, pad], axis=1)
    cT16, sT16 = jnp.tile(c16, (1, 4)).T, jnp.tile(s16, (1, 4)).T
    return ct64, st64, ct64.T, st64.T, ct16, st16, cT16, sT16


_HALF_PERM = np.array([h * 64 + half * 32 + j for half in (0, 1) for h in range(4) for j in range(32)])
_QROPE_ROWS = np.array([h * 96 + 64 + half * 16 + j for half in (0, 1) for h in range(4) for j in range(16)])
_QNOPE_ROWS = np.array([h * 96 + j for h in range(4) for j in range(64)])
_KNOPE_COLS = np.array([h * 128 + j for h in range(4) for j in range(64)])
_VMLA_COLS = np.array([h * 128 + 64 + j for h in range(4) for j in range(64)])


def _prep_layer_weights(w_in, b_forget, g_mla_q, w_mla_q_up, g_mla_kv, w_mla_kv_up, w_out, w_gate, w_up, w_down):
    bf16 = jnp.bfloat16
    depth = w_in.shape[0]
    sl = lambda a, b: w_in[:, :, a:b]
    scale = HEAD_DIM ** -0.5 * LOG2E
    moba_q, moba_k, moba_v = sl(0, 256)[..., _HALF_PERM] * scale, sl(256, 512)[..., _HALF_PERM], sl(512, 768)
    fox_q, fox_k, fox_v = sl(768, 1024) * scale, sl(1024, 1280), sl(1280, 1536)
    fg, cq, ckv, kr = sl(1536, 1540), sl(1540, 1732), sl(1732, 1860), sl(1860, 1892)
    dil_q, dil_k, dil_v = sl(1892, 2148)[..., _HALF_PERM] * scale, sl(2148, 2404)[..., _HALF_PERM], sl(2404, 2660)
    kr_rot = jnp.concatenate([-kr[..., 16:], kr[..., :16]], axis=-1)
    z = lambda n: jnp.zeros((depth, D_MODEL, n), w_in.dtype)
    wtok = jnp.concatenate([moba_k, fox_k, dil_q, dil_k, dil_v, cq, z(256 - MLA_Q_RANK), ckv,
                            kr, fg, z(128 - 36), kr_rot, z(96)], axis=-1).astype(bf16)
    wtr = jnp.swapaxes(jnp.concatenate([moba_q, moba_v, fox_q, fox_v], axis=-1), 1, 2).astype(bf16)
    bfg = jnp.zeros((depth, 1, 128), jnp.float32).at[:, 0, _FG_LANE:_FG_LANE + N_HEADS].set(b_forget)
    gq = jnp.pad(g_mla_q, ((0, 0), (0, 256 - MLA_Q_RANK)))[:, None, :]
    wq_rows = jnp.swapaxes(w_mla_q_up, 1, 2)
    wqT = jnp.concatenate([wq_rows[:, _QNOPE_ROWS], wq_rows[:, _QROPE_ROWS]], axis=1)
    wqT = jnp.pad(wqT, ((0, 0), (0, 0), (0, 256 - MLA_Q_RANK))).astype(bf16)
    gkv = g_mla_kv[:, None, :]
    wkn = w_mla_kv_up[:, :, _KNOPE_COLS].astype(bf16)
    wvT = jnp.swapaxes(w_mla_kv_up[:, :, _VMLA_COLS], 1, 2).astype(bf16)
    wout = w_out.reshape(depth, 4, GROUP_W, D_MODEL).astype(bf16)
    wg, wu, wd = w_gate.astype(bf16), w_up.astype(bf16), w_down.astype(bf16)
    return wtok, wtr, bfg, gq, wqT, gkv, wkn, wvT, wout, wg, wu, wd


def _const_spec(shape):
    return pl.BlockSpec(shape, lambda *_: (0,) * len(shape))


def _params(*sem):
    return pltpu.CompilerParams(dimension_semantics=sem, vmem_limit_bytes=VMEM_LIMIT)


def _in_proj(x2, g, wtok, wtr, tri, tables, bfg, gq, wqT, gkv, wkn, wvT, seq):
    t = x2.shape[0]
    tm = ROW_TILE
    nt = t // tm
    bf16, f32 = jnp.bfloat16, jnp.float32
    ct64, st64, cT64, sT64, ct16, st16, cT16, sT16 = tables
    tok_spec = lambda w: pl.BlockSpec((tm, w), lambda i: (i, 0))
    tr_spec = lambda r: pl.BlockSpec((r, tm), lambda i: (0, i))
    blk3 = lambda n, r, c: pl.BlockSpec((n, r, c), lambda i: (i, 0, 0))
    mb = tm // MOBA_BLOCK
    ab = tm // KEY_BLOCK
    vrows = N_HEADS * V_ROWS
    vt_shape = jax.ShapeDtypeStruct((t // KEY_BLOCK, vrows, KEY_BLOCK), bf16)
    in_specs = [tok_spec(D_MODEL), _const_spec((1, D_MODEL)), _const_spec(wtok.shape), _const_spec(wtr.shape),
                _const_spec(tri.shape), tok_spec(128), tok_spec(128), tr_spec(128), tr_spec(128),
                tok_spec(128), tok_spec(128), tr_spec(64), tr_spec(64), _const_spec((1, 128)),
                _const_spec((1, 256)), _const_spec(wqT.shape), _const_spec((1, 128)), _const_spec(wkn.shape),
                _const_spec(wvT.shape)]
    out_shape = [
        jax.ShapeDtypeStruct((t, GROUP_W), bf16),
        jax.ShapeDtypeStruct((nt, mb, GROUP_W), f32),
        jax.ShapeDtypeStruct((GROUP_W, t), bf16),
        vt_shape,
        jax.ShapeDtypeStruct((t, 512), bf16),
        jax.ShapeDtypeStruct((GROUP_W, t), bf16),
        vt_shape,
        jax.ShapeDtypeStruct((nt, 1, 128), f32),
        *[jax.ShapeDtypeStruct((t // d, d * GROUP_W), bf16) for d in DILATIONS] * 3,
        jax.ShapeDtypeStruct((t, 512), bf16),
        jax.ShapeDtypeStruct((384, t), bf16),
        vt_shape,
    ]
    vt_spec = blk3(ab, vrows, KEY_BLOCK)
    out_specs = [tok_spec(GROUP_W), blk3(1, mb, GROUP_W), tr_spec(GROUP_W), vt_spec,
                 tok_spec(512), tr_spec(GROUP_W), vt_spec, blk3(1, 1, 128),
                 *[pl.BlockSpec((tm // d, d * GROUP_W), lambda i: (i, 0)) for d in DILATIONS] * 3,
                 tok_spec(512), tr_spec(384), vt_spec]
    return pl.pallas_call(
        functools.partial(_in_proj_kernel, tiles_per_seq=seq // tm),
        grid=(nt,), in_specs=in_specs, out_specs=out_specs, out_shape=out_shape,
        scratch_shapes=[pltpu.VMEM((1, 128), f32), pltpu.VMEM((3 * GROUP_W // 128, tm, 128), f32)],
        compiler_params=_params("arbitrary"), name="in_proj",
    )(x2, g, wtok, wtr, tri, ct64, st64, cT64, sT64, ct16, st16, cT16, sT16, bfg, gq, wqT, gkv, wkn, wvT)


def _attention_scratch(tq, tk=KEY_BLOCK):
    return [pltpu.VMEM((GROUP_W, N_HEADS * tq), jnp.bfloat16),
            pltpu.VMEM((8, tq), jnp.float32),
            pltpu.VMEM((N_HEADS, V_ROWS, tq), jnp.float32),
            pltpu.VMEM((N_HEADS, tk, tq), jnp.float32),
            pltpu.VMEM((N_HEADS, tk, tq), jnp.float32),
            pltpu.VMEM((N_HEADS, 8, tq), jnp.float32),
            pltpu.VMEM((N_HEADS, 8, tq), jnp.float32)]


def _dense_attention(body, name, qT, k, vT, extra, batch, seq):
    tq = ATT_TILE
    nq = seq // tq
    t = batch * seq
    in_specs = [pl.BlockSpec((qT.shape[0], tq), lambda b, i: (0, b * nq + i)),
                pl.BlockSpec((seq, k.shape[1]), lambda b, i: (b, 0)),
                pl.BlockSpec((seq // KEY_BLOCK, N_HEADS * V_ROWS, KEY_BLOCK), lambda b, i: (b, 0, 0))]
    in_specs += [pl.BlockSpec((seq // KEY_BLOCK, 1, 128), lambda b, i: (b, 0, 0)) for _ in extra]
    return pl.pallas_call(
        body, grid=(batch, nq), in_specs=in_specs,
        out_specs=pl.BlockSpec((tq, GROUP_W), lambda b, i: (b * nq + i, 0)),
        out_shape=jax.ShapeDtypeStruct((t, GROUP_W), jnp.bfloat16),
        scratch_shapes=_attention_scratch(tq),
        compiler_params=_params("arbitrary", "arbitrary"), name=name,
    )(qT, k, vT, *extra)


def _moba_attention(qT, k, vT, kmean, batch, seq):
    tq = ATT_TILE
    nq = seq // tq
    nb = seq // MOBA_BLOCK
    t = batch * seq
    return pl.pallas_call(
        _moba_kernel, grid=(batch, nq),
        in_specs=[pl.BlockSpec((GROUP_W, tq), lambda b, i: (0, b * nq + i)),
                  pl.BlockSpec((seq, GROUP_W), lambda b, i: (b, 0)),
                  pl.BlockSpec((seq // KEY_BLOCK, N_HEADS * V_ROWS, KEY_BLOCK), lambda b, i: (b, 0, 0)),
                  pl.BlockSpec((1, nb, GROUP_W), lambda b, i: (b, 0, 0))],
        out_specs=pl.BlockSpec((tq, GROUP_W), lambda b, i: (b * nq + i, 0)),
        out_shape=jax.ShapeDtypeStruct((t, GROUP_W), jnp.bfloat16),
        scratch_shapes=_attention_scratch(tq) + [pltpu.VMEM((N_HEADS * nb, tq), jnp.float32)],
        compiler_params=_params("arbitrary", "arbitrary"), name="moba_attention",
    )(qT, k, vT, kmean)


def _dilated_attention(q, k, v, batch, seq, dil):
    n = seq // dil
    tq = n if n <= 2 * DIL_TILE else DIL_TILE
    window = min(tq + DIL_WINDOW_STEPS, n)
    view = lambda a: a.reshape(batch, n, dil * GROUP_W)
    qspec = pl.BlockSpec((1, tq, GROUP_W), lambda b, r, i: (b, i, r))
    kspec = pl.BlockSpec((1, n, GROUP_W), lambda b, r, i: (b, 0, r))
    shape = jax.ShapeDtypeStruct((batch, n, dil * GROUP_W), jnp.float32)
    o, lse = pl.pallas_call(
        functools.partial(_dilated_kernel, window=window), grid=(batch, dil, n // tq),
        in_specs=[qspec, kspec, kspec], out_specs=[qspec, qspec], out_shape=[shape, shape],
        compiler_params=_params("arbitrary", "arbitrary", "arbitrary"), name=f"dilated_attention_d{dil}",
    )(view(q), view(k), view(v))
    return o.reshape(batch * n, dil * GROUP_W), lse.reshape(batch * n, dil * GROUP_W)


def _out_ffn(x2, omoba, ofox, omla, dil_outs, wout, gpost, gpre, wg, wu, wd, gpf):
    t = x2.shape[0]
    tm = ROW_TILE
    row = lambda w: pl.BlockSpec((tm, w), lambda i: (i, 0))
    (od1, l1), (od4, l4), (od16, l16) = dil_outs
    dil_specs = [pl.BlockSpec((tm // d, d * GROUP_W), lambda i: (i, 0)) for d in DILATIONS]
    in_specs = ([row(D_MODEL)] + [row(GROUP_W)] * 3 + dil_specs * 2
                + [_const_spec(wout.shape), _const_spec((1, D_MODEL)), _const_spec((1, D_MODEL)),
                   _const_spec(wg.shape), _const_spec(wu.shape), _const_spec(wd.shape), _const_spec((1, D_MODEL))])
    return pl.pallas_call(
        _out_ffn_kernel, grid=(t // tm,), in_specs=in_specs, out_specs=row(D_MODEL),
        out_shape=jax.ShapeDtypeStruct((t, D_MODEL), jnp.float32),
        scratch_shapes=[pltpu.VMEM((tm, D_MODEL), jnp.float32), pltpu.VMEM((4 * GROUP_W // 128, tm, 128), jnp.float32)],
        compiler_params=_params("arbitrary"), name="out_ffn",
    )(x2, omoba, ofox, omla, od1, od4, od16, l1, l4, l16, wout, gpost, gpre, wg, wu, wd, gpf)


def kernel(x, positions, w_in, b_forget, g_mla_q, w_mla_q_up, g_mla_kv, w_mla_kv_up, w_out, g_pre_mix, g_post_mix, w_gate, w_up, w_down, g_pre_ffn, g_post_ffn):
    batch, seq, _ = x.shape
    depth = w_in.shape[0]
    assert seq % ROW_TILE == 0 and seq % (DILATIONS[-1] * DIL_WINDOW_STEPS) == 0
    assert ROW_TILE == KEY_BLOCK
    tables = _rope_tables(positions)
    wtok, wtr, bfg, gq, wqT, gkv, wkn, wvT, wout, wg, wu, wd = _prep_layer_weights(
        w_in, b_forget, g_mla_q, w_mla_q_up, g_mla_kv, w_mla_kv_up, w_out, w_gate, w_up, w_down)
    tri = jnp.tril(jnp.ones((ROW_TILE, ROW_TILE), jnp.bfloat16))
    x2 = x.reshape(batch * seq, D_MODEL)
    for l in range(depth):
        (k_moba, kmean, qT_moba, vT_moba, k_fox, qT_fox, vT_fox, cbase, *dil_qkv,
         k_mla, qT_mla, vT_mla) = _in_proj(x2, g_pre_mix[l][None], wtok[l], wtr[l], tri, tables, bfg[l], gq[l],
                                          wqT[l], gkv[l], wkn[l], wvT[l], seq)
        nd = len(DILATIONS)
        o_moba = _moba_attention(qT_moba, k_moba, vT_moba, kmean.reshape(batch, seq // MOBA_BLOCK, GROUP_W),
                                 batch, seq)
        o_fox = _dense_attention(_fox_kernel, "fox_attention", qT_fox, k_fox, vT_fox, [cbase], batch, seq)
        o_mla = _dense_attention(_mla_kernel, "mla_attention", qT_mla, k_mla, vT_mla, [], batch, seq)
        dil_outs = [_dilated_attention(dil_qkv[n], dil_qkv[nd + n], dil_qkv[2 * nd + n], batch, seq, d)
                    for n, d in enumerate(DILATIONS)]
        x2 = _out_ffn(x2, o_moba, o_fox, o_mla, dil_outs, wout[l], g_post_mix[l][None], g_pre_ffn[l][None],
                      wg[l], wu[l], wd[l], g_post_ffn[l][None])
    return x2.reshape(batch, seq, D_MODEL)
```

```python
import functools

import numpy as np
import jax
import jax.numpy as jnp
from jax import lax
from jax.experimental import pallas as pl
from jax.experimental.pallas import tpu as pltpu

D_MODEL = 1024
HEAD_DIM = 64
N_HEADS = 4
GROUP_W = N_HEADS * HEAD_DIM
MOBA_BLOCK = 256
MOBA_TOPK = 3
MLA_Q_RANK = 192
MLA_KV_RANK = 128
MLA_NOPE = 64
MLA_ROPE = 32
DIL_WINDOW_STEPS = 128
DILATIONS = (1, 4, 16)
ROPE_THETA = 10000.0
RMS_EPS = 1e-6
D_FF = 2816
FF_CHUNK = 256
V_ROWS = 80
LOG2E = 1.4426950408889634
MASKED_LOGIT = -1e30

ROW_TILE = 512
ATT_TILE = 512
KEY_BLOCK = 512
SLAB = MOBA_BLOCK
DIL_TILE = 256
DIL_PROBLEMS = 4
VMEM_LIMIT = 56 * 1024 * 1024

_C_KMOBA, _C_KFOX, _C_QDIL, _C_KDIL, _C_VDIL, _C_CQ = 0, 256, 512, 768, 1024, 1280
_C_CKV, _C_X, _C_Y, _N_TOK = 1536, 1664, 1792, 1920
_FG_LANE = 32

_NT = (((1,), (1,)), ((), ()))


def _dot(a, b):
    return jnp.dot(a, b, preferred_element_type=jnp.float32)


def _dot_nt(a, b):
    return lax.dot_general(a, b, _NT, preferred_element_type=jnp.float32)


def _rms_scale(v, n):
    return lax.rsqrt(jnp.sum(v * v, axis=-1, keepdims=True) * (1.0 / n) + RMS_EPS)


def _in_proj_kernel(x_ref, g_ref, wtok_ref, wtr_ref, tri_ref, ct64_ref, st64_ref, cT64_ref, sT64_ref,
                    ct16_ref, st16_ref, cT16_ref, sT16_ref, bfg_ref, gq_ref, wqT_ref, gkv_ref, wkn_ref, wvT_ref,
                    kmoba_ref, kmean_ref, qTmoba_ref, vTmoba_ref, kfox_ref, qTfox_ref, vTfox_ref, cbase_ref,
                    q1_ref, q4_ref, q16_ref, k1_ref, k4_ref, k16_ref, v1_ref, v4_ref, v16_ref,
                    kmla_ref, qTmla_ref, vTmla_ref, carry_ref, dscr_ref, *, tiles_per_seq):
    bf16 = jnp.bfloat16
    qdil_refs, kdil_refs, vdil_refs = (q1_ref, q4_ref, q16_ref), (k1_ref, k4_ref, k16_ref), (v1_ref, v4_ref, v16_ref)
    x = x_ref[...]
    hb = (x * _rms_scale(x, D_MODEL) * g_ref[...]).astype(bf16)
    tm = x.shape[0]

    def tok(c0, w):
        return _dot(hb, wtok_ref[:, c0:c0 + w])

    def rope_tok(z):
        c, s = ct64_ref[...], st64_ref[...]
        x1, x2 = z[:, :128], z[:, 128:]
        return jnp.concatenate([x1 * c - x2 * s, x2 * c + x1 * s], axis=1)

    def with_ones(vT):
        ones = jnp.ones((V_ROWS - HEAD_DIM, tm), bf16)
        parts = []
        for h in range(N_HEADS):
            parts += [vT[h * HEAD_DIM:(h + 1) * HEAD_DIM].astype(bf16), ones]
        return jnp.concatenate(parts, axis=0)

    def rope_tr(zT):
        c, s = cT64_ref[...], sT64_ref[...]
        x1, x2 = zT[:128], zT[128:]
        return jnp.concatenate([x1 * c - x2 * s, x2 * c + x1 * s], axis=0)

    k_moba = rope_tok(tok(_C_KMOBA, GROUP_W))
    kmoba_ref[...] = k_moba.astype(bf16)
    for blk in range(tm // MOBA_BLOCK):
        kmean_ref[0, blk:blk + 1, :] = jnp.mean(k_moba[blk * MOBA_BLOCK:(blk + 1) * MOBA_BLOCK], axis=0, keepdims=True)
    qTmoba_ref[...] = rope_tr(_dot_nt(wtr_ref[0:256, :], hb)).astype(bf16)
    def store_value_blocks(ref, vT):
        for blk in range(tm // KEY_BLOCK):
            ref[blk] = vT[:, blk * KEY_BLOCK:(blk + 1) * KEY_BLOCK]

    store_value_blocks(vTmoba_ref, with_ones(_dot_nt(wtr_ref[256:512, :], hb)))

    qTfox_ref[...] = _dot_nt(wtr_ref[512:768, :], hb).astype(bf16)
    store_value_blocks(vTfox_ref, with_ones(_dot_nt(wtr_ref[768:1024, :], hb)))

    def emit_dilated(refs, slot, z):
        refs[0][...] = z.astype(bf16)
        for c in range(GROUP_W // 128):
            dscr_ref[slot + c] = z[:, c * 128:(c + 1) * 128]
        for ref, dil in zip(refs[1:], DILATIONS[1:]):
            for rho in range(dil):
                for c in range(GROUP_W // 128):
                    rows = dscr_ref[slot + c, pl.ds(rho, tm // dil, stride=dil), :]
                    ref[:, rho * GROUP_W + c * 128:rho * GROUP_W + (c + 1) * 128] = rows.astype(bf16)

    emit_dilated(qdil_refs, 0, rope_tok(tok(_C_QDIL, GROUP_W)))
    emit_dilated(kdil_refs, 2, rope_tok(tok(_C_KDIL, GROUP_W)))
    emit_dilated(vdil_refs, 4, tok(_C_VDIL, GROUP_W))

    xy = tok(_C_X, 256)
    xblk = xy[:, :128]
    yblk = xy[:, 128:]
    fg = xblk + bfg_ref[...]
    logf = jnp.minimum(fg, 0.0) - jnp.log1p(jnp.exp(-jnp.abs(fg)))
    lane = lax.broadcasted_iota(jnp.int32, logf.shape, 1)
    is_gate = (lane >= _FG_LANE) & (lane < _FG_LANE + N_HEADS)

    def three_pieces(v):
        v = jnp.where(is_gate, v, 0.0)
        p1 = v.astype(bf16).astype(jnp.float32)
        p2 = (v - p1).astype(bf16).astype(jnp.float32)
        p3 = (v - p1 - p2).astype(bf16).astype(jnp.float32)
        return p1 + pltpu.roll(p2, 8, 1) + pltpu.roll(p3, 16, 1)

    @pl.when(pl.program_id(0) % tiles_per_seq == 0)
    def _():
        carry_ref[...] = jnp.zeros_like(carry_ref)

    part = _dot(tri_ref[...], three_pieces(logf).astype(bf16))
    cum = part + pltpu.roll(part, 128 - 8, 1) + pltpu.roll(part, 128 - 16, 1) + carry_ref[...]
    carry_ref[...] = cum[tm - 1:tm, :]
    cbase_ref[0] = cum[0:1, :]

    dcols = three_pieces((cum[0:1, :] - cum) * LOG2E)
    kf = tok(_C_KFOX, GROUP_W)
    kfox_ref[...] = jnp.concatenate([kf[:, :128], dcols, kf[:, 128:], dcols], axis=1).astype(bf16)

    zcq = tok(_C_CQ, 256)
    cq = (zcq * _rms_scale(zcq, MLA_Q_RANK) * gq_ref[...]).astype(bf16)
    qcT = _dot_nt(wqT_ref[...], cq)
    c16, s16 = cT16_ref[...], sT16_ref[...]
    r1h, r2h = qcT[256:320], qcT[320:384]
    qT = jnp.concatenate([qcT[0:256], r1h * c16 - r2h * s16, r2h * c16 + r1h * s16], axis=0)
    qTmla_ref[...] = (qT * ((MLA_NOPE + MLA_ROPE) ** -0.5 * LOG2E)).astype(bf16)

    zckv = tok(_C_CKV, MLA_KV_RANK)
    ckv = (zckv * _rms_scale(zckv, MLA_KV_RANK) * gkv_ref[...]).astype(bf16)
    kn = _dot(ckv, wkn_ref[...])
    krope = xblk * ct16_ref[...] + yblk * st16_ref[...]
    kmla_ref[...] = jnp.concatenate([kn[:, :128], krope, kn[:, 128:], krope], axis=1).astype(bf16)
    store_value_blocks(vTmla_ref, with_ones(_dot_nt(wvT_ref[...], ckv)))


def _softmax_update(parts, vb, m_ref, acc_ref, h, offsets=None, maxes=None):
    if offsets is None:
        offsets = [None] * len(parts)
    if maxes is None:
        maxes = [jnp.max(s, axis=0, keepdims=True) for s in parts]
    m_old = m_ref[h:h + 1, :]
    blk_max = None
    for mx, off in zip(maxes, offsets):
        mx = mx if off is None else mx + off
        blk_max = mx if blk_max is None else jnp.maximum(blk_max, mx)
    m_new = jnp.maximum(m_old, blk_max)
    ps = []
    for s, off in zip(parts, offsets):
        shift = m_new if off is None else m_new - off
        ps.append(jnp.exp2(s - shift).astype(jnp.bfloat16))
    p = ps[0] if len(ps) == 1 else jnp.concatenate(ps, axis=0)
    acc_ref[h] = jnp.exp2(m_old - m_new) * acc_ref[h] + _dot(vb, p)
    m_ref[h:h + 1, :] = m_new


def _store_scores(buf, first_head, s):
    s_ref, mx_ref = buf
    tq = s_ref.shape[2]
    for k in range(s.shape[1] // tq):
        sh = s[:, k * tq:(k + 1) * tq]
        s_ref[first_head + k] = sh
        for n in range(sh.shape[0] // SLAB):
            mx_ref[first_head + k, n:n + 1, :] = jnp.max(sh[n * SLAB:(n + 1) * SLAB], axis=0, keepdims=True)


def _slabs(buf, h, tq):
    s_ref, mx_ref = buf
    n_slabs = s_ref.shape[1] // SLAB
    return ([s_ref[h, n * SLAB:(n + 1) * SLAB, :] for n in range(n_slabs)],
            [mx_ref[h, n:n + 1, :] for n in range(n_slabs)])


def _own_block_mask(i, tq, tk):
    key = lax.broadcasted_iota(jnp.int32, (tk, tq), 0)
    qry = lax.broadcasted_iota(jnp.int32, (tk, tq), 1) + (i % (tk // tq)) * tq
    return key <= qry


def _init_state(m_ref, acc_ref):
    m_ref[...] = jnp.full(m_ref.shape, -jnp.inf, jnp.float32)
    acc_ref[...] = jnp.zeros(acc_ref.shape, jnp.float32)


def _finalize(acc_ref, o_ref):
    outs = [acc_ref[h, 0:HEAD_DIM, :] / acc_ref[h, HEAD_DIM:HEAD_DIM + 1, :] for h in range(N_HEADS)]
    o_ref[...] = jnp.concatenate(outs, axis=0).T.astype(o_ref.dtype)


def _vblock(vT_ref, j, h):
    return vT_ref[j, h * V_ROWS:(h + 1) * V_ROWS, :]


def _run_key_blocks(own, last, buf_a, buf_b, scores_into, consume):
    def step(buf_next, j_next, buf_cur, j_cur, is_own):
        for h in range(N_HEADS):
            scores_into(buf_next, j_next, h)
            consume(buf_cur, j_cur, is_own, (h,))

    for h in range(N_HEADS):
        scores_into(buf_a, own, h)
    step(buf_b, 0, buf_a, own, True)

    def pair(p, carry):
        j0 = 2 * p
        step(buf_a, j0 + 1, buf_b, j0, False)
        step(buf_b, jnp.minimum(j0 + 2, last), buf_a, j0 + 1, False)
        return carry

    lax.fori_loop(0, own // 2, pair, 0)

    @pl.when(own % 2 == 1)
    def _():
        consume(buf_b, own - 1, False, range(N_HEADS))


def _key_rows(k_ref, j, tk):
    return k_ref[pl.ds(pl.multiple_of(j * tk, tk), tk), :]


def _causal_slabs(parts, causal):
    return [jnp.where(causal[n * SLAB:(n + 1) * SLAB], s, -jnp.inf) for n, s in enumerate(parts)]


def _two_tile_scores(buf, k_ref, qcat_ref, j, h, tk, tq):
    kb = k_ref[pl.ds(pl.multiple_of(j * tk, tk), tk), 256 * (h // 2):256 * (h // 2 + 1)]
    _store_scores(buf, h, _dot(kb, qcat_ref[:, h * tq:(h + 1) * tq]))


def _fox_kernel(qT_ref, k_ref, vT_ref, cbase_ref, o_ref, qcat_ref, m_ref, acc_ref, sa_ref, sb_ref, mxa_ref, mxb_ref):
    i = pl.program_id(1)
    tq = qT_ref.shape[1]
    zeros64 = jnp.zeros((HEAD_DIM, tq), jnp.bfloat16)
    row = lax.broadcasted_iota(jnp.int32, (128, tq), 0)
    for h in range(N_HEADS):
        qh = qT_ref[h * HEAD_DIM:(h + 1) * HEAD_DIM, :]
        pick = (row == _FG_LANE + h) | (row == _FG_LANE + 8 + h) | (row == _FG_LANE + 16 + h)
        rows = ([qh, zeros64] if h % 2 == 0 else [zeros64, qh]) + [pick.astype(jnp.bfloat16)]
        qcat_ref[:, h * tq:(h + 1) * tq] = jnp.concatenate(rows, axis=0)
    _init_state(m_ref, acc_ref)
    tk = sa_ref.shape[1]
    assert tq == tk
    causal = _own_block_mask(i, tq, tk)

    def scores_into(buf, j, h):
        _two_tile_scores(buf, k_ref, qcat_ref, j, h, tk, tq)

    def consume(buf, j, own, heads):
        offs = (cbase_ref[i] - cbase_ref[j]) * LOG2E
        for h in heads:
            parts, maxes = _slabs(buf, h, tq)
            if own:
                parts, maxes = _causal_slabs(parts, causal), None
            off = offs[:, _FG_LANE + h:_FG_LANE + h + 1]
            _softmax_update(parts, _vblock(vT_ref, j, h), m_ref, acc_ref, h, [off] * len(parts), maxes)

    _run_key_blocks(i, k_ref.shape[0] // tk - 1, (sa_ref, mxa_ref), (sb_ref, mxb_ref), scores_into, consume)
    _finalize(acc_ref, o_ref)


def _mla_kernel(qT_ref, k_ref, vT_ref, o_ref, qcat_ref, m_ref, acc_ref, sa_ref, sb_ref, mxa_ref, mxb_ref):
    i = pl.program_id(1)
    tq = qT_ref.shape[1]
    zeros64 = jnp.zeros((MLA_NOPE, tq), jnp.bfloat16)
    zeros96 = jnp.zeros((256 - 128 - MLA_ROPE, tq), jnp.bfloat16)
    for h in range(N_HEADS):
        qn = qT_ref[h * MLA_NOPE:(h + 1) * MLA_NOPE, :]
        qr1 = qT_ref[256 + 16 * h:256 + 16 * (h + 1), :]
        qr2 = qT_ref[320 + 16 * h:320 + 16 * (h + 1), :]
        nope = [qn, zeros64] if h % 2 == 0 else [zeros64, qn]
        qcat_ref[:, h * tq:(h + 1) * tq] = jnp.concatenate(nope + [qr1, qr2, zeros96], axis=0)
    _init_state(m_ref, acc_ref)
    tk = sa_ref.shape[1]
    causal = _own_block_mask(i, tq, tk)

    def scores_into(buf, j, h):
        _two_tile_scores(buf, k_ref, qcat_ref, j, h, tk, tq)

    def consume(buf, j, own, heads):
        for h in heads:
            parts, maxes = _slabs(buf, h, tq)
            if own:
                parts, maxes = _causal_slabs(parts, causal), None
            _softmax_update(parts, _vblock(vT_ref, j, h), m_ref, acc_ref, h, None, maxes)

    _run_key_blocks(i * tq // tk, k_ref.shape[0] // tk - 1, (sa_ref, mxa_ref), (sb_ref, mxb_ref), scores_into,
                    consume)
    _finalize(acc_ref, o_ref)


def _moba_kernel(qT_ref, k_ref, vT_ref, km_ref, o_ref, qcat_ref, m_ref, acc_ref, sa_ref, sb_ref, mxa_ref, mxb_ref,
                 bias_ref):
    i = pl.program_id(1)
    tq = qT_ref.shape[1]
    nb = km_ref.shape[1]
    q = qT_ref[...]
    row = lax.broadcasted_iota(jnp.int32, q.shape, 0)
    km = km_ref[0].astype(jnp.bfloat16)
    blk = lax.broadcasted_iota(jnp.int32, (nb, tq), 0)
    qblk = i * (tq // MOBA_BLOCK) + lax.broadcasted_iota(jnp.int32, (1, tq), 1) // MOBA_BLOCK
    neg_inf = jnp.float32(-jnp.inf)
    for h in range(N_HEADS):
        in_head = (((row >= 32 * h) & (row < 32 * (h + 1)))
                   | ((row >= 128 + 32 * h) & (row < 128 + 32 * (h + 1))))
        qm = jnp.where(in_head, q, jnp.zeros_like(q))
        qcat_ref[:, h * tq:(h + 1) * tq] = qm
        g = jnp.where(blk < qblk, _dot(km, qm), neg_inf)
        sel = jnp.zeros((nb, tq), jnp.bool_)
        for _ in range(MOBA_TOPK):
            mx = jnp.max(g, axis=0, keepdims=True)
            cand = jnp.where((g == mx) & (mx > neg_inf), blk, nb)
            chosen = blk == jnp.min(cand, axis=0, keepdims=True)
            sel = sel | chosen
            g = jnp.where(chosen, neg_inf, g)
        bias_ref[h * nb:(h + 1) * nb, :] = jnp.where(sel, 0.0, MASKED_LOGIT)
    _init_state(m_ref, acc_ref)
    tk = sa_ref.shape[1]
    per = tk // SLAB
    causal = _own_block_mask(i, tq, tk)

    def scores_into(buf, j, h):
        _store_scores(buf, h, _dot(_key_rows(k_ref, j, tk), qcat_ref[:, h * tq:(h + 1) * tq]))

    def consume(buf, j, own, heads):
        for h in heads:
            parts, maxes = _slabs(buf, h, tq)
            gates = [bias_ref[pl.ds(h * nb + per * j + n, 1), :] for n in range(per)]
            if own:
                gates = [jnp.where(per * j + n < qblk, g, 0.0) for n, g in enumerate(gates)]
                parts, maxes = _causal_slabs(parts, causal), None
            _softmax_update(parts, _vblock(vT_ref, j, h), m_ref, acc_ref, h, gates, maxes)

    _run_key_blocks(i * tq // tk, k_ref.shape[0] // tk - 1, (sa_ref, mxa_ref), (sb_ref, mxb_ref), scores_into,
                    consume)
    _finalize(acc_ref, o_ref)


def _dilated_kernel(q_ref, k_ref, v_ref, o_ref, lse_ref, *, window, tq):
    i = pl.program_id(2)
    n = k_ref.shape[1]
    tiles = q_ref.shape[1] // tq
    lane = lax.broadcasted_iota(jnp.int32, (tq, GROUP_W), 1)
    for c in range(q_ref.shape[2] // GROUP_W):
        cls = slice(c * GROUP_W, (c + 1) * GROUP_W)
        for g in range(tiles):
            rows = slice(g * tq, (g + 1) * tq)
            q = q_ref[0, rows, cls]
            a = (i * tiles + g) * tq
            ks = jnp.clip(a - DIL_WINDOW_STEPS, 0, n - window)
            ks = pl.multiple_of(ks, DIL_WINDOW_STEPS)
            kw = k_ref[0, pl.ds(ks, window), cls]
            vw = v_ref[0, pl.ds(ks, window), cls]
            jq = a + lax.broadcasted_iota(jnp.int32, (tq, window), 0)
            jk = ks + lax.broadcasted_iota(jnp.int32, (tq, window), 1)
            band = (jq - jk >= 0) & (jq - jk <= DIL_WINDOW_STEPS)
            o = jnp.zeros((tq, GROUP_W), jnp.float32)
            lse = jnp.zeros((tq, GROUP_W), jnp.float32)
            for h in range(N_HEADS):
                in_head = (((lane >= 32 * h) & (lane < 32 * (h + 1)))
                           | ((lane >= 128 + 32 * h) & (lane < 128 + 32 * (h + 1))))
                qm = jnp.where(in_head, q, jnp.zeros_like(q))
                s = jnp.where(band, _dot_nt(qm, kw), -jnp.inf)
                m = jnp.max(s, axis=1, keepdims=True)
                p = jnp.exp2(s - m)
                l = jnp.sum(p, axis=1, keepdims=True)
                oh = _dot(p.astype(jnp.bfloat16), vw) / l
                out_lanes = (lane >= h * HEAD_DIM) & (lane < (h + 1) * HEAD_DIM)
                o = jnp.where(out_lanes, oh, o)
                lse = jnp.where(out_lanes, m + jnp.log2(l), lse)
            o_ref[0, rows, cls] = o
            lse_ref[0, rows, cls] = lse


def _out_ffn_kernel(x_ref, omoba_ref, ofox_ref, omla_ref, od1_ref, od4_ref, od16_ref, l1_ref, l4_ref, l16_ref,
                    wout_ref, gpost_ref, gpre_ref, wg_ref, wu_ref, wd_ref, gpf_ref, out_ref, acc_ref,
                    dscr_ref):
    bf16 = jnp.bfloat16
    tm = x_ref.shape[0]

    def token_order(ref, slot, dil):
        halves = GROUP_W // 128
        for rho in range(dil):
            for c in range(halves):
                lanes = slice(rho * GROUP_W + c * 128, rho * GROUP_W + (c + 1) * 128)
                dscr_ref[slot + c, pl.ds(rho, tm // dil, stride=dil), :] = ref[:, lanes]
        return jnp.concatenate([dscr_ref[slot + c] for c in range(halves)], axis=1)

    l1, od1 = l1_ref[...], od1_ref[...]
    l4, od4 = token_order(l4_ref, 0, 4), token_order(od4_ref, 2, 4)
    l16, od16 = token_order(l16_ref, 4, 16), token_order(od16_ref, 6, 16)
    m = jnp.maximum(jnp.maximum(l1, l4), l16)
    e1, e4, e16 = jnp.exp2(l1 - m), jnp.exp2(l4 - m), jnp.exp2(l16 - m)
    odil = (e1 * od1 + e4 * od4 + e16 * od16) / (e1 + e4 + e16)
    y = (_dot(omoba_ref[...], wout_ref[0]) + _dot(ofox_ref[...], wout_ref[1])
         + _dot(omla_ref[...], wout_ref[2]) + _dot(odil.astype(bf16), wout_ref[3]))
    x1 = x_ref[...] + y * _rms_scale(y, D_MODEL) * gpost_ref[...]
    hb = (x1 * _rms_scale(x1, D_MODEL) * gpre_ref[...]).astype(bf16)
    acc_ref[...] = jnp.zeros_like(acc_ref)
    for c in range(D_FF // FF_CHUNK):
        cols = slice(c * FF_CHUNK, (c + 1) * FF_CHUNK)
        g = _dot(hb, wg_ref[:, cols])
        u = _dot(hb, wu_ref[:, cols])
        f = (g * jax.nn.sigmoid(g) * u).astype(bf16)
        acc_ref[...] += _dot(f, wd_ref[cols, :])
    f = acc_ref[...]
    out_ref[...] = x1 + f * _rms_scale(f, D_MODEL) * gpf_ref[...]


def _rope_tables(positions):
    t = positions.reshape(-1).astype(jnp.float32)

    def tab(dim):
        inv = ROPE_THETA ** (-jnp.arange(0, dim, 2, dtype=jnp.float32) / dim)
        ang = t[:, None] * inv
        return jnp.cos(ang), jnp.sin(ang)

    c64, s64 = tab(HEAD_DIM)
    c16, s16 = tab(MLA_ROPE)
    ct64, st64 = jnp.tile(c64, (1, 4)), jnp.tile(s64, (1, 4))
    pad = jnp.zeros((t.shape[0], 128 - MLA_ROPE), jnp.float32)
    ct16 = jnp.concatenate([c16, c16, pad], axis=1)
    st16 = jnp.concatenate([s16, s16, pad], axis=1)
    cT16, sT16 = jnp.tile(c16, (1, 4)).T, jnp.tile(s16, (1, 4)).T
    return ct64, st64, ct64.T, st64.T, ct16, st16, cT16, sT16


_HALF_PERM = np.array([h * 64 + half * 32 + j for half in (0, 1) for h in range(4) for j in range(32)])
_QROPE_ROWS = np.array([h * 96 + 64 + half * 16 + j for half in (0, 1) for h in range(4) for j in range(16)])
_QNOPE_ROWS = np.array([h * 96 + j for h in range(4) for j in range(64)])
_KNOPE_COLS = np.array([h * 128 + j for h in range(4) for j in range(64)])
_VMLA_COLS = np.array([h * 128 + 64 + j for h in range(4) for j in range(64)])


def _prep_layer_weights(w_in, b_forget, g_mla_q, w_mla_q_up, g_mla_kv, w_mla_kv_up, w_out, w_gate, w_up, w_down):
    bf16 = jnp.bfloat16
    depth = w_in.shape[0]
    sl = lambda a, b: w_in[:, :, a:b]
    scale = HEAD_DIM ** -0.5 * LOG2E
    moba_q, moba_k, moba_v = sl(0, 256)[..., _HALF_PERM] * scale, sl(256, 512)[..., _HALF_PERM], sl(512, 768)
    fox_q, fox_k, fox_v = sl(768, 1024) * scale, sl(1024, 1280), sl(1280, 1536)
    fg, cq, ckv, kr = sl(1536, 1540), sl(1540, 1732), sl(1732, 1860), sl(1860, 1892)
    dil_q, dil_k, dil_v = sl(1892, 2148)[..., _HALF_PERM] * scale, sl(2148, 2404)[..., _HALF_PERM], sl(2404, 2660)
    kr_rot = jnp.concatenate([-kr[..., 16:], kr[..., :16]], axis=-1)
    z = lambda n: jnp.zeros((depth, D_MODEL, n), w_in.dtype)
    wtok = jnp.concatenate([moba_k, fox_k, dil_q, dil_k, dil_v, cq, z(256 - MLA_Q_RANK), ckv,
                            kr, fg, z(128 - 36), kr_rot, z(96)], axis=-1).astype(bf16)
    wtr = jnp.swapaxes(jnp.concatenate([moba_q, moba_v, fox_q, fox_v], axis=-1), 1, 2).astype(bf16)
    bfg = jnp.zeros((depth, 1, 128), jnp.float32).at[:, 0, _FG_LANE:_FG_LANE + N_HEADS].set(b_forget)
    gq = jnp.pad(g_mla_q, ((0, 0), (0, 256 - MLA_Q_RANK)))[:, None, :]
    wq_rows = jnp.swapaxes(w_mla_q_up, 1, 2)
    wqT = jnp.concatenate([wq_rows[:, _QNOPE_ROWS], wq_rows[:, _QROPE_ROWS]], axis=1)
    wqT = jnp.pad(wqT, ((0, 0), (0, 0), (0, 256 - MLA_Q_RANK))).astype(bf16)
    gkv = g_mla_kv[:, None, :]
    wkn = w_mla_kv_up[:, :, _KNOPE_COLS].astype(bf16)
    wvT = jnp.swapaxes(w_mla_kv_up[:, :, _VMLA_COLS], 1, 2).astype(bf16)
    wout = w_out.reshape(depth, 4, GROUP_W, D_MODEL).astype(bf16)
    wg, wu, wd = w_gate.astype(bf16), w_up.astype(bf16), w_down.astype(bf16)
    return wtok, wtr, bfg, gq, wqT, gkv, wkn, wvT, wout, wg, wu, wd


def _const_spec(shape):
    return pl.BlockSpec(shape, lambda *_: (0,) * len(shape))


def _params(*sem):
    return pltpu.CompilerParams(dimension_semantics=sem, vmem_limit_bytes=VMEM_LIMIT)


def _in_proj(x2, g, wtok, wtr, tri, tables, bfg, gq, wqT, gkv, wkn, wvT, seq):
    t = x2.shape[0]
    tm = ROW_TILE
    nt = t // tm
    bf16, f32 = jnp.bfloat16, jnp.float32
    ct64, st64, cT64, sT64, ct16, st16, cT16, sT16 = tables
    tok_spec = lambda w: pl.BlockSpec((tm, w), lambda i: (i, 0))
    tr_spec = lambda r: pl.BlockSpec((r, tm), lambda i: (0, i))
    blk3 = lambda n, r, c: pl.BlockSpec((n, r, c), lambda i: (i, 0, 0))
    mb = tm // MOBA_BLOCK
    ab = tm // KEY_BLOCK
    vrows = N_HEADS * V_ROWS
    vt_shape = jax.ShapeDtypeStruct((t // KEY_BLOCK, vrows, KEY_BLOCK), bf16)
    in_specs = [tok_spec(D_MODEL), _const_spec((1, D_MODEL)), _const_spec(wtok.shape), _const_spec(wtr.shape),
                _const_spec(tri.shape), tok_spec(128), tok_spec(128), tr_spec(128), tr_spec(128),
                tok_spec(128), tok_spec(128), tr_spec(64), tr_spec(64), _const_spec((1, 128)),
                _const_spec((1, 256)), _const_spec(wqT.shape), _const_spec((1, 128)), _const_spec(wkn.shape),
                _const_spec(wvT.shape)]
    out_shape = [
        jax.ShapeDtypeStruct((t, GROUP_W), bf16),
        jax.ShapeDtypeStruct((nt, mb, GROUP_W), f32),
        jax.ShapeDtypeStruct((GROUP_W, t), bf16),
        vt_shape,
        jax.ShapeDtypeStruct((t, 512), bf16),
        jax.ShapeDtypeStruct((GROUP_W, t), bf16),
        vt_shape,
        jax.ShapeDtypeStruct((nt, 1, 128), f32),
        *[jax.ShapeDtypeStruct((t // d, d * GROUP_W), bf16) for d in DILATIONS] * 3,
        jax.ShapeDtypeStruct((t, 512), bf16),
        jax.ShapeDtypeStruct((384, t), bf16),
        vt_shape,
    ]
    vt_spec = blk3(ab, vrows, KEY_BLOCK)
    out_specs = [tok_spec(GROUP_W), blk3(1, mb, GROUP_W), tr_spec(GROUP_W), vt_spec,
                 tok_spec(512), tr_spec(GROUP_W), vt_spec, blk3(1, 1, 128),
                 *[pl.BlockSpec((tm // d, d * GROUP_W), lambda i: (i, 0)) for d in DILATIONS] * 3,
                 tok_spec(512), tr_spec(384), vt_spec]
    return pl.pallas_call(
        functools.partial(_in_proj_kernel, tiles_per_seq=seq // tm),
        grid=(nt,), in_specs=in_specs, out_specs=out_specs, out_shape=out_shape,
        scratch_shapes=[pltpu.VMEM((1, 128), f32), pltpu.VMEM((3 * GROUP_W // 128, tm, 128), f32)],
        compiler_params=_params("arbitrary"), name="in_proj",
    )(x2, g, wtok, wtr, tri, ct64, st64, cT64, sT64, ct16, st16, cT16, sT16, bfg, gq, wqT, gkv, wkn, wvT)


def _attention_scratch(tq, tk=KEY_BLOCK):
    return [pltpu.VMEM((GROUP_W, N_HEADS * tq), jnp.bfloat16),
            pltpu.VMEM((8, tq), jnp.float32),
            pltpu.VMEM((N_HEADS, V_ROWS, tq), jnp.float32),
            pltpu.VMEM((N_HEADS, tk, tq), jnp.float32),
            pltpu.VMEM((N_HEADS, tk, tq), jnp.float32),
            pltpu.VMEM((N_HEADS, 8, tq), jnp.float32),
            pltpu.VMEM((N_HEADS, 8, tq), jnp.float32)]


def _dense_attention(body, name, qT, k, vT, extra, batch, seq):
    tq = ATT_TILE
    nq = seq // tq
    t = batch * seq
    in_specs = [pl.BlockSpec((qT.shape[0], tq), lambda b, i: (0, b * nq + i)),
                pl.BlockSpec((seq, k.shape[1]), lambda b, i: (b, 0)),
                pl.BlockSpec((seq // KEY_BLOCK, N_HEADS * V_ROWS, KEY_BLOCK), lambda b, i: (b, 0, 0))]
    in_specs += [pl.BlockSpec((seq // KEY_BLOCK, 1, 128), lambda b, i: (b, 0, 0)) for _ in extra]
    return pl.pallas_call(
        body, grid=(batch, nq), in_specs=in_specs,
        out_specs=pl.BlockSpec((tq, GROUP_W), lambda b, i: (b * nq + i, 0)),
        out_shape=jax.ShapeDtypeStruct((t, GROUP_W), jnp.bfloat16),
        scratch_shapes=_attention_scratch(tq),
        compiler_params=_params("arbitrary", "arbitrary"), name=name,
    )(qT, k, vT, *extra)


def _moba_attention(qT, k, vT, kmean, batch, seq):
    tq = ATT_TILE
    nq = seq // tq
    nb = seq // MOBA_BLOCK
    t = batch * seq
    return pl.pallas_call(
        _moba_kernel, grid=(batch, nq),
        in_specs=[pl.BlockSpec((GROUP_W, tq), lambda b, i: (0, b * nq + i)),
                  pl.BlockSpec((seq, GROUP_W), lambda b, i: (b, 0)),
                  pl.BlockSpec((seq // KEY_BLOCK, N_HEADS * V_ROWS, KEY_BLOCK), lambda b, i: (b, 0, 0)),
                  pl.BlockSpec((1, nb, GROUP_W), lambda b, i: (b, 0, 0))],
        out_specs=pl.BlockSpec((tq, GROUP_W), lambda b, i: (b * nq + i, 0)),
        out_shape=jax.ShapeDtypeStruct((t, GROUP_W), jnp.bfloat16),
        scratch_shapes=_attention_scratch(tq) + [pltpu.VMEM((N_HEADS * nb, tq), jnp.float32)],
        compiler_params=_params("arbitrary", "arbitrary"), name="moba_attention",
    )(qT, k, vT, kmean)


def _dilated_attention(q, k, v, batch, seq, dil):
    n = seq // dil
    tq = n if n <= 2 * DIL_TILE else DIL_TILE
    window = min(tq + DIL_WINDOW_STEPS, n)
    view = lambda a: a.reshape(batch, n, dil * GROUP_W)
    tiles = min(DIL_PROBLEMS, n // tq)
    classes = min(DIL_PROBLEMS // tiles, dil)
    qspec = pl.BlockSpec((1, tiles * tq, classes * GROUP_W), lambda b, r, i: (b, i, r))
    kspec = pl.BlockSpec((1, n, classes * GROUP_W), lambda b, r, i: (b, 0, r))
    shape = jax.ShapeDtypeStruct((batch, n, dil * GROUP_W), jnp.float32)
    o, lse = pl.pallas_call(
        functools.partial(_dilated_kernel, window=window, tq=tq),
        grid=(batch, dil // classes, n // (tiles * tq)),
        in_specs=[qspec, kspec, kspec], out_specs=[qspec, qspec], out_shape=[shape, shape],
        compiler_params=_params("arbitrary", "arbitrary", "arbitrary"), name=f"dilated_attention_d{dil}",
    )(view(q), view(k), view(v))
    return o.reshape(batch * n, dil * GROUP_W), lse.reshape(batch * n, dil * GROUP_W)


def _out_ffn(x2, omoba, ofox, omla, dil_outs, wout, gpost, gpre, wg, wu, wd, gpf):
    t = x2.shape[0]
    tm = ROW_TILE
    row = lambda w: pl.BlockSpec((tm, w), lambda i: (i, 0))
    (od1, l1), (od4, l4), (od16, l16) = dil_outs
    dil_specs = [pl.BlockSpec((tm // d, d * GROUP_W), lambda i: (i, 0)) for d in DILATIONS]
    in_specs = ([row(D_MODEL)] + [row(GROUP_W)] * 3 + dil_specs * 2
                + [_const_spec(wout.shape), _const_spec((1, D_MODEL)), _const_spec((1, D_MODEL)),
                   _const_spec(wg.shape), _const_spec(wu.shape), _const_spec(wd.shape), _const_spec((1, D_MODEL))])
    return pl.pallas_call(
        _out_ffn_kernel, grid=(t // tm,), in_specs=in_specs, out_specs=row(D_MODEL),
        out_shape=jax.ShapeDtypeStruct((t, D_MODEL), jnp.float32),
        scratch_shapes=[pltpu.VMEM((tm, D_MODEL), jnp.float32), pltpu.VMEM((4 * GROUP_W // 128, tm, 128), jnp.float32)],
        compiler_params=_params("arbitrary"), name="out_ffn",
    )(x2, omoba, ofox, omla, od1, od4, od16, l1, l4, l16, wout, gpost, gpre, wg, wu, wd, gpf)


def kernel(x, positions, w_in, b_forget, g_mla_q, w_mla_q_up, g_mla_kv, w_mla_kv_up, w_out, g_pre_mix, g_post_mix, w_gate, w_up, w_down, g_pre_ffn, g_post_ffn):
    batch, seq, _ = x.shape
    depth = w_in.shape[0]
    assert seq % ROW_TILE == 0 and seq % (DILATIONS[-1] * DIL_WINDOW_STEPS) == 0
    assert ROW_TILE == KEY_BLOCK
    tables = _rope_tables(positions)
    wtok, wtr, bfg, gq, wqT, gkv, wkn, wvT, wout, wg, wu, wd = _prep_layer_weights(
        w_in, b_forget, g_mla_q, w_mla_q_up, g_mla_kv, w_mla_kv_up, w_out, w_gate, w_up, w_down)
    tri = jnp.tril(jnp.ones((ROW_TILE, ROW_TILE), jnp.bfloat16))
    x2 = x.reshape(batch * seq, D_MODEL)
    for l in range(depth):
        (k_moba, kmean, qT_moba, vT_moba, k_fox, qT_fox, vT_fox, cbase, *dil_qkv,
         k_mla, qT_mla, vT_mla) = _in_proj(x2, g_pre_mix[l][None], wtok[l], wtr[l], tri, tables, bfg[l], gq[l],
                                          wqT[l], gkv[l], wkn[l], wvT[l], seq)
        nd = len(DILATIONS)
        o_moba = _moba_attention(qT_moba, k_moba, vT_moba, kmean.reshape(batch, seq // MOBA_BLOCK, GROUP_W),
                                 batch, seq)
        o_fox = _dense_attention(_fox_kernel, "fox_attention", qT_fox, k_fox, vT_fox, [cbase], batch, seq)
        o_mla = _dense_attention(_mla_kernel, "mla_attention", qT_mla, k_mla, vT_mla, [], batch, seq)
        dil_outs = [_dilated_attention(dil_qkv[n], dil_qkv[nd + n], dil_qkv[2 * nd + n], batch, seq, d)
                    for n, d in enumerate(DILATIONS)]
        x2 = _out_ffn(x2, o_moba, o_fox, o_mla, dil_outs, wout[l], g_post_mix[l][None], g_pre_ffn[l][None],
                      wg[l], wu[l], wd[l], g_post_ffn[l][None])
    return x2.reshape(batch, seq, D_MODEL)
```

```python
import functools

import numpy as np
import jax
import jax.numpy as jnp
from jax import lax
from jax.experimental import pallas as pl
from jax.experimental.pallas import tpu as pltpu

D_MODEL = 1024
HEAD_DIM = 64
N_HEADS = 4
GROUP_W = N_HEADS * HEAD_DIM
MOBA_BLOCK = 256
MOBA_TOPK = 3
MLA_Q_RANK = 192
MLA_KV_RANK = 128
MLA_NOPE = 64
MLA_ROPE = 32
DIL_WINDOW_STEPS = 128
DILATIONS = (1, 4, 16)
ROPE_THETA = 10000.0
RMS_EPS = 1e-6
D_FF = 2816
FF_CHUNK = 256
V_ROWS = 80
LOG2E = 1.4426950408889634
MASKED_LOGIT = -1e30

ROW_TILE = 512
ATT_TILE = 512
KEY_BLOCK = 512
SLAB = MOBA_BLOCK
DIL_TILE = 256
DIL_PROBLEMS = 8
VMEM_LIMIT = 56 * 1024 * 1024

_C_KMOBA, _C_KFOX, _C_QDIL, _C_KDIL, _C_VDIL, _C_CQ = 0, 256, 512, 768, 1024, 1280
_C_CKV, _C_X, _C_Y, _N_TOK = 1536, 1664, 1792, 1920
_FG_LANE = 32

_NT = (((1,), (1,)), ((), ()))


def _dot(a, b):
    return jnp.dot(a, b, preferred_element_type=jnp.float32)


def _dot_nt(a, b):
    return lax.dot_general(a, b, _NT, preferred_element_type=jnp.float32)


def _rms_scale(v, n):
    return lax.rsqrt(jnp.sum(v * v, axis=-1, keepdims=True) * (1.0 / n) + RMS_EPS)


def _in_proj_kernel(x_ref, g_ref, wtok_ref, wtr_ref, tri_ref, ct64_ref, st64_ref, cT64_ref, sT64_ref,
                    ct16_ref, st16_ref, cT16_ref, sT16_ref, bfg_ref, gq_ref, wqT_ref, gkv_ref, wkn_ref, wvT_ref,
                    kmoba_ref, kmean_ref, qTmoba_ref, vTmoba_ref, kfox_ref, qTfox_ref, vTfox_ref, cbase_ref,
                    q1_ref, q4_ref, q16_ref, k1_ref, k4_ref, k16_ref, v1_ref, v4_ref, v16_ref,
                    kmla_ref, qTmla_ref, vTmla_ref, carry_ref, dscr_ref, *, tiles_per_seq):
    bf16 = jnp.bfloat16
    qdil_refs, kdil_refs, vdil_refs = (q1_ref, q4_ref, q16_ref), (k1_ref, k4_ref, k16_ref), (v1_ref, v4_ref, v16_ref)
    x = x_ref[...]
    hb = (x * _rms_scale(x, D_MODEL) * g_ref[...]).astype(bf16)
    tm = x.shape[0]

    def tok(c0, w):
        return _dot(hb, wtok_ref[:, c0:c0 + w])

    def rope_tok(z):
        c, s = ct64_ref[...], st64_ref[...]
        x1, x2 = z[:, :128], z[:, 128:]
        return jnp.concatenate([x1 * c - x2 * s, x2 * c + x1 * s], axis=1)

    def with_ones(vT):
        ones = jnp.ones((V_ROWS - HEAD_DIM, tm), bf16)
        parts = []
        for h in range(N_HEADS):
            parts += [vT[h * HEAD_DIM:(h + 1) * HEAD_DIM].astype(bf16), ones]
        return jnp.concatenate(parts, axis=0)

    def rope_tr(zT):
        c, s = cT64_ref[...], sT64_ref[...]
        x1, x2 = zT[:128], zT[128:]
        return jnp.concatenate([x1 * c - x2 * s, x2 * c + x1 * s], axis=0)

    k_moba = rope_tok(tok(_C_KMOBA, GROUP_W))
    kmoba_ref[...] = k_moba.astype(bf16)
    for blk in range(tm // MOBA_BLOCK):
        kmean_ref[0, blk:blk + 1, :] = jnp.mean(k_moba[blk * MOBA_BLOCK:(blk + 1) * MOBA_BLOCK], axis=0, keepdims=True)
    qTmoba_ref[...] = rope_tr(_dot_nt(wtr_ref[0:256, :], hb)).astype(bf16)
    def store_value_blocks(ref, vT):
        for blk in range(tm // KEY_BLOCK):
            ref[blk] = vT[:, blk * KEY_BLOCK:(blk + 1) * KEY_BLOCK]

    store_value_blocks(vTmoba_ref, with_ones(_dot_nt(wtr_ref[256:512, :], hb)))

    qTfox_ref[...] = _dot_nt(wtr_ref[512:768, :], hb).astype(bf16)
    store_value_blocks(vTfox_ref, with_ones(_dot_nt(wtr_ref[768:1024, :], hb)))

    def emit_dilated(refs, slot, z):
        refs[0][...] = z.astype(bf16)
        for c in range(GROUP_W // 128):
            dscr_ref[slot + c] = z[:, c * 128:(c + 1) * 128]
        for ref, dil in zip(refs[1:], DILATIONS[1:]):
            for rho in range(dil):
                for c in range(GROUP_W // 128):
                    rows = dscr_ref[slot + c, pl.ds(rho, tm // dil, stride=dil), :]
                    ref[:, rho * GROUP_W + c * 128:rho * GROUP_W + (c + 1) * 128] = rows.astype(bf16)

    emit_dilated(qdil_refs, 0, rope_tok(tok(_C_QDIL, GROUP_W)))
    emit_dilated(kdil_refs, 2, rope_tok(tok(_C_KDIL, GROUP_W)))
    emit_dilated(vdil_refs, 4, tok(_C_VDIL, GROUP_W))

    xy = tok(_C_X, 256)
    xblk = xy[:, :128]
    yblk = xy[:, 128:]
    fg = xblk + bfg_ref[...]
    logf = jnp.minimum(fg, 0.0) - jnp.log1p(jnp.exp(-jnp.abs(fg)))
    lane = lax.broadcasted_iota(jnp.int32, logf.shape, 1)
    is_gate = (lane >= _FG_LANE) & (lane < _FG_LANE + N_HEADS)

    def three_pieces(v):
        v = jnp.where(is_gate, v, 0.0)
        p1 = v.astype(bf16).astype(jnp.float32)
        p2 = (v - p1).astype(bf16).astype(jnp.float32)
        p3 = (v - p1 - p2).astype(bf16).astype(jnp.float32)
        return p1 + pltpu.roll(p2, 8, 1) + pltpu.roll(p3, 16, 1)

    @pl.when(pl.program_id(0) % tiles_per_seq == 0)
    def _():
        carry_ref[...] = jnp.zeros_like(carry_ref)

    part = _dot(tri_ref[...], three_pieces(logf).astype(bf16))
    cum = part + pltpu.roll(part, 128 - 8, 1) + pltpu.roll(part, 128 - 16, 1) + carry_ref[...]
    carry_ref[...] = cum[tm - 1:tm, :]
    cbase_ref[0] = cum[0:1, :]

    dcols = three_pieces((cum[0:1, :] - cum) * LOG2E)
    kf = tok(_C_KFOX, GROUP_W)
    kfox_ref[...] = jnp.concatenate([kf[:, :128], dcols, kf[:, 128:], dcols], axis=1).astype(bf16)

    zcq = tok(_C_CQ, 256)
    cq = (zcq * _rms_scale(zcq, MLA_Q_RANK) * gq_ref[...]).astype(bf16)
    qcT = _dot_nt(wqT_ref[...], cq)
    c16, s16 = cT16_ref[...], sT16_ref[...]
    r1h, r2h = qcT[256:320], qcT[320:384]
    qT = jnp.concatenate([qcT[0:256], r1h * c16 - r2h * s16, r2h * c16 + r1h * s16], axis=0)
    qTmla_ref[...] = (qT * ((MLA_NOPE + MLA_ROPE) ** -0.5 * LOG2E)).astype(bf16)

    zckv = tok(_C_CKV, MLA_KV_RANK)
    ckv = (zckv * _rms_scale(zckv, MLA_KV_RANK) * gkv_ref[...]).astype(bf16)
    kn = _dot(ckv, wkn_ref[...])
    krope = xblk * ct16_ref[...] + yblk * st16_ref[...]
    kmla_ref[...] = jnp.concatenate([kn[:, :128], krope, kn[:, 128:], krope], axis=1).astype(bf16)
    store_value_blocks(vTmla_ref, with_ones(_dot_nt(wvT_ref[...], ckv)))


def _softmax_update(parts, vb, m_ref, acc_ref, h, offsets=None, maxes=None):
    if offsets is None:
        offsets = [None] * len(parts)
    if maxes is None:
        maxes = [jnp.max(s, axis=0, keepdims=True) for s in parts]
    m_old = m_ref[h:h + 1, :]
    blk_max = None
    for mx, off in zip(maxes, offsets):
        mx = mx if off is None else mx + off
        blk_max = mx if blk_max is None else jnp.maximum(blk_max, mx)
    m_new = jnp.maximum(m_old, blk_max)
    ps = []
    for s, off in zip(parts, offsets):
        shift = m_new if off is None else m_new - off
        ps.append(jnp.exp2(s - shift).astype(jnp.bfloat16))
    p = ps[0] if len(ps) == 1 else jnp.concatenate(ps, axis=0)
    acc_ref[h] = jnp.exp2(m_old - m_new) * acc_ref[h] + _dot(vb, p)
    m_ref[h:h + 1, :] = m_new


def _store_scores(buf, first_head, s):
    s_ref, mx_ref = buf
    tq = s_ref.shape[2]
    for k in range(s.shape[1] // tq):
        sh = s[:, k * tq:(k + 1) * tq]
        s_ref[first_head + k] = sh
        for n in range(sh.shape[0] // SLAB):
            mx_ref[first_head + k, n:n + 1, :] = jnp.max(sh[n * SLAB:(n + 1) * SLAB], axis=0, keepdims=True)


def _slabs(buf, h, tq):
    s_ref, mx_ref = buf
    n_slabs = s_ref.shape[1] // SLAB
    return ([s_ref[h, n * SLAB:(n + 1) * SLAB, :] for n in range(n_slabs)],
            [mx_ref[h, n:n + 1, :] for n in range(n_slabs)])


def _own_block_mask(i, tq, tk):
    key = lax.broadcasted_iota(jnp.int32, (tk, tq), 0)
    qry = lax.broadcasted_iota(jnp.int32, (tk, tq), 1) + (i % (tk // tq)) * tq
    return key <= qry


def _init_state(m_ref, acc_ref):
    m_ref[...] = jnp.full(m_ref.shape, -jnp.inf, jnp.float32)
    acc_ref[...] = jnp.zeros(acc_ref.shape, jnp.float32)


def _finalize(acc_ref, o_ref):
    outs = [acc_ref[h, 0:HEAD_DIM, :] / acc_ref[h, HEAD_DIM:HEAD_DIM + 1, :] for h in range(N_HEADS)]
    o_ref[...] = jnp.concatenate(outs, axis=0).T.astype(o_ref.dtype)


def _vblock(vT_ref, j, h):
    return vT_ref[j, h * V_ROWS:(h + 1) * V_ROWS, :]


def _run_key_blocks(own, last, buf_a, buf_b, scores_into, consume):
    def step(buf_next, j_next, buf_cur, j_cur, is_own):
        for h in range(N_HEADS):
            scores_into(buf_next, j_next, h)
            consume(buf_cur, j_cur, is_own, (h,))

    for h in range(N_HEADS):
        scores_into(buf_a, own, h)
    step(buf_b, 0, buf_a, own, True)

    def pair(p, carry):
        j0 = 2 * p
        step(buf_a, j0 + 1, buf_b, j0, False)
        step(buf_b, jnp.minimum(j0 + 2, last), buf_a, j0 + 1, False)
        return carry

    lax.fori_loop(0, own // 2, pair, 0)

    @pl.when(own % 2 == 1)
    def _():
        consume(buf_b, own - 1, False, range(N_HEADS))


def _key_rows(k_ref, j, tk):
    return k_ref[pl.ds(pl.multiple_of(j * tk, tk), tk), :]


def _causal_slabs(parts, causal):
    return [jnp.where(causal[n * SLAB:(n + 1) * SLAB], s, -jnp.inf) for n, s in enumerate(parts)]


def _two_tile_scores(buf, k_ref, qcat_ref, j, h, tk, tq):
    kb = k_ref[pl.ds(pl.multiple_of(j * tk, tk), tk), 256 * (h // 2):256 * (h // 2 + 1)]
    _store_scores(buf, h, _dot(kb, qcat_ref[:, h * tq:(h + 1) * tq]))


def _fox_kernel(qT_ref, k_ref, vT_ref, cbase_ref, o_ref, qcat_ref, m_ref, acc_ref, sa_ref, sb_ref, mxa_ref, mxb_ref):
    i = pl.program_id(1)
    tq = qT_ref.shape[1]
    zeros64 = jnp.zeros((HEAD_DIM, tq), jnp.bfloat16)
    row = lax.broadcasted_iota(jnp.int32, (128, tq), 0)
    for h in range(N_HEADS):
        qh = qT_ref[h * HEAD_DIM:(h + 1) * HEAD_DIM, :]
        pick = (row == _FG_LANE + h) | (row == _FG_LANE + 8 + h) | (row == _FG_LANE + 16 + h)
        rows = ([qh, zeros64] if h % 2 == 0 else [zeros64, qh]) + [pick.astype(jnp.bfloat16)]
        qcat_ref[:, h * tq:(h + 1) * tq] = jnp.concatenate(rows, axis=0)
    _init_state(m_ref, acc_ref)
    tk = sa_ref.shape[1]
    assert tq == tk
    causal = _own_block_mask(i, tq, tk)

    def scores_into(buf, j, h):
        _two_tile_scores(buf, k_ref, qcat_ref, j, h, tk, tq)

    def consume(buf, j, own, heads):
        offs = (cbase_ref[i] - cbase_ref[j]) * LOG2E
        for h in heads:
            parts, maxes = _slabs(buf, h, tq)
            if own:
                parts, maxes = _causal_slabs(parts, causal), None
            off = offs[:, _FG_LANE + h:_FG_LANE + h + 1]
            _softmax_update(parts, _vblock(vT_ref, j, h), m_ref, acc_ref, h, [off] * len(parts), maxes)

    _run_key_blocks(i, k_ref.shape[0] // tk - 1, (sa_ref, mxa_ref), (sb_ref, mxb_ref), scores_into, consume)
    _finalize(acc_ref, o_ref)


def _mla_kernel(qT_ref, k_ref, vT_ref, o_ref, qcat_ref, m_ref, acc_ref, sa_ref, sb_ref, mxa_ref, mxb_ref):
    i = pl.program_id(1)
    tq = qT_ref.shape[1]
    zeros64 = jnp.zeros((MLA_NOPE, tq), jnp.bfloat16)
    zeros96 = jnp.zeros((256 - 128 - MLA_ROPE, tq), jnp.bfloat16)
    for h in range(N_HEADS):
        qn = qT_ref[h * MLA_NOPE:(h + 1) * MLA_NOPE, :]
        qr1 = qT_ref[256 + 16 * h:256 + 16 * (h + 1), :]
        qr2 = qT_ref[320 + 16 * h:320 + 16 * (h + 1), :]
        nope = [qn, zeros64] if h % 2 == 0 else [zeros64, qn]
        qcat_ref[:, h * tq:(h + 1) * tq] = jnp.concatenate(nope + [qr1, qr2, zeros96], axis=0)
    _init_state(m_ref, acc_ref)
    tk = sa_ref.shape[1]
    causal = _own_block_mask(i, tq, tk)

    def scores_into(buf, j, h):
        _two_tile_scores(buf, k_ref, qcat_ref, j, h, tk, tq)

    def consume(buf, j, own, heads):
        for h in heads:
            parts, maxes = _slabs(buf, h, tq)
            if own:
                parts, maxes = _causal_slabs(parts, causal), None
            _softmax_update(parts, _vblock(vT_ref, j, h), m_ref, acc_ref, h, None, maxes)

    _run_key_blocks(i * tq // tk, k_ref.shape[0] // tk - 1, (sa_ref, mxa_ref), (sb_ref, mxb_ref), scores_into,
                    consume)
    _finalize(acc_ref, o_ref)


def _moba_kernel(qT_ref, k_ref, vT_ref, km_ref, o_ref, qcat_ref, m_ref, acc_ref, sa_ref, sb_ref, mxa_ref, mxb_ref,
                 bias_ref):
    i = pl.program_id(1)
    tq = qT_ref.shape[1]
    nb = km_ref.shape[1]
    q = qT_ref[...]
    row = lax.broadcasted_iota(jnp.int32, q.shape, 0)
    km = km_ref[0].astype(jnp.bfloat16)
    blk = lax.broadcasted_iota(jnp.int32, (nb, tq), 0)
    qblk = i * (tq // MOBA_BLOCK) + lax.broadcasted_iota(jnp.int32, (1, tq), 1) // MOBA_BLOCK
    neg_inf = jnp.float32(-jnp.inf)
    for h in range(N_HEADS):
        in_head = (((row >= 32 * h) & (row < 32 * (h + 1)))
                   | ((row >= 128 + 32 * h) & (row < 128 + 32 * (h + 1))))
        qm = jnp.where(in_head, q, jnp.zeros_like(q))
        qcat_ref[:, h * tq:(h + 1) * tq] = qm
        g = jnp.where(blk < qblk, _dot(km, qm), neg_inf)
        sel = jnp.zeros((nb, tq), jnp.bool_)
        for _ in range(MOBA_TOPK):
            mx = jnp.max(g, axis=0, keepdims=True)
            cand = jnp.where((g == mx) & (mx > neg_inf), blk, nb)
            chosen = blk == jnp.min(cand, axis=0, keepdims=True)
            sel = sel | chosen
            g = jnp.where(chosen, neg_inf, g)
        bias_ref[h * nb:(h + 1) * nb, :] = jnp.where(sel, 0.0, MASKED_LOGIT)
    _init_state(m_ref, acc_ref)
    tk = sa_ref.shape[1]
    per = tk // SLAB
    causal = _own_block_mask(i, tq, tk)

    def scores_into(buf, j, h):
        _store_scores(buf, h, _dot(_key_rows(k_ref, j, tk), qcat_ref[:, h * tq:(h + 1) * tq]))

    def consume(buf, j, own, heads):
        for h in heads:
            parts, maxes = _slabs(buf, h, tq)
            gates = [bias_ref[pl.ds(h * nb + per * j + n, 1), :] for n in range(per)]
            if own:
                gates = [jnp.where(per * j + n < qblk, g, 0.0) for n, g in enumerate(gates)]
                parts, maxes = _causal_slabs(parts, causal), None
            _softmax_update(parts, _vblock(vT_ref, j, h), m_ref, acc_ref, h, gates, maxes)

    _run_key_blocks(i * tq // tk, k_ref.shape[0] // tk - 1, (sa_ref, mxa_ref), (sb_ref, mxb_ref), scores_into,
                    consume)
    _finalize(acc_ref, o_ref)


def _dilated_kernel(q_ref, k_ref, v_ref, o_ref, lse_ref, *, window, tq):
    i = pl.program_id(2)
    n = k_ref.shape[1]
    tiles = q_ref.shape[1] // tq
    lane = lax.broadcasted_iota(jnp.int32, (tq, GROUP_W), 1)
    for c in range(q_ref.shape[2] // GROUP_W):
        cls = slice(c * GROUP_W, (c + 1) * GROUP_W)
        for g in range(tiles):
            rows = slice(g * tq, (g + 1) * tq)
            q = q_ref[0, rows, cls]
            a = (i * tiles + g) * tq
            ks = jnp.clip(a - DIL_WINDOW_STEPS, 0, n - window)
            ks = pl.multiple_of(ks, DIL_WINDOW_STEPS)
            kw = k_ref[0, pl.ds(ks, window), cls]
            vw = v_ref[0, pl.ds(ks, window), cls]
            jq = a + lax.broadcasted_iota(jnp.int32, (tq, window), 0)
            jk = ks + lax.broadcasted_iota(jnp.int32, (tq, window), 1)
            band = (jq - jk >= 0) & (jq - jk <= DIL_WINDOW_STEPS)
            o = jnp.zeros((tq, GROUP_W), jnp.float32)
            lse = jnp.zeros((tq, GROUP_W), jnp.float32)
            for h in range(N_HEADS):
                in_head = (((lane >= 32 * h) & (lane < 32 * (h + 1)))
                           | ((lane >= 128 + 32 * h) & (lane < 128 + 32 * (h + 1))))
                qm = jnp.where(in_head, q, jnp.zeros_like(q))
                s = jnp.where(band, _dot_nt(qm, kw), -jnp.inf)
                m = jnp.max(s, axis=1, keepdims=True)
                p = jnp.exp2(s - m)
                l = jnp.sum(p, axis=1, keepdims=True)
                oh = _dot(p.astype(jnp.bfloat16), vw) / l
                out_lanes = (lane >= h * HEAD_DIM) & (lane < (h + 1) * HEAD_DIM)
                o = jnp.where(out_lanes, oh, o)
                lse = jnp.where(out_lanes, m + jnp.log2(l), lse)
            o_ref[0, rows, cls] = o
            lse_ref[0, rows, cls] = lse


def _out_ffn_kernel(x_ref, omoba_ref, ofox_ref, omla_ref, od1_ref, od4_ref, od16_ref, l1_ref, l4_ref, l16_ref,
                    wout_ref, gpost_ref, gpre_ref, wg_ref, wu_ref, wd_ref, gpf_ref, out_ref, acc_ref,
                    dscr_ref):
    bf16 = jnp.bfloat16
    tm = x_ref.shape[0]

    def token_order(ref, slot, dil):
        halves = GROUP_W // 128
        for rho in range(dil):
            for c in range(halves):
                lanes = slice(rho * GROUP_W + c * 128, rho * GROUP_W + (c + 1) * 128)
                dscr_ref[slot + c, pl.ds(rho, tm // dil, stride=dil), :] = ref[:, lanes]
        return jnp.concatenate([dscr_ref[slot + c] for c in range(halves)], axis=1)

    l1, od1 = l1_ref[...], od1_ref[...]
    l4, od4 = token_order(l4_ref, 0, 4), token_order(od4_ref, 2, 4)
    l16, od16 = token_order(l16_ref, 4, 16), token_order(od16_ref, 6, 16)
    m = jnp.maximum(jnp.maximum(l1, l4), l16)
    e1, e4, e16 = jnp.exp2(l1 - m), jnp.exp2(l4 - m), jnp.exp2(l16 - m)
    odil = (e1 * od1 + e4 * od4 + e16 * od16) / (e1 + e4 + e16)
    y = (_dot(omoba_ref[...], wout_ref[0]) + _dot(ofox_ref[...], wout_ref[1])
         + _dot(omla_ref[...], wout_ref[2]) + _dot(odil.astype(bf16), wout_ref[3]))
    x1 = x_ref[...] + y * _rms_scale(y, D_MODEL) * gpost_ref[...]
    hb = (x1 * _rms_scale(x1, D_MODEL) * gpre_ref[...]).astype(bf16)
    acc_ref[...] = jnp.zeros_like(acc_ref)
    for c in range(D_FF // FF_CHUNK):
        cols = slice(c * FF_CHUNK, (c + 1) * FF_CHUNK)
        g = _dot(hb, wg_ref[:, cols])
        u = _dot(hb, wu_ref[:, cols])
        f = (g * jax.nn.sigmoid(g) * u).astype(bf16)
        acc_ref[...] += _dot(f, wd_ref[cols, :])
    f = acc_ref[...]
    out_ref[...] = x1 + f * _rms_scale(f, D_MODEL) * gpf_ref[...]


def _rope_tables(positions):
    t = positions.reshape(-1).astype(jnp.float32)

    def tab(dim):
        inv = ROPE_THETA ** (-jnp.arange(0, dim, 2, dtype=jnp.float32) / dim)
        ang = t[:, None] * inv
        return jnp.cos(ang), jnp.sin(ang)

    c64, s64 = tab(HEAD_DIM)
    c16, s16 = tab(MLA_ROPE)
    ct64, st64 = jnp.tile(c64, (1, 4)), jnp.tile(s64, (1, 4))
    pad = jnp.zeros((t.shape[0], 128 - MLA_ROPE), jnp.float32)
    ct16 = jnp.concatenate([c16, c16, pad], axis=1)
    st16 = jnp.concatenate([s16, s16, pad], axis=1)
    cT16, sT16 = jnp.tile(c16, (1, 4)).T, jnp.tile(s16, (1, 4)).T
    return ct64, st64, ct64.T, st64.T, ct16, st16, cT16, sT16


_HALF_PERM = np.array([h * 64 + half * 32 + j for half in (0, 1) for h in range(4) for j in range(32)])
_QROPE_ROWS = np.array([h * 96 + 64 + half * 16 + j for half in (0, 1) for h in range(4) for j in range(16)])
_QNOPE_ROWS = np.array([h * 96 + j for h in range(4) for j in range(64)])
_KNOPE_COLS = np.array([h * 128 + j for h in range(4) for j in range(64)])
_VMLA_COLS = np.array([h * 128 + 64 + j for h in range(4) for j in range(64)])


def _prep_layer_weights(w_in, b_forget, g_mla_q, w_mla_q_up, g_mla_kv, w_mla_kv_up, w_out, w_gate, w_up, w_down):
    bf16 = jnp.bfloat16
    depth = w_in.shape[0]
    sl = lambda a, b: w_in[:, :, a:b]
    scale = HEAD_DIM ** -0.5 * LOG2E
    moba_q, moba_k, moba_v = sl(0, 256)[..., _HALF_PERM] * scale, sl(256, 512)[..., _HALF_PERM], sl(512, 768)
    fox_q, fox_k, fox_v = sl(768, 1024) * scale, sl(1024, 1280), sl(1280, 1536)
    fg, cq, ckv, kr = sl(1536, 1540), sl(1540, 1732), sl(1732, 1860), sl(1860, 1892)
    dil_q, dil_k, dil_v = sl(1892, 2148)[..., _HALF_PERM] * scale, sl(2148, 2404)[..., _HALF_PERM], sl(2404, 2660)
    kr_rot = jnp.concatenate([-kr[..., 16:], kr[..., :16]], axis=-1)
    z = lambda n: jnp.zeros((depth, D_MODEL, n), w_in.dtype)
    wtok = jnp.concatenate([moba_k, fox_k, dil_q, dil_k, dil_v, cq, z(256 - MLA_Q_RANK), ckv,
                            kr, fg, z(128 - 36), kr_rot, z(96)], axis=-1).astype(bf16)
    wtr = jnp.swapaxes(jnp.concatenate([moba_q, moba_v, fox_q, fox_v], axis=-1), 1, 2).astype(bf16)
    bfg = jnp.zeros((depth, 1, 128), jnp.float32).at[:, 0, _FG_LANE:_FG_LANE + N_HEADS].set(b_forget)
    gq = jnp.pad(g_mla_q, ((0, 0), (0, 256 - MLA_Q_RANK)))[:, None, :]
    wq_rows = jnp.swapaxes(w_mla_q_up, 1, 2)
    wqT = jnp.concatenate([wq_rows[:, _QNOPE_ROWS], wq_rows[:, _QROPE_ROWS]], axis=1)
    wqT = jnp.pad(wqT, ((0, 0), (0, 0), (0, 256 - MLA_Q_RANK))).astype(bf16)
    gkv = g_mla_kv[:, None, :]
    wkn = w_mla_kv_up[:, :, _KNOPE_COLS].astype(bf16)
    wvT = jnp.swapaxes(w_mla_kv_up[:, :, _VMLA_COLS], 1, 2).astype(bf16)
    wout = w_out.reshape(depth, 4, GROUP_W, D_MODEL).astype(bf16)
    wg, wu, wd = w_gate.astype(bf16), w_up.astype(bf16), w_down.astype(bf16)
    return wtok, wtr, bfg, gq, wqT, gkv, wkn, wvT, wout, wg, wu, wd


def _const_spec(shape):
    return pl.BlockSpec(shape, lambda *_: (0,) * len(shape))


def _params(*sem):
    return pltpu.CompilerParams(dimension_semantics=sem, vmem_limit_bytes=VMEM_LIMIT)


def _in_proj(x2, g, wtok, wtr, tri, tables, bfg, gq, wqT, gkv, wkn, wvT, seq):
    t = x2.shape[0]
    tm = ROW_TILE
    nt = t // tm
    bf16, f32 = jnp.bfloat16, jnp.float32
    ct64, st64, cT64, sT64, ct16, st16, cT16, sT16 = tables
    tok_spec = lambda w: pl.BlockSpec((tm, w), lambda i: (i, 0))
    tr_spec = lambda r: pl.BlockSpec((r, tm), lambda i: (0, i))
    blk3 = lambda n, r, c: pl.BlockSpec((n, r, c), lambda i: (i, 0, 0))
    mb = tm // MOBA_BLOCK
    ab = tm // KEY_BLOCK
    vrows = N_HEADS * V_ROWS
    vt_shape = jax.ShapeDtypeStruct((t // KEY_BLOCK, vrows, KEY_BLOCK), bf16)
    in_specs = [tok_spec(D_MODEL), _const_spec((1, D_MODEL)), _const_spec(wtok.shape), _const_spec(wtr.shape),
                _const_spec(tri.shape), tok_spec(128), tok_spec(128), tr_spec(128), tr_spec(128),
                tok_spec(128), tok_spec(128), tr_spec(64), tr_spec(64), _const_spec((1, 128)),
                _const_spec((1, 256)), _const_spec(wqT.shape), _const_spec((1, 128)), _const_spec(wkn.shape),
                _const_spec(wvT.shape)]
    out_shape = [
        jax.ShapeDtypeStruct((t, GROUP_W), bf16),
        jax.ShapeDtypeStruct((nt, mb, GROUP_W), f32),
        jax.ShapeDtypeStruct((GROUP_W, t), bf16),
        vt_shape,
        jax.ShapeDtypeStruct((t, 512), bf16),
        jax.ShapeDtypeStruct((GROUP_W, t), bf16),
        vt_shape,
        jax.ShapeDtypeStruct((nt, 1, 128), f32),
        *[jax.ShapeDtypeStruct((t // d, d * GROUP_W), bf16) for d in DILATIONS] * 3,
        jax.ShapeDtypeStruct((t, 512), bf16),
        jax.ShapeDtypeStruct((384, t), bf16),
        vt_shape,
    ]
    vt_spec = blk3(ab, vrows, KEY_BLOCK)
    out_specs = [tok_spec(GROUP_W), blk3(1, mb, GROUP_W), tr_spec(GROUP_W), vt_spec,
                 tok_spec(512), tr_spec(GROUP_W), vt_spec, blk3(1, 1, 128),
                 *[pl.BlockSpec((tm // d, d * GROUP_W), lambda i: (i, 0)) for d in DILATIONS] * 3,
                 tok_spec(512), tr_spec(384), vt_spec]
    return pl.pallas_call(
        functools.partial(_in_proj_kernel, tiles_per_seq=seq // tm),
        grid=(nt,), in_specs=in_specs, out_specs=out_specs, out_shape=out_shape,
        scratch_shapes=[pltpu.VMEM((1, 128), f32), pltpu.VMEM((3 * GROUP_W // 128, tm, 128), f32)],
        compiler_params=_params("arbitrary"), name="in_proj",
    )(x2, g, wtok, wtr, tri, ct64, st64, cT64, sT64, ct16, st16, cT16, sT16, bfg, gq, wqT, gkv, wkn, wvT)


def _attention_scratch(tq, tk=KEY_BLOCK):
    return [pltpu.VMEM((GROUP_W, N_HEADS * tq), jnp.bfloat16),
            pltpu.VMEM((8, tq), jnp.float32),
            pltpu.VMEM((N_HEADS, V_ROWS, tq), jnp.float32),
            pltpu.VMEM((N_HEADS, tk, tq), jnp.float32),
            pltpu.VMEM((N_HEADS, tk, tq), jnp.float32),
            pltpu.VMEM((N_HEADS, 8, tq), jnp.float32),
            pltpu.VMEM((N_HEADS, 8, tq), jnp.float32)]


def _dense_attention(body, name, qT, k, vT, extra, batch, seq):
    tq = ATT_TILE
    nq = seq // tq
    t = batch * seq
    in_specs = [pl.BlockSpec((qT.shape[0], tq), lambda b, i: (0, b * nq + i)),
                pl.BlockSpec((seq, k.shape[1]), lambda b, i: (b, 0)),
                pl.BlockSpec((seq // KEY_BLOCK, N_HEADS * V_ROWS, KEY_BLOCK), lambda b, i: (b, 0, 0))]
    in_specs += [pl.BlockSpec((seq // KEY_BLOCK, 1, 128), lambda b, i: (b, 0, 0)) for _ in extra]
    return pl.pallas_call(
        body, grid=(batch, nq), in_specs=in_specs,
        out_specs=pl.BlockSpec((tq, GROUP_W), lambda b, i: (b * nq + i, 0)),
        out_shape=jax.ShapeDtypeStruct((t, GROUP_W), jnp.bfloat16),
        scratch_shapes=_attention_scratch(tq),
        compiler_params=_params("arbitrary", "arbitrary"), name=name,
    )(qT, k, vT, *extra)


def _moba_attention(qT, k, vT, kmean, batch, seq):
    tq = ATT_TILE
    nq = seq // tq
    nb = seq // MOBA_BLOCK
    t = batch * seq
    return pl.pallas_call(
        _moba_kernel, grid=(batch, nq),
        in_specs=[pl.BlockSpec((GROUP_W, tq), lambda b, i: (0, b * nq + i)),
                  pl.BlockSpec((seq, GROUP_W), lambda b, i: (b, 0)),
                  pl.BlockSpec((seq // KEY_BLOCK, N_HEADS * V_ROWS, KEY_BLOCK), lambda b, i: (b, 0, 0)),
                  pl.BlockSpec((1, nb, GROUP_W), lambda b, i: (b, 0, 0))],
        out_specs=pl.BlockSpec((tq, GROUP_W), lambda b, i: (b * nq + i, 0)),
        out_shape=jax.ShapeDtypeStruct((t, GROUP_W), jnp.bfloat16),
        scratch_shapes=_attention_scratch(tq) + [pltpu.VMEM((N_HEADS * nb, tq), jnp.float32)],
        compiler_params=_params("arbitrary", "arbitrary"), name="moba_attention",
    )(qT, k, vT, kmean)


def _dilated_attention(q, k, v, batch, seq, dil):
    n = seq // dil
    tq = n if n <= 2 * DIL_TILE else DIL_TILE
    window = min(tq + DIL_WINDOW_STEPS, n)
    view = lambda a: a.reshape(batch, n, dil * GROUP_W)
    tiles = min(DIL_PROBLEMS, n // tq)
    classes = min(DIL_PROBLEMS // tiles, dil)
    qspec = pl.BlockSpec((1, tiles * tq, classes * GROUP_W), lambda b, r, i: (b, i, r))
    kspec = pl.BlockSpec((1, n, classes * GROUP_W), lambda b, r, i: (b, 0, r))
    shape = jax.ShapeDtypeStruct((batch, n, dil * GROUP_W), jnp.float32)
    o, lse = pl.pallas_call(
        functools.partial(_dilated_kernel, window=window, tq=tq),
        grid=(batch, dil // classes, n // (tiles * tq)),
        in_specs=[qspec, kspec, kspec], out_specs=[qspec, qspec], out_shape=[shape, shape],
        compiler_params=_params("arbitrary", "arbitrary", "arbitrary"), name=f"dilated_attention_d{dil}",
    )(view(q), view(k), view(v))
    return o.reshape(batch * n, dil * GROUP_W), lse.reshape(batch * n, dil * GROUP_W)


def _out_ffn(x2, omoba, ofox, omla, dil_outs, wout, gpost, gpre, wg, wu, wd, gpf):
    t = x2.shape[0]
    tm = ROW_TILE
    row = lambda w: pl.BlockSpec((tm, w), lambda i: (i, 0))
    (od1, l1), (od4, l4), (od16, l16) = dil_outs
    dil_specs = [pl.BlockSpec((tm // d, d * GROUP_W), lambda i: (i, 0)) for d in DILATIONS]
    in_specs = ([row(D_MODEL)] + [row(GROUP_W)] * 3 + dil_specs * 2
                + [_const_spec(wout.shape), _const_spec((1, D_MODEL)), _const_spec((1, D_MODEL)),
                   _const_spec(wg.shape), _const_spec(wu.shape), _const_spec(wd.shape), _const_spec((1, D_MODEL))])
    return pl.pallas_call(
        _out_ffn_kernel, grid=(t // tm,), in_specs=in_specs, out_specs=row(D_MODEL),
        out_shape=jax.ShapeDtypeStruct((t, D_MODEL), jnp.float32),
        scratch_shapes=[pltpu.VMEM((tm, D_MODEL), jnp.float32), pltpu.VMEM((4 * GROUP_W // 128, tm, 128), jnp.float32)],
        compiler_params=_params("arbitrary"), name="out_ffn",
    )(x2, omoba, ofox, omla, od1, od4, od16, l1, l4, l16, wout, gpost, gpre, wg, wu, wd, gpf)


def kernel(x, positions, w_in, b_forget, g_mla_q, w_mla_q_up, g_mla_kv, w_mla_kv_up, w_out, g_pre_mix, g_post_mix, w_gate, w_up, w_down, g_pre_ffn, g_post_ffn):
    batch, seq, _ = x.shape
    depth = w_in.shape[0]
    assert seq % ROW_TILE == 0 and seq % (DILATIONS[-1] * DIL_WINDOW_STEPS) == 0
    assert ROW_TILE == KEY_BLOCK
    tables = _rope_tables(positions)
    wtok, wtr, bfg, gq, wqT, gkv, wkn, wvT, wout, wg, wu, wd = _prep_layer_weights(
        w_in, b_forget, g_mla_q, w_mla_q_up, g_mla_kv, w_mla_kv_up, w_out, w_gate, w_up, w_down)
    tri = jnp.tril(jnp.ones((ROW_TILE, ROW_TILE), jnp.bfloat16))
    x2 = x.reshape(batch * seq, D_MODEL)
    for l in range(depth):
        (k_moba, kmean, qT_moba, vT_moba, k_fox, qT_fox, vT_fox, cbase, *dil_qkv,
         k_mla, qT_mla, vT_mla) = _in_proj(x2, g_pre_mix[l][None], wtok[l], wtr[l], tri, tables, bfg[l], gq[l],
                                          wqT[l], gkv[l], wkn[l], wvT[l], seq)
        nd = len(DILATIONS)
        o_moba = _moba_attention(qT_moba, k_moba, vT_moba, kmean.reshape(batch, seq // MOBA_BLOCK, GROUP_W),
                                 batch, seq)
        o_fox = _dense_attention(_fox_kernel, "fox_attention", qT_fox, k_fox, vT_fox, [cbase], batch, seq)
        o_mla = _dense_attention(_mla_kernel, "mla_attention", qT_mla, k_mla, vT_mla, [], batch, seq)
        dil_outs = [_dilated_attention(dil_qkv[n], dil_qkv[nd + n], dil_qkv[2 * nd + n], batch, seq, d)
                    for n, d in enumerate(DILATIONS)]
        x2 = _out_ffn(x2, o_moba, o_fox, o_mla, dil_outs, wout[l], g_post_mix[l][None], g_pre_ffn[l][None],
                      wg[l], wu[l], wd[l], g_post_ffn[l][None])
    return x2.reshape(batch, seq, D_MODEL)
```

```python
import functools

import numpy as np
import jax
import jax.numpy as jnp
from jax import lax
from jax.experimental import pallas as pl
from jax.experimental.pallas import tpu as pltpu

D_MODEL = 1024
HEAD_DIM = 64
N_HEADS = 4
GROUP_W = N_HEADS * HEAD_DIM
MOBA_BLOCK = 256
MOBA_TOPK = 3
MLA_Q_RANK = 192
MLA_KV_RANK = 128
MLA_NOPE = 64
MLA_ROPE = 32
DIL_WINDOW_STEPS = 128
DILATIONS = (1, 4, 16)
ROPE_THETA = 10000.0
RMS_EPS = 1e-6
D_FF = 2816
FF_CHUNK = 256
V_ROWS = 80
LOG2E = 1.4426950408889634
MASKED_LOGIT = -1e30

ROW_TILE = 512
ATT_TILE = 512
KEY_BLOCK = 512
SLAB = MOBA_BLOCK
DIL_TILE = 256
DIL_PROBLEMS = 8
VMEM_LIMIT = 56 * 1024 * 1024

_C_KMOBA, _C_KFOX, _C_QDIL, _C_KDIL, _C_VDIL, _C_CQ = 0, 256, 512, 768, 1024, 1280
_C_CKV, _C_X, _C_Y, _N_TOK = 1536, 1664, 1792, 1920
_FG_LANE = 32

_NT = (((1,), (1,)), ((), ()))


def _dot(a, b):
    return jnp.dot(a, b, preferred_element_type=jnp.float32)


def _dot_nt(a, b):
    return lax.dot_general(a, b, _NT, preferred_element_type=jnp.float32)


def _rms_scale(v, n):
    return lax.rsqrt(jnp.sum(v * v, axis=-1, keepdims=True) * (1.0 / n) + RMS_EPS)


def _in_proj_kernel(x_ref, g_ref, wtok_ref, wtr_ref, tri_ref, ct64_ref, st64_ref, cT64_ref, sT64_ref,
                    ct16_ref, st16_ref, cT16_ref, sT16_ref, bfg_ref, gq_ref, wqT_ref, gkv_ref, wkn_ref, wvT_ref,
                    kmoba_ref, kmean_ref, qTmoba_ref, vTmoba_ref, kfox_ref, qTfox_ref, vTfox_ref, cbase_ref,
                    q1_ref, q4_ref, q16_ref, k1_ref, k4_ref, k16_ref, v1_ref, v4_ref, v16_ref,
                    kmla_ref, qTmla_ref, vTmla_ref, carry_ref, dscr_ref, *, tiles_per_seq):
    bf16 = jnp.bfloat16
    qdil_refs, kdil_refs, vdil_refs = (q1_ref, q4_ref, q16_ref), (k1_ref, k4_ref, k16_ref), (v1_ref, v4_ref, v16_ref)
    x = x_ref[...]
    hb = (x * _rms_scale(x, D_MODEL) * g_ref[...]).astype(bf16)
    tm = x.shape[0]

    def tok(c0, w):
        return _dot(hb, wtok_ref[:, c0:c0 + w])

    def rope_tok(z):
        c, s = ct64_ref[...], st64_ref[...]
        x1, x2 = z[:, :128], z[:, 128:]
        return jnp.concatenate([x1 * c - x2 * s, x2 * c + x1 * s], axis=1)

    def with_ones(vT):
        ones = jnp.ones((V_ROWS - HEAD_DIM, tm), bf16)
        parts = []
        for h in range(N_HEADS):
            parts += [vT[h * HEAD_DIM:(h + 1) * HEAD_DIM].astype(bf16), ones]
        return jnp.concatenate(parts, axis=0)

    def rope_tr(zT):
        c, s = cT64_ref[...], sT64_ref[...]
        x1, x2 = zT[:128], zT[128:]
        return jnp.concatenate([x1 * c - x2 * s, x2 * c + x1 * s], axis=0)

    k_moba = rope_tok(tok(_C_KMOBA, GROUP_W))
    kmoba_ref[...] = k_moba.astype(bf16)
    for blk in range(tm // MOBA_BLOCK):
        kmean_ref[0, blk:blk + 1, :] = jnp.mean(k_moba[blk * MOBA_BLOCK:(blk + 1) * MOBA_BLOCK], axis=0, keepdims=True)
    qTmoba_ref[...] = rope_tr(_dot_nt(wtr_ref[0:256, :], hb)).astype(bf16)
    def store_value_blocks(ref, vT):
        for blk in range(tm // KEY_BLOCK):
            ref[blk] = vT[:, blk * KEY_BLOCK:(blk + 1) * KEY_BLOCK]

    store_value_blocks(vTmoba_ref, with_ones(_dot_nt(wtr_ref[256:512, :], hb)))

    qTfox_ref[...] = _dot_nt(wtr_ref[512:768, :], hb).astype(bf16)
    store_value_blocks(vTfox_ref, with_ones(_dot_nt(wtr_ref[768:1024, :], hb)))

    def emit_dilated(refs, slot, z):
        refs[0][...] = z.astype(bf16)
        for c in range(GROUP_W // 128):
            dscr_ref[slot + c] = z[:, c * 128:(c + 1) * 128]
        for ref, dil in zip(refs[1:], DILATIONS[1:]):
            for rho in range(dil):
                for c in range(GROUP_W // 128):
                    rows = dscr_ref[slot + c, pl.ds(rho, tm // dil, stride=dil), :]
                    ref[:, rho * GROUP_W + c * 128:rho * GROUP_W + (c + 1) * 128] = rows.astype(bf16)

    emit_dilated(qdil_refs, 0, rope_tok(tok(_C_QDIL, GROUP_W)))
    emit_dilated(kdil_refs, 2, rope_tok(tok(_C_KDIL, GROUP_W)))
    emit_dilated(vdil_refs, 4, tok(_C_VDIL, GROUP_W))

    xy = tok(_C_X, 256)
    xblk = xy[:, :128]
    yblk = xy[:, 128:]
    fg = xblk + bfg_ref[...]
    logf = jnp.minimum(fg, 0.0) - jnp.log1p(jnp.exp(-jnp.abs(fg)))
    lane = lax.broadcasted_iota(jnp.int32, logf.shape, 1)
    is_gate = (lane >= _FG_LANE) & (lane < _FG_LANE + N_HEADS)

    def three_pieces(v):
        v = jnp.where(is_gate, v, 0.0)
        p1 = v.astype(bf16).astype(jnp.float32)
        p2 = (v - p1).astype(bf16).astype(jnp.float32)
        p3 = (v - p1 - p2).astype(bf16).astype(jnp.float32)
        return p1 + pltpu.roll(p2, 8, 1) + pltpu.roll(p3, 16, 1)

    @pl.when(pl.program_id(0) % tiles_per_seq == 0)
    def _():
        carry_ref[...] = jnp.zeros_like(carry_ref)

    part = _dot(tri_ref[...], three_pieces(logf).astype(bf16))
    cum = part + pltpu.roll(part, 128 - 8, 1) + pltpu.roll(part, 128 - 16, 1) + carry_ref[...]
    carry_ref[...] = cum[tm - 1:tm, :]
    cbase_ref[0] = cum[0:1, :]

    dcols = three_pieces((cum[0:1, :] - cum) * LOG2E)
    kf = tok(_C_KFOX, GROUP_W)
    kfox_ref[...] = jnp.concatenate([kf[:, :128], dcols, kf[:, 128:], dcols], axis=1).astype(bf16)

    zcq = tok(_C_CQ, 256)
    cq = (zcq * _rms_scale(zcq, MLA_Q_RANK) * gq_ref[...]).astype(bf16)
    qcT = _dot_nt(wqT_ref[...], cq)
    c16, s16 = cT16_ref[...], sT16_ref[...]
    r1h, r2h = qcT[256:320], qcT[320:384]
    qT = jnp.concatenate([qcT[0:256], r1h * c16 - r2h * s16, r2h * c16 + r1h * s16], axis=0)
    qTmla_ref[...] = (qT * ((MLA_NOPE + MLA_ROPE) ** -0.5 * LOG2E)).astype(bf16)

    zckv = tok(_C_CKV, MLA_KV_RANK)
    ckv = (zckv * _rms_scale(zckv, MLA_KV_RANK) * gkv_ref[...]).astype(bf16)
    kn = _dot(ckv, wkn_ref[...])
    krope = xblk * ct16_ref[...] + yblk * st16_ref[...]
    kmla_ref[...] = jnp.concatenate([kn[:, :128], krope, kn[:, 128:], krope], axis=1).astype(bf16)
    store_value_blocks(vTmla_ref, with_ones(_dot_nt(wvT_ref[...], ckv)))


def _softmax_update(parts, vb, m_ref, acc_ref, h, offsets=None, maxes=None):
    if offsets is None:
        offsets = [None] * len(parts)
    if maxes is None:
        maxes = [jnp.max(s, axis=0, keepdims=True) for s in parts]
    m_old = m_ref[h:h + 1, :]
    blk_max = None
    for mx, off in zip(maxes, offsets):
        mx = mx if off is None else mx + off
        blk_max = mx if blk_max is None else jnp.maximum(blk_max, mx)
    m_new = jnp.maximum(m_old, blk_max)
    ps = []
    for s, off in zip(parts, offsets):
        shift = m_new if off is None else m_new - off
        ps.append(jnp.exp2(s - shift).astype(jnp.bfloat16))
    p = ps[0] if len(ps) == 1 else jnp.concatenate(ps, axis=0)
    acc_ref[h] = jnp.exp2(m_old - m_new) * acc_ref[h] + _dot(vb, p)
    m_ref[h:h + 1, :] = m_new


def _store_scores(buf, first_head, s):
    s_ref, mx_ref = buf
    tq = s_ref.shape[2]
    for k in range(s.shape[1] // tq):
        sh = s[:, k * tq:(k + 1) * tq]
        s_ref[first_head + k] = sh
        for n in range(sh.shape[0] // SLAB):
            mx_ref[first_head + k, n:n + 1, :] = jnp.max(sh[n * SLAB:(n + 1) * SLAB], axis=0, keepdims=True)


def _slabs(buf, h, tq):
    s_ref, mx_ref = buf
    n_slabs = s_ref.shape[1] // SLAB
    return ([s_ref[h, n * SLAB:(n + 1) * SLAB, :] for n in range(n_slabs)],
            [mx_ref[h, n:n + 1, :] for n in range(n_slabs)])


def _diagonal_masks(tq, tk):
    key = lax.broadcasted_iota(jnp.int32, (tk, tq), 0)
    qry = lax.broadcasted_iota(jnp.int32, (tk, tq), 1)
    return [key + r * tk <= qry for r in range(tq // tk)]


def _init_state(m_ref, acc_ref):
    m_ref[...] = jnp.full(m_ref.shape, -jnp.inf, jnp.float32)
    acc_ref[...] = jnp.zeros(acc_ref.shape, jnp.float32)


def _finalize(acc_ref, o_ref):
    outs = [acc_ref[h, 0:HEAD_DIM, :] / acc_ref[h, HEAD_DIM:HEAD_DIM + 1, :] for h in range(N_HEADS)]
    o_ref[...] = jnp.concatenate(outs, axis=0).T.astype(o_ref.dtype)


def _vblock(vT_ref, j, h):
    return vT_ref[j, h * V_ROWS:(h + 1) * V_ROWS, :]


def _run_key_blocks(first, n_diag, last, buf_a, buf_b, scores_into, consume):
    def step(buf_next, j_next, buf_cur, j_cur, diag):
        for h in range(N_HEADS):
            scores_into(buf_next, j_next, h)
            consume(buf_cur, j_cur, diag, (h,))

    bufs = (buf_a, buf_b)
    for h in range(N_HEADS):
        scores_into(buf_a, first, h)
    for r in range(n_diag):
        step(bufs[(r + 1) % 2], first + r + 1 if r + 1 < n_diag else 0, bufs[r % 2], first + r, r)
    cur, other = bufs[n_diag % 2], bufs[(n_diag + 1) % 2]

    def pair(p, carry):
        j0 = 2 * p
        step(other, j0 + 1, cur, j0, None)
        step(cur, jnp.minimum(j0 + 2, last), other, j0 + 1, None)
        return carry

    lax.fori_loop(0, first // 2, pair, 0)

    @pl.when(first % 2 == 1)
    def _():
        consume(cur, first - 1, None, range(N_HEADS))


def _key_rows(k_ref, j, tk):
    return k_ref[pl.ds(pl.multiple_of(j * tk, tk), tk), :]


def _causal_slabs(parts, causal):
    return [jnp.where(causal[n * SLAB:(n + 1) * SLAB], s, -jnp.inf) for n, s in enumerate(parts)]


def _two_tile_scores(buf, k_ref, qcat_ref, j, h, tk, tq):
    kb = k_ref[pl.ds(pl.multiple_of(j * tk, tk), tk), 256 * (h // 2):256 * (h // 2 + 1)]
    _store_scores(buf, h, _dot(kb, qcat_ref[:, h * tq:(h + 1) * tq]))


def _fox_kernel(qT_ref, k_ref, vT_ref, cbase_ref, o_ref, qcat_ref, m_ref, acc_ref, sa_ref, sb_ref, mxa_ref, mxb_ref):
    i = pl.program_id(1)
    tq = qT_ref.shape[1]
    zeros64 = jnp.zeros((HEAD_DIM, tq), jnp.bfloat16)
    row = lax.broadcasted_iota(jnp.int32, (128, tq), 0)
    for h in range(N_HEADS):
        qh = qT_ref[h * HEAD_DIM:(h + 1) * HEAD_DIM, :]
        pick = (row == _FG_LANE + h) | (row == _FG_LANE + 8 + h) | (row == _FG_LANE + 16 + h)
        rows = ([qh, zeros64] if h % 2 == 0 else [zeros64, qh]) + [pick.astype(jnp.bfloat16)]
        qcat_ref[:, h * tq:(h + 1) * tq] = jnp.concatenate(rows, axis=0)
    _init_state(m_ref, acc_ref)
    tk = sa_ref.shape[1]
    n_diag = tq // tk
    first = i * n_diag
    causal = _diagonal_masks(tq, tk)

    def scores_into(buf, j, h):
        _two_tile_scores(buf, k_ref, qcat_ref, j, h, tk, tq)

    def consume(buf, j, diag, heads):
        offs = (cbase_ref[first] - cbase_ref[j]) * LOG2E
        for h in heads:
            parts, maxes = _slabs(buf, h, tq)
            if diag is not None:
                parts, maxes = _causal_slabs(parts, causal[diag]), None
            off = offs[:, _FG_LANE + h:_FG_LANE + h + 1]
            _softmax_update(parts, _vblock(vT_ref, j, h), m_ref, acc_ref, h, [off] * len(parts), maxes)

    _run_key_blocks(first, n_diag, k_ref.shape[0] // tk - 1, (sa_ref, mxa_ref), (sb_ref, mxb_ref), scores_into,
                    consume)
    _finalize(acc_ref, o_ref)


def _mla_kernel(qT_ref, k_ref, vT_ref, o_ref, qcat_ref, m_ref, acc_ref, sa_ref, sb_ref, mxa_ref, mxb_ref):
    i = pl.program_id(1)
    tq = qT_ref.shape[1]
    zeros64 = jnp.zeros((MLA_NOPE, tq), jnp.bfloat16)
    zeros96 = jnp.zeros((256 - 128 - MLA_ROPE, tq), jnp.bfloat16)
    for h in range(N_HEADS):
        qn = qT_ref[h * MLA_NOPE:(h + 1) * MLA_NOPE, :]
        qr1 = qT_ref[256 + 16 * h:256 + 16 * (h + 1), :]
        qr2 = qT_ref[320 + 16 * h:320 + 16 * (h + 1), :]
        nope = [qn, zeros64] if h % 2 == 0 else [zeros64, qn]
        qcat_ref[:, h * tq:(h + 1) * tq] = jnp.concatenate(nope + [qr1, qr2, zeros96], axis=0)
    _init_state(m_ref, acc_ref)
    tk = sa_ref.shape[1]
    n_diag = tq // tk
    causal = _diagonal_masks(tq, tk)

    def scores_into(buf, j, h):
        _two_tile_scores(buf, k_ref, qcat_ref, j, h, tk, tq)

    def consume(buf, j, diag, heads):
        for h in heads:
            parts, maxes = _slabs(buf, h, tq)
            if diag is not None:
                parts, maxes = _causal_slabs(parts, causal[diag]), None
            _softmax_update(parts, _vblock(vT_ref, j, h), m_ref, acc_ref, h, None, maxes)

    _run_key_blocks(i * n_diag, n_diag, k_ref.shape[0] // tk - 1, (sa_ref, mxa_ref), (sb_ref, mxb_ref),
                    scores_into, consume)
    _finalize(acc_ref, o_ref)


def _moba_kernel(qT_ref, k_ref, vT_ref, km_ref, o_ref, qcat_ref, m_ref, acc_ref, sa_ref, sb_ref, mxa_ref, mxb_ref,
                 bias_ref):
    i = pl.program_id(1)
    tq = qT_ref.shape[1]
    nb = km_ref.shape[1]
    q = qT_ref[...]
    row = lax.broadcasted_iota(jnp.int32, q.shape, 0)
    km = km_ref[0].astype(jnp.bfloat16)
    blk = lax.broadcasted_iota(jnp.int32, (nb, tq), 0)
    qblk = i * (tq // MOBA_BLOCK) + lax.broadcasted_iota(jnp.int32, (1, tq), 1) // MOBA_BLOCK
    neg_inf = jnp.float32(-jnp.inf)
    for h in range(N_HEADS):
        in_head = (((row >= 32 * h) & (row < 32 * (h + 1)))
                   | ((row >= 128 + 32 * h) & (row < 128 + 32 * (h + 1))))
        qm = jnp.where(in_head, q, jnp.zeros_like(q))
        qcat_ref[:, h * tq:(h + 1) * tq] = qm
        g = jnp.where(blk < qblk, _dot(km, qm), neg_inf)
        sel = jnp.zeros((nb, tq), jnp.bool_)
        for _ in range(MOBA_TOPK):
            mx = jnp.max(g, axis=0, keepdims=True)
            cand = jnp.where((g == mx) & (mx > neg_inf), blk, nb)
            chosen = blk == jnp.min(cand, axis=0, keepdims=True)
            sel = sel | chosen
            g = jnp.where(chosen, neg_inf, g)
        bias_ref[h * nb:(h + 1) * nb, :] = jnp.where(sel, 0.0, MASKED_LOGIT)
    _init_state(m_ref, acc_ref)
    tk = sa_ref.shape[1]
    per = tk // SLAB
    n_diag = tq // tk
    causal = _diagonal_masks(tq, tk)

    def scores_into(buf, j, h):
        _store_scores(buf, h, _dot(_key_rows(k_ref, j, tk), qcat_ref[:, h * tq:(h + 1) * tq]))

    def consume(buf, j, diag, heads):
        for h in heads:
            parts, maxes = _slabs(buf, h, tq)
            gates = [bias_ref[pl.ds(h * nb + per * j + n, 1), :] for n in range(per)]
            if diag is not None:
                gates = [jnp.where(per * j + n < qblk, g, 0.0) for n, g in enumerate(gates)]
                parts, maxes = _causal_slabs(parts, causal[diag]), None
            _softmax_update(parts, _vblock(vT_ref, j, h), m_ref, acc_ref, h, gates, maxes)

    _run_key_blocks(i * n_diag, n_diag, k_ref.shape[0] // tk - 1, (sa_ref, mxa_ref), (sb_ref, mxb_ref),
                    scores_into, consume)
    _finalize(acc_ref, o_ref)


def _dilated_kernel(q_ref, k_ref, v_ref, o_ref, lse_ref, *, window, tq):
    i = pl.program_id(2)
    n = k_ref.shape[1]
    tiles = q_ref.shape[1] // tq
    lane = lax.broadcasted_iota(jnp.int32, (tq, GROUP_W), 1)
    for c in range(q_ref.shape[2] // GROUP_W):
        cls = slice(c * GROUP_W, (c + 1) * GROUP_W)
        for g in range(tiles):
            rows = slice(g * tq, (g + 1) * tq)
            q = q_ref[0, rows, cls]
            a = (i * tiles + g) * tq
            ks = jnp.clip(a - DIL_WINDOW_STEPS, 0, n - window)
            ks = pl.multiple_of(ks, DIL_WINDOW_STEPS)
            kw = k_ref[0, pl.ds(ks, window), cls]
            vw = v_ref[0, pl.ds(ks, window), cls]
            jq = a + lax.broadcasted_iota(jnp.int32, (tq, window), 0)
            jk = ks + lax.broadcasted_iota(jnp.int32, (tq, window), 1)
            band = (jq - jk >= 0) & (jq - jk <= DIL_WINDOW_STEPS)
            o = jnp.zeros((tq, GROUP_W), jnp.float32)
            lse = jnp.zeros((tq, GROUP_W), jnp.float32)
            for h in range(N_HEADS):
                in_head = (((lane >= 32 * h) & (lane < 32 * (h + 1)))
                           | ((lane >= 128 + 32 * h) & (lane < 128 + 32 * (h + 1))))
                qm = jnp.where(in_head, q, jnp.zeros_like(q))
                s = jnp.where(band, _dot_nt(qm, kw), -jnp.inf)
                m = jnp.max(s, axis=1, keepdims=True)
                p = jnp.exp2(s - m)
                l = jnp.sum(p, axis=1, keepdims=True)
                oh = _dot(p.astype(jnp.bfloat16), vw) / l
                out_lanes = (lane >= h * HEAD_DIM) & (lane < (h + 1) * HEAD_DIM)
                o = jnp.where(out_lanes, oh, o)
                lse = jnp.where(out_lanes, m + jnp.log2(l), lse)
            o_ref[0, rows, cls] = o
            lse_ref[0, rows, cls] = lse


def _out_ffn_kernel(x_ref, omoba_ref, ofox_ref, omla_ref, od1_ref, od4_ref, od16_ref, l1_ref, l4_ref, l16_ref,
                    wout_ref, gpost_ref, gpre_ref, wg_ref, wu_ref, wd_ref, gpf_ref, out_ref, acc_ref,
                    dscr_ref):
    bf16 = jnp.bfloat16
    tm = x_ref.shape[0]

    def token_order(ref, slot, dil):
        halves = GROUP_W // 128
        for rho in range(dil):
            for c in range(halves):
                lanes = slice(rho * GROUP_W + c * 128, rho * GROUP_W + (c + 1) * 128)
                dscr_ref[slot + c, pl.ds(rho, tm // dil, stride=dil), :] = ref[:, lanes]
        return jnp.concatenate([dscr_ref[slot + c] for c in range(halves)], axis=1)

    l1, od1 = l1_ref[...], od1_ref[...]
    l4, od4 = token_order(l4_ref, 0, 4), token_order(od4_ref, 2, 4)
    l16, od16 = token_order(l16_ref, 4, 16), token_order(od16_ref, 6, 16)
    m = jnp.maximum(jnp.maximum(l1, l4), l16)
    e1, e4, e16 = jnp.exp2(l1 - m), jnp.exp2(l4 - m), jnp.exp2(l16 - m)
    odil = (e1 * od1 + e4 * od4 + e16 * od16) / (e1 + e4 + e16)
    y = (_dot(omoba_ref[...], wout_ref[0]) + _dot(ofox_ref[...], wout_ref[1])
         + _dot(omla_ref[...], wout_ref[2]) + _dot(odil.astype(bf16), wout_ref[3]))
    x1 = x_ref[...] + y * _rms_scale(y, D_MODEL) * gpost_ref[...]
    hb = (x1 * _rms_scale(x1, D_MODEL) * gpre_ref[...]).astype(bf16)
    acc_ref[...] = jnp.zeros_like(acc_ref)
    for c in range(D_FF // FF_CHUNK):
        cols = slice(c * FF_CHUNK, (c + 1) * FF_CHUNK)
        g = _dot(hb, wg_ref[:, cols])
        u = _dot(hb, wu_ref[:, cols])
        f = (g * jax.nn.sigmoid(g) * u).astype(bf16)
        acc_ref[...] += _dot(f, wd_ref[cols, :])
    f = acc_ref[...]
    out_ref[...] = x1 + f * _rms_scale(f, D_MODEL) * gpf_ref[...]


def _rope_tables(positions):
    t = positions.reshape(-1).astype(jnp.float32)

    def tab(dim):
        inv = ROPE_THETA ** (-jnp.arange(0, dim, 2, dtype=jnp.float32) / dim)
        ang = t[:, None] * inv
        return jnp.cos(ang), jnp.sin(ang)

    c64, s64 = tab(HEAD_DIM)
    c16, s16 = tab(MLA_ROPE)
    ct64, st64 = jnp.tile(c64, (1, 4)), jnp.tile(s64, (1, 4))
    pad = jnp.zeros((t.shape[0], 128 - MLA_ROPE), jnp.float32)
    ct16 = jnp.concatenate([c16, c16, pad], axis=1)
    st16 = jnp.concatenate([s16, s16, pad], axis=1)
    cT16, sT16 = jnp.tile(c16, (1, 4)).T, jnp.tile(s16, (1, 4)).T
    return ct64, st64, ct64.T, st64.T, ct16, st16, cT16, sT16


_HALF_PERM = np.array([h * 64 + half * 32 + j for half in (0, 1) for h in range(4) for j in range(32)])
_QROPE_ROWS = np.array([h * 96 + 64 + half * 16 + j for half in (0, 1) for h in range(4) for j in range(16)])
_QNOPE_ROWS = np.array([h * 96 + j for h in range(4) for j in range(64)])
_KNOPE_COLS = np.array([h * 128 + j for h in range(4) for j in range(64)])
_VMLA_COLS = np.array([h * 128 + 64 + j for h in range(4) for j in range(64)])


def _prep_layer_weights(w_in, b_forget, g_mla_q, w_mla_q_up, g_mla_kv, w_mla_kv_up, w_out, w_gate, w_up, w_down):
    bf16 = jnp.bfloat16
    depth = w_in.shape[0]
    sl = lambda a, b: w_in[:, :, a:b]
    scale = HEAD_DIM ** -0.5 * LOG2E
    moba_q, moba_k, moba_v = sl(0, 256)[..., _HALF_PERM] * scale, sl(256, 512)[..., _HALF_PERM], sl(512, 768)
    fox_q, fox_k, fox_v = sl(768, 1024) * scale, sl(1024, 1280), sl(1280, 1536)
    fg, cq, ckv, kr = sl(1536, 1540), sl(1540, 1732), sl(1732, 1860), sl(1860, 1892)
    dil_q, dil_k, dil_v = sl(1892, 2148)[..., _HALF_PERM] * scale, sl(2148, 2404)[..., _HALF_PERM], sl(2404, 2660)
    kr_rot = jnp.concatenate([-kr[..., 16:], kr[..., :16]], axis=-1)
    z = lambda n: jnp.zeros((depth, D_MODEL, n), w_in.dtype)
    wtok = jnp.concatenate([moba_k, fox_k, dil_q, dil_k, dil_v, cq, z(256 - MLA_Q_RANK), ckv,
                            kr, fg, z(128 - 36), kr_rot, z(96)], axis=-1).astype(bf16)
    wtr = jnp.swapaxes(jnp.concatenate([moba_q, moba_v, fox_q, fox_v], axis=-1), 1, 2).astype(bf16)
    bfg = jnp.zeros((depth, 1, 128), jnp.float32).at[:, 0, _FG_LANE:_FG_LANE + N_HEADS].set(b_forget)
    gq = jnp.pad(g_mla_q, ((0, 0), (0, 256 - MLA_Q_RANK)))[:, None, :]
    wq_rows = jnp.swapaxes(w_mla_q_up, 1, 2)
    wqT = jnp.concatenate([wq_rows[:, _QNOPE_ROWS], wq_rows[:, _QROPE_ROWS]], axis=1)
    wqT = jnp.pad(wqT, ((0, 0), (0, 0), (0, 256 - MLA_Q_RANK))).astype(bf16)
    gkv = g_mla_kv[:, None, :]
    wkn = w_mla_kv_up[:, :, _KNOPE_COLS].astype(bf16)
    wvT = jnp.swapaxes(w_mla_kv_up[:, :, _VMLA_COLS], 1, 2).astype(bf16)
    wout = w_out.reshape(depth, 4, GROUP_W, D_MODEL).astype(bf16)
    wg, wu, wd = w_gate.astype(bf16), w_up.astype(bf16), w_down.astype(bf16)
    return wtok, wtr, bfg, gq, wqT, gkv, wkn, wvT, wout, wg, wu, wd


def _const_spec(shape):
    return pl.BlockSpec(shape, lambda *_: (0,) * len(shape))


def _params(*sem):
    return pltpu.CompilerParams(dimension_semantics=sem, vmem_limit_bytes=VMEM_LIMIT)


def _in_proj(x2, g, wtok, wtr, tri, tables, bfg, gq, wqT, gkv, wkn, wvT, seq):
    t = x2.shape[0]
    tm = ROW_TILE
    nt = t // tm
    bf16, f32 = jnp.bfloat16, jnp.float32
    ct64, st64, cT64, sT64, ct16, st16, cT16, sT16 = tables
    tok_spec = lambda w: pl.BlockSpec((tm, w), lambda i: (i, 0))
    tr_spec = lambda r: pl.BlockSpec((r, tm), lambda i: (0, i))
    blk3 = lambda n, r, c: pl.BlockSpec((n, r, c), lambda i: (i, 0, 0))
    mb = tm // MOBA_BLOCK
    ab = tm // KEY_BLOCK
    vrows = N_HEADS * V_ROWS
    vt_shape = jax.ShapeDtypeStruct((t // KEY_BLOCK, vrows, KEY_BLOCK), bf16)
    in_specs = [tok_spec(D_MODEL), _const_spec((1, D_MODEL)), _const_spec(wtok.shape), _const_spec(wtr.shape),
                _const_spec(tri.shape), tok_spec(128), tok_spec(128), tr_spec(128), tr_spec(128),
                tok_spec(128), tok_spec(128), tr_spec(64), tr_spec(64), _const_spec((1, 128)),
                _const_spec((1, 256)), _const_spec(wqT.shape), _const_spec((1, 128)), _const_spec(wkn.shape),
                _const_spec(wvT.shape)]
    out_shape = [
        jax.ShapeDtypeStruct((t, GROUP_W), bf16),
        jax.ShapeDtypeStruct((nt, mb, GROUP_W), f32),
        jax.ShapeDtypeStruct((GROUP_W, t), bf16),
        vt_shape,
        jax.ShapeDtypeStruct((t, 512), bf16),
        jax.ShapeDtypeStruct((GROUP_W, t), bf16),
        vt_shape,
        jax.ShapeDtypeStruct((nt, 1, 128), f32),
        *[jax.ShapeDtypeStruct((t // d, d * GROUP_W), bf16) for d in DILATIONS] * 3,
        jax.ShapeDtypeStruct((t, 512), bf16),
        jax.ShapeDtypeStruct((384, t), bf16),
        vt_shape,
    ]
    vt_spec = blk3(ab, vrows, KEY_BLOCK)
    out_specs = [tok_spec(GROUP_W), blk3(1, mb, GROUP_W), tr_spec(GROUP_W), vt_spec,
                 tok_spec(512), tr_spec(GROUP_W), vt_spec, blk3(1, 1, 128),
                 *[pl.BlockSpec((tm // d, d * GROUP_W), lambda i: (i, 0)) for d in DILATIONS] * 3,
                 tok_spec(512), tr_spec(384), vt_spec]
    return pl.pallas_call(
        functools.partial(_in_proj_kernel, tiles_per_seq=seq // tm),
        grid=(nt,), in_specs=in_specs, out_specs=out_specs, out_shape=out_shape,
        scratch_shapes=[pltpu.VMEM((1, 128), f32), pltpu.VMEM((3 * GROUP_W // 128, tm, 128), f32)],
        compiler_params=_params("arbitrary"), name="in_proj",
    )(x2, g, wtok, wtr, tri, ct64, st64, cT64, sT64, ct16, st16, cT16, sT16, bfg, gq, wqT, gkv, wkn, wvT)


def _attention_scratch(tq, tk=KEY_BLOCK):
    return [pltpu.VMEM((GROUP_W, N_HEADS * tq), jnp.bfloat16),
            pltpu.VMEM((8, tq), jnp.float32),
            pltpu.VMEM((N_HEADS, V_ROWS, tq), jnp.float32),
            pltpu.VMEM((N_HEADS, tk, tq), jnp.float32),
            pltpu.VMEM((N_HEADS, tk, tq), jnp.float32),
            pltpu.VMEM((N_HEADS, 8, tq), jnp.float32),
            pltpu.VMEM((N_HEADS, 8, tq), jnp.float32)]


def _dense_attention(body, name, qT, k, vT, extra, batch, seq):
    tq = ATT_TILE
    nq = seq // tq
    t = batch * seq
    in_specs = [pl.BlockSpec((qT.shape[0], tq), lambda b, i: (0, b * nq + i)),
                pl.BlockSpec((seq, k.shape[1]), lambda b, i: (b, 0)),
                pl.BlockSpec((seq // KEY_BLOCK, N_HEADS * V_ROWS, KEY_BLOCK), lambda b, i: (b, 0, 0))]
    in_specs += [pl.BlockSpec((seq // KEY_BLOCK, 1, 128), lambda b, i: (b, 0, 0)) for _ in extra]
    return pl.pallas_call(
        body, grid=(batch, nq), in_specs=in_specs,
        out_specs=pl.BlockSpec((tq, GROUP_W), lambda b, i: (b * nq + i, 0)),
        out_shape=jax.ShapeDtypeStruct((t, GROUP_W), jnp.bfloat16),
        scratch_shapes=_attention_scratch(tq),
        compiler_params=_params("arbitrary", "arbitrary"), name=name,
    )(qT, k, vT, *extra)


def _moba_attention(qT, k, vT, kmean, batch, seq):
    tq = ATT_TILE
    nq = seq // tq
    nb = seq // MOBA_BLOCK
    t = batch * seq
    return pl.pallas_call(
        _moba_kernel, grid=(batch, nq),
        in_specs=[pl.BlockSpec((GROUP_W, tq), lambda b, i: (0, b * nq + i)),
                  pl.BlockSpec((seq, GROUP_W), lambda b, i: (b, 0)),
                  pl.BlockSpec((seq // KEY_BLOCK, N_HEADS * V_ROWS, KEY_BLOCK), lambda b, i: (b, 0, 0)),
                  pl.BlockSpec((1, nb, GROUP_W), lambda b, i: (b, 0, 0))],
        out_specs=pl.BlockSpec((tq, GROUP_W), lambda b, i: (b * nq + i, 0)),
        out_shape=jax.ShapeDtypeStruct((t, GROUP_W), jnp.bfloat16),
        scratch_shapes=_attention_scratch(tq) + [pltpu.VMEM((N_HEADS * nb, tq), jnp.float32)],
        compiler_params=_params("arbitrary", "arbitrary"), name="moba_attention",
    )(qT, k, vT, kmean)


def _dilated_attention(q, k, v, batch, seq, dil):
    n = seq // dil
    tq = n if n <= 2 * DIL_TILE else DIL_TILE
    window = min(tq + DIL_WINDOW_STEPS, n)
    view = lambda a: a.reshape(batch, n, dil * GROUP_W)
    tiles = min(DIL_PROBLEMS, n // tq)
    classes = min(DIL_PROBLEMS // tiles, dil)
    qspec = pl.BlockSpec((1, tiles * tq, classes * GROUP_W), lambda b, r, i: (b, i, r))
    kspec = pl.BlockSpec((1, n, classes * GROUP_W), lambda b, r, i: (b, 0, r))
    shape = jax.ShapeDtypeStruct((batch, n, dil * GROUP_W), jnp.float32)
    o, lse = pl.pallas_call(
        functools.partial(_dilated_kernel, window=window, tq=tq),
        grid=(batch, dil // classes, n // (tiles * tq)),
        in_specs=[qspec, kspec, kspec], out_specs=[qspec, qspec], out_shape=[shape, shape],
        compiler_params=_params("arbitrary", "arbitrary", "arbitrary"), name=f"dilated_attention_d{dil}",
    )(view(q), view(k), view(v))
    return o.reshape(batch * n, dil * GROUP_W), lse.reshape(batch * n, dil * GROUP_W)


def _out_ffn(x2, omoba, ofox, omla, dil_outs, wout, gpost, gpre, wg, wu, wd, gpf):
    t = x2.shape[0]
    tm = ROW_TILE
    row = lambda w: pl.BlockSpec((tm, w), lambda i: (i, 0))
    (od1, l1), (od4, l4), (od16, l16) = dil_outs
    dil_specs = [pl.BlockSpec((tm // d, d * GROUP_W), lambda i: (i, 0)) for d in DILATIONS]
    in_specs = ([row(D_MODEL)] + [row(GROUP_W)] * 3 + dil_specs * 2
                + [_const_spec(wout.shape), _const_spec((1, D_MODEL)), _const_spec((1, D_MODEL)),
                   _const_spec(wg.shape), _const_spec(wu.shape), _const_spec(wd.shape), _const_spec((1, D_MODEL))])
    return pl.pallas_call(
        _out_ffn_kernel, grid=(t // tm,), in_specs=in_specs, out_specs=row(D_MODEL),
        out_shape=jax.ShapeDtypeStruct((t, D_MODEL), jnp.float32),
        scratch_shapes=[pltpu.VMEM((tm, D_MODEL), jnp.float32), pltpu.VMEM((4 * GROUP_W // 128, tm, 128), jnp.float32)],
        compiler_params=_params("arbitrary"), name="out_ffn",
    )(x2, omoba, ofox, omla, od1, od4, od16, l1, l4, l16, wout, gpost, gpre, wg, wu, wd, gpf)


def kernel(x, positions, w_in, b_forget, g_mla_q, w_mla_q_up, g_mla_kv, w_mla_kv_up, w_out, g_pre_mix, g_post_mix, w_gate, w_up, w_down, g_pre_ffn, g_post_ffn):
    batch, seq, _ = x.shape
    depth = w_in.shape[0]
    assert seq % ROW_TILE == 0 and seq % (DILATIONS[-1] * DIL_WINDOW_STEPS) == 0
    assert ROW_TILE == KEY_BLOCK
    tables = _rope_tables(positions)
    wtok, wtr, bfg, gq, wqT, gkv, wkn, wvT, wout, wg, wu, wd = _prep_layer_weights(
        w_in, b_forget, g_mla_q, w_mla_q_up, g_mla_kv, w_mla_kv_up, w_out, w_gate, w_up, w_down)
    tri = jnp.tril(jnp.ones((ROW_TILE, ROW_TILE), jnp.bfloat16))
    x2 = x.reshape(batch * seq, D_MODEL)
    for l in range(depth):
        (k_moba, kmean, qT_moba, vT_moba, k_fox, qT_fox, vT_fox, cbase, *dil_qkv,
         k_mla, qT_mla, vT_mla) = _in_proj(x2, g_pre_mix[l][None], wtok[l], wtr[l], tri, tables, bfg[l], gq[l],
                                          wqT[l], gkv[l], wkn[l], wvT[l], seq)
        nd = len(DILATIONS)
        o_moba = _moba_attention(qT_moba, k_moba, vT_moba, kmean.reshape(batch, seq // MOBA_BLOCK, GROUP_W),
                                 batch, seq)
        o_fox = _dense_attention(_fox_kernel, "fox_attention", qT_fox, k_fox, vT_fox, [cbase], batch, seq)
        o_mla = _dense_attention(_mla_kernel, "mla_attention", qT_mla, k_mla, vT_mla, [], batch, seq)
        dil_outs = [_dilated_attention(dil_qkv[n], dil_qkv[nd + n], dil_qkv[2 * nd + n], batch, seq, d)
                    for n, d in enumerate(DILATIONS)]
        x2 = _out_ffn(x2, o_moba, o_fox, o_mla, dil_outs, wout[l], g_post_mix[l][None], g_pre_ffn[l][None],
                      wg[l], wu[l], wd[l], g_post_ffn[l][None])
    return x2.reshape(batch, seq, D_MODEL)
```

```python
import functools

import numpy as np
import jax
import jax.numpy as jnp
from jax import lax
from jax.experimental import pallas as pl
from jax.experimental.pallas import tpu as pltpu

D_MODEL = 1024
HEAD_DIM = 64
N_HEADS = 4
GROUP_W = N_HEADS * HEAD_DIM
MOBA_BLOCK = 256
MOBA_TOPK = 3
MLA_Q_RANK = 192
MLA_KV_RANK = 128
MLA_NOPE = 64
MLA_ROPE = 32
DIL_WINDOW_STEPS = 128
DILATIONS = (1, 4, 16)
ROPE_THETA = 10000.0
RMS_EPS = 1e-6
D_FF = 2816
FF_CHUNK = 256
V_ROWS = 80
LOG2E = 1.4426950408889634
MASKED_LOGIT = -1e30

ROW_TILE = 512
ATT_TILE = 512
KEY_BLOCK = 512
TILES_PER_STEP = 2
SLAB = MOBA_BLOCK
DIL_TILE = 256
DIL_PROBLEMS = 8
VMEM_LIMIT = 56 * 1024 * 1024

_C_KMOBA, _C_KFOX, _C_QDIL, _C_KDIL, _C_VDIL, _C_CQ = 0, 256, 512, 768, 1024, 1280
_C_CKV, _C_X, _C_Y, _N_TOK = 1536, 1664, 1792, 1920
_FG_LANE = 32

_NT = (((1,), (1,)), ((), ()))


def _dot(a, b):
    return jnp.dot(a, b, preferred_element_type=jnp.float32)


def _dot_nt(a, b):
    return lax.dot_general(a, b, _NT, preferred_element_type=jnp.float32)


def _rms_scale(v, n):
    return lax.rsqrt(jnp.sum(v * v, axis=-1, keepdims=True) * (1.0 / n) + RMS_EPS)


def _in_proj_kernel(x_ref, g_ref, wtok_ref, wtr_ref, tri_ref, ct64_ref, st64_ref, cT64_ref, sT64_ref,
                    ct16_ref, st16_ref, cT16_ref, sT16_ref, bfg_ref, gq_ref, wqT_ref, gkv_ref, wkn_ref, wvT_ref,
                    kmoba_ref, kmean_ref, qTmoba_ref, vTmoba_ref, kfox_ref, qTfox_ref, vTfox_ref, cbase_ref,
                    q1_ref, q4_ref, q16_ref, k1_ref, k4_ref, k16_ref, v1_ref, v4_ref, v16_ref,
                    kmla_ref, qTmla_ref, vTmla_ref, carry_ref, dscr_ref, *, tiles_per_seq):
    bf16 = jnp.bfloat16
    qdil_refs, kdil_refs, vdil_refs = (q1_ref, q4_ref, q16_ref), (k1_ref, k4_ref, k16_ref), (v1_ref, v4_ref, v16_ref)
    x = x_ref[...]
    hb = (x * _rms_scale(x, D_MODEL) * g_ref[...]).astype(bf16)
    tm = x.shape[0]

    def tok(c0, w):
        return _dot(hb, wtok_ref[:, c0:c0 + w])

    def rope_tok(z):
        c, s = ct64_ref[...], st64_ref[...]
        x1, x2 = z[:, :128], z[:, 128:]
        return jnp.concatenate([x1 * c - x2 * s, x2 * c + x1 * s], axis=1)

    def with_ones(vT):
        ones = jnp.ones((V_ROWS - HEAD_DIM, tm), bf16)
        parts = []
        for h in range(N_HEADS):
            parts += [vT[h * HEAD_DIM:(h + 1) * HEAD_DIM].astype(bf16), ones]
        return jnp.concatenate(parts, axis=0)

    def rope_tr(zT):
        c, s = cT64_ref[...], sT64_ref[...]
        x1, x2 = zT[:128], zT[128:]
        return jnp.concatenate([x1 * c - x2 * s, x2 * c + x1 * s], axis=0)

    k_moba = rope_tok(tok(_C_KMOBA, GROUP_W))
    kmoba_ref[...] = k_moba.astype(bf16)
    for blk in range(tm // MOBA_BLOCK):
        kmean_ref[0, blk:blk + 1, :] = jnp.mean(k_moba[blk * MOBA_BLOCK:(blk + 1) * MOBA_BLOCK], axis=0, keepdims=True)
    qTmoba_ref[...] = rope_tr(_dot_nt(wtr_ref[0:256, :], hb)).astype(bf16)
    def store_value_blocks(ref, vT):
        for blk in range(tm // KEY_BLOCK):
            ref[blk] = vT[:, blk * KEY_BLOCK:(blk + 1) * KEY_BLOCK]

    store_value_blocks(vTmoba_ref, with_ones(_dot_nt(wtr_ref[256:512, :], hb)))

    qTfox_ref[...] = _dot_nt(wtr_ref[512:768, :], hb).astype(bf16)
    store_value_blocks(vTfox_ref, with_ones(_dot_nt(wtr_ref[768:1024, :], hb)))

    def emit_dilated(refs, slot, z):
        refs[0][...] = z.astype(bf16)
        for c in range(GROUP_W // 128):
            dscr_ref[slot + c] = z[:, c * 128:(c + 1) * 128]
        for ref, dil in zip(refs[1:], DILATIONS[1:]):
            for rho in range(dil):
                for c in range(GROUP_W // 128):
                    rows = dscr_ref[slot + c, pl.ds(rho, tm // dil, stride=dil), :]
                    ref[:, rho * GROUP_W + c * 128:rho * GROUP_W + (c + 1) * 128] = rows.astype(bf16)

    emit_dilated(qdil_refs, 0, rope_tok(tok(_C_QDIL, GROUP_W)))
    emit_dilated(kdil_refs, 2, rope_tok(tok(_C_KDIL, GROUP_W)))
    emit_dilated(vdil_refs, 4, tok(_C_VDIL, GROUP_W))

    xy = tok(_C_X, 256)
    xblk = xy[:, :128]
    yblk = xy[:, 128:]
    fg = xblk + bfg_ref[...]
    logf = jnp.minimum(fg, 0.0) - jnp.log1p(jnp.exp(-jnp.abs(fg)))
    lane = lax.broadcasted_iota(jnp.int32, logf.shape, 1)
    is_gate = (lane >= _FG_LANE) & (lane < _FG_LANE + N_HEADS)

    def three_pieces(v):
        v = jnp.where(is_gate, v, 0.0)
        p1 = v.astype(bf16).astype(jnp.float32)
        p2 = (v - p1).astype(bf16).astype(jnp.float32)
        p3 = (v - p1 - p2).astype(bf16).astype(jnp.float32)
        return p1 + pltpu.roll(p2, 8, 1) + pltpu.roll(p3, 16, 1)

    @pl.when(pl.program_id(0) % tiles_per_seq == 0)
    def _():
        carry_ref[...] = jnp.zeros_like(carry_ref)

    part = _dot(tri_ref[...], three_pieces(logf).astype(bf16))
    cum = part + pltpu.roll(part, 128 - 8, 1) + pltpu.roll(part, 128 - 16, 1) + carry_ref[...]
    carry_ref[...] = cum[tm - 1:tm, :]
    cbase_ref[0] = cum[0:1, :]

    dcols = three_pieces((cum[0:1, :] - cum) * LOG2E)
    kf = tok(_C_KFOX, GROUP_W)
    kfox_ref[...] = jnp.concatenate([kf[:, :128], dcols, kf[:, 128:], dcols], axis=1).astype(bf16)

    zcq = tok(_C_CQ, 256)
    cq = (zcq * _rms_scale(zcq, MLA_Q_RANK) * gq_ref[...]).astype(bf16)
    qcT = _dot_nt(wqT_ref[...], cq)
    c16, s16 = cT16_ref[...], sT16_ref[...]
    r1h, r2h = qcT[256:320], qcT[320:384]
    qT = jnp.concatenate([qcT[0:256], r1h * c16 - r2h * s16, r2h * c16 + r1h * s16], axis=0)
    qTmla_ref[...] = (qT * ((MLA_NOPE + MLA_ROPE) ** -0.5 * LOG2E)).astype(bf16)

    zckv = tok(_C_CKV, MLA_KV_RANK)
    ckv = (zckv * _rms_scale(zckv, MLA_KV_RANK) * gkv_ref[...]).astype(bf16)
    kn = _dot(ckv, wkn_ref[...])
    krope = xblk * ct16_ref[...] + yblk * st16_ref[...]
    kmla_ref[...] = jnp.concatenate([kn[:, :128], krope, kn[:, 128:], krope], axis=1).astype(bf16)
    store_value_blocks(vTmla_ref, with_ones(_dot_nt(wvT_ref[...], ckv)))


def _softmax_update(parts, vb, m_ref, acc_ref, h, offsets=None, maxes=None):
    if offsets is None:
        offsets = [None] * len(parts)
    if maxes is None:
        maxes = [jnp.max(s, axis=0, keepdims=True) for s in parts]
    m_old = m_ref[h:h + 1, :]
    blk_max = None
    for mx, off in zip(maxes, offsets):
        mx = mx if off is None else mx + off
        blk_max = mx if blk_max is None else jnp.maximum(blk_max, mx)
    m_new = jnp.maximum(m_old, blk_max)
    ps = []
    for s, off in zip(parts, offsets):
        shift = m_new if off is None else m_new - off
        ps.append(jnp.exp2(s - shift).astype(jnp.bfloat16))
    p = ps[0] if len(ps) == 1 else jnp.concatenate(ps, axis=0)
    acc_ref[h] = jnp.exp2(m_old - m_new) * acc_ref[h] + _dot(vb, p)
    m_ref[h:h + 1, :] = m_new


def _store_scores(buf, first_head, s):
    s_ref, mx_ref = buf
    tq = s_ref.shape[2]
    for k in range(s.shape[1] // tq):
        sh = s[:, k * tq:(k + 1) * tq]
        s_ref[first_head + k] = sh
        for n in range(sh.shape[0] // SLAB):
            mx_ref[first_head + k, n:n + 1, :] = jnp.max(sh[n * SLAB:(n + 1) * SLAB], axis=0, keepdims=True)


def _slabs(buf, h, tq):
    s_ref, mx_ref = buf
    n_slabs = s_ref.shape[1] // SLAB
    return ([s_ref[h, n * SLAB:(n + 1) * SLAB, :] for n in range(n_slabs)],
            [mx_ref[h, n:n + 1, :] for n in range(n_slabs)])


def _diagonal_masks(tq, tk):
    key = lax.broadcasted_iota(jnp.int32, (tk, tq), 0)
    qry = lax.broadcasted_iota(jnp.int32, (tk, tq), 1)
    return [key + r * tk <= qry for r in range(tq // tk)]


def _init_state(m_ref, acc_ref):
    m_ref[...] = jnp.full(m_ref.shape, -jnp.inf, jnp.float32)
    acc_ref[...] = jnp.zeros(acc_ref.shape, jnp.float32)


def _finalize(acc_ref, o_ref):
    outs = [acc_ref[h, 0:HEAD_DIM, :] / acc_ref[h, HEAD_DIM:HEAD_DIM + 1, :] for h in range(N_HEADS)]
    o_ref[...] = jnp.concatenate(outs, axis=0).T.astype(o_ref.dtype)


def _vblock(vT_ref, j, h):
    return vT_ref[j, h * V_ROWS:(h + 1) * V_ROWS, :]


def _run_problems(problems, last):
    for first, buf_a, _, scores_into, _ in problems:
        for h in range(N_HEADS):
            scores_into(buf_a, first, h)
    for h in range(N_HEADS):
        for first, buf_a, buf_b, scores_into, consume in problems:
            scores_into(buf_b, 0, h)
            consume(buf_a, first, True, (h,))

    def pair(p, carry):
        j0 = 2 * p
        for h in range(N_HEADS):
            for _, buf_a, buf_b, scores_into, consume in problems:
                scores_into(buf_a, j0 + 1, h)
                consume(buf_b, j0, False, (h,))
        for h in range(N_HEADS):
            for _, buf_a, buf_b, scores_into, consume in problems:
                scores_into(buf_b, jnp.minimum(j0 + 2, last), h)
                consume(buf_a, j0 + 1, False, (h,))
        return carry

    lax.fori_loop(0, problems[0][0] // 2, pair, 0)
    for t, (first, _, buf_b, _, consume) in enumerate(problems):
        if t % 2 == 1:
            consume(buf_b, first - 1, False, range(N_HEADS))


def _key_rows(k_ref, j, tk):
    return k_ref[pl.ds(pl.multiple_of(j * tk, tk), tk), :]


def _causal_slabs(parts, causal):
    return [jnp.where(causal[n * SLAB:(n + 1) * SLAB], s, -jnp.inf) for n, s in enumerate(parts)]


def _two_tile_scores(buf, k_ref, qcat_ref, j, h, tk, tq):
    kb = k_ref[pl.ds(pl.multiple_of(j * tk, tk), tk), 256 * (h // 2):256 * (h // 2 + 1)]
    _store_scores(buf, h, _dot(kb, qcat_ref[:, h * tq:(h + 1) * tq]))


def _tile_views(t, tq, qT_ref, o_ref, scratch):
    return (qT_ref.at[:, t * tq:(t + 1) * tq], o_ref.at[t * tq:(t + 1) * tq, :]) + tuple(r.at[t] for r in scratch)


def _fox_kernel(qT_ref, k_ref, vT_ref, cbase_ref, o_ref, *scratch):
    tq = qT_ref.shape[1] // TILES_PER_STEP
    tk = scratch[3].shape[2]
    assert tq == tk
    causal = _diagonal_masks(tq, tk)[0]
    zeros64 = jnp.zeros((HEAD_DIM, tq), jnp.bfloat16)
    row = lax.broadcasted_iota(jnp.int32, (128, tq), 0)

    def problem(t):
        q_ref, _, qcat_ref, m_ref, acc_ref, sa_ref, sb_ref, mxa_ref, mxb_ref = _tile_views(t, tq, qT_ref, o_ref, scratch)
        first = pl.program_id(1) * TILES_PER_STEP + t
        for h in range(N_HEADS):
            qh = q_ref[h * HEAD_DIM:(h + 1) * HEAD_DIM, :]
            pick = (row == _FG_LANE + h) | (row == _FG_LANE + 8 + h) | (row == _FG_LANE + 16 + h)
            rows = ([qh, zeros64] if h % 2 == 0 else [zeros64, qh]) + [pick.astype(jnp.bfloat16)]
            qcat_ref[:, h * tq:(h + 1) * tq] = jnp.concatenate(rows, axis=0)
        _init_state(m_ref, acc_ref)

        def scores_into(buf, j, h):
            _two_tile_scores(buf, k_ref, qcat_ref, j, h, tk, tq)

        def consume(buf, j, own, heads):
            offs = (cbase_ref[first] - cbase_ref[j]) * LOG2E
            for h in heads:
                parts, maxes = _slabs(buf, h, tq)
                if own:
                    parts, maxes = _causal_slabs(parts, causal), None
                off = offs[:, _FG_LANE + h:_FG_LANE + h + 1]
                _softmax_update(parts, _vblock(vT_ref, j, h), m_ref, acc_ref, h, [off] * len(parts), maxes)

        return first, (sa_ref, mxa_ref), (sb_ref, mxb_ref), scores_into, consume

    _run_problems([problem(t) for t in range(TILES_PER_STEP)], k_ref.shape[0] // tk - 1)
    for t in range(TILES_PER_STEP):
        views = _tile_views(t, tq, qT_ref, o_ref, scratch)
        _finalize(views[4], views[1])


def _mla_kernel(qT_ref, k_ref, vT_ref, o_ref, *scratch):
    tq = qT_ref.shape[1] // TILES_PER_STEP
    tk = scratch[3].shape[2]
    assert tq == tk
    causal = _diagonal_masks(tq, tk)[0]
    zeros64 = jnp.zeros((MLA_NOPE, tq), jnp.bfloat16)
    zeros96 = jnp.zeros((256 - 128 - MLA_ROPE, tq), jnp.bfloat16)

    def problem(t):
        q_ref, _, qcat_ref, m_ref, acc_ref, sa_ref, sb_ref, mxa_ref, mxb_ref = _tile_views(t, tq, qT_ref, o_ref, scratch)
        for h in range(N_HEADS):
            qn = q_ref[h * MLA_NOPE:(h + 1) * MLA_NOPE, :]
            qr1 = q_ref[256 + 16 * h:256 + 16 * (h + 1), :]
            qr2 = q_ref[320 + 16 * h:320 + 16 * (h + 1), :]
            nope = [qn, zeros64] if h % 2 == 0 else [zeros64, qn]
            qcat_ref[:, h * tq:(h + 1) * tq] = jnp.concatenate(nope + [qr1, qr2, zeros96], axis=0)
        _init_state(m_ref, acc_ref)

        def scores_into(buf, j, h):
            _two_tile_scores(buf, k_ref, qcat_ref, j, h, tk, tq)

        def consume(buf, j, own, heads):
            for h in heads:
                parts, maxes = _slabs(buf, h, tq)
                if own:
                    parts, maxes = _causal_slabs(parts, causal), None
                _softmax_update(parts, _vblock(vT_ref, j, h), m_ref, acc_ref, h, None, maxes)

        return (pl.program_id(1) * TILES_PER_STEP + t, (sa_ref, mxa_ref), (sb_ref, mxb_ref), scores_into, consume)

    _run_problems([problem(t) for t in range(TILES_PER_STEP)], k_ref.shape[0] // tk - 1)
    for t in range(TILES_PER_STEP):
        views = _tile_views(t, tq, qT_ref, o_ref, scratch)
        _finalize(views[4], views[1])


def _moba_kernel(qT_ref, k_ref, vT_ref, km_ref, o_ref, *scratch):
    tq = qT_ref.shape[1] // TILES_PER_STEP
    tk = scratch[3].shape[2]
    assert tq == tk
    nb = km_ref.shape[1]
    per = tk // SLAB
    causal = _diagonal_masks(tq, tk)[0]
    km = km_ref[0].astype(jnp.bfloat16)
    row = lax.broadcasted_iota(jnp.int32, (GROUP_W, tq), 0)
    blk = lax.broadcasted_iota(jnp.int32, (nb, tq), 0)
    neg_inf = jnp.float32(-jnp.inf)

    def problem(t):
        (q_ref, _, qcat_ref, m_ref, acc_ref, sa_ref, sb_ref, mxa_ref, mxb_ref,
         bias_ref) = _tile_views(t, tq, qT_ref, o_ref, scratch)
        first = pl.program_id(1) * TILES_PER_STEP + t
        q = q_ref[...]
        qblk = first * (tq // MOBA_BLOCK) + lax.broadcasted_iota(jnp.int32, (1, tq), 1) // MOBA_BLOCK
        for h in range(N_HEADS):
            in_head = (((row >= 32 * h) & (row < 32 * (h + 1)))
                       | ((row >= 128 + 32 * h) & (row < 128 + 32 * (h + 1))))
            qm = jnp.where(in_head, q, jnp.zeros_like(q))
            qcat_ref[:, h * tq:(h + 1) * tq] = qm
            g = jnp.where(blk < qblk, _dot(km, qm), neg_inf)
            sel = jnp.zeros((nb, tq), jnp.bool_)
            for _ in range(MOBA_TOPK):
                mx = jnp.max(g, axis=0, keepdims=True)
                cand = jnp.where((g == mx) & (mx > neg_inf), blk, nb)
                chosen = blk == jnp.min(cand, axis=0, keepdims=True)
                sel = sel | chosen
                g = jnp.where(chosen, neg_inf, g)
            bias_ref[h * nb:(h + 1) * nb, :] = jnp.where(sel, 0.0, MASKED_LOGIT)
        _init_state(m_ref, acc_ref)

        def scores_into(buf, j, h):
            _store_scores(buf, h, _dot(_key_rows(k_ref, j, tk), qcat_ref[:, h * tq:(h + 1) * tq]))

        def consume(buf, j, own, heads):
            for h in heads:
                parts, maxes = _slabs(buf, h, tq)
                gates = [bias_ref[pl.ds(h * nb + per * j + n, 1), :] for n in range(per)]
                if own:
                    gates = [jnp.where(per * j + n < qblk, g, 0.0) for n, g in enumerate(gates)]
                    parts, maxes = _causal_slabs(parts, causal), None
                _softmax_update(parts, _vblock(vT_ref, j, h), m_ref, acc_ref, h, gates, maxes)

        return first, (sa_ref, mxa_ref), (sb_ref, mxb_ref), scores_into, consume

    _run_problems([problem(t) for t in range(TILES_PER_STEP)], k_ref.shape[0] // tk - 1)
    for t in range(TILES_PER_STEP):
        views = _tile_views(t, tq, qT_ref, o_ref, scratch)
        _finalize(views[4], views[1])


def _dilated_kernel(q_ref, k_ref, v_ref, o_ref, lse_ref, *, window, tq):
    i = pl.program_id(2)
    n = k_ref.shape[1]
    tiles = q_ref.shape[1] // tq
    lane = lax.broadcasted_iota(jnp.int32, (tq, GROUP_W), 1)
    for c in range(q_ref.shape[2] // GROUP_W):
        cls = slice(c * GROUP_W, (c + 1) * GROUP_W)
        for g in range(tiles):
            rows = slice(g * tq, (g + 1) * tq)
            q = q_ref[0, rows, cls]
            a = (i * tiles + g) * tq
            ks = jnp.clip(a - DIL_WINDOW_STEPS, 0, n - window)
            ks = pl.multiple_of(ks, DIL_WINDOW_STEPS)
            kw = k_ref[0, pl.ds(ks, window), cls]
            vw = v_ref[0, pl.ds(ks, window), cls]
            jq = a + lax.broadcasted_iota(jnp.int32, (tq, window), 0)
            jk = ks + lax.broadcasted_iota(jnp.int32, (tq, window), 1)
            band = (jq - jk >= 0) & (jq - jk <= DIL_WINDOW_STEPS)
            o = jnp.zeros((tq, GROUP_W), jnp.float32)
            lse = jnp.zeros((tq, GROUP_W), jnp.float32)
            for h in range(N_HEADS):
                in_head = (((lane >= 32 * h) & (lane < 32 * (h + 1)))
                           | ((lane >= 128 + 32 * h) & (lane < 128 + 32 * (h + 1))))
                qm = jnp.where(in_head, q, jnp.zeros_like(q))
                s = jnp.where(band, _dot_nt(qm, kw), -jnp.inf)
                m = jnp.max(s, axis=1, keepdims=True)
                p = jnp.exp2(s - m)
                l = jnp.sum(p, axis=1, keepdims=True)
                oh = _dot(p.astype(jnp.bfloat16), vw) / l
                out_lanes = (lane >= h * HEAD_DIM) & (lane < (h + 1) * HEAD_DIM)
                o = jnp.where(out_lanes, oh, o)
                lse = jnp.where(out_lanes, m + jnp.log2(l), lse)
            o_ref[0, rows, cls] = o
            lse_ref[0, rows, cls] = lse


def _out_ffn_kernel(x_ref, omoba_ref, ofox_ref, omla_ref, od1_ref, od4_ref, od16_ref, l1_ref, l4_ref, l16_ref,
                    wout_ref, gpost_ref, gpre_ref, wg_ref, wu_ref, wd_ref, gpf_ref, out_ref, acc_ref,
                    dscr_ref):
    bf16 = jnp.bfloat16
    tm = x_ref.shape[0]

    def token_order(ref, slot, dil):
        halves = GROUP_W // 128
        for rho in range(dil):
            for c in range(halves):
                lanes = slice(rho * GROUP_W + c * 128, rho * GROUP_W + (c + 1) * 128)
                dscr_ref[slot + c, pl.ds(rho, tm // dil, stride=dil), :] = ref[:, lanes]
        return jnp.concatenate([dscr_ref[slot + c] for c in range(halves)], axis=1)

    l1, od1 = l1_ref[...], od1_ref[...]
    l4, od4 = token_order(l4_ref, 0, 4), token_order(od4_ref, 2, 4)
    l16, od16 = token_order(l16_ref, 4, 16), token_order(od16_ref, 6, 16)
    m = jnp.maximum(jnp.maximum(l1, l4), l16)
    e1, e4, e16 = jnp.exp2(l1 - m), jnp.exp2(l4 - m), jnp.exp2(l16 - m)
    odil = (e1 * od1 + e4 * od4 + e16 * od16) / (e1 + e4 + e16)
    y = (_dot(omoba_ref[...], wout_ref[0]) + _dot(ofox_ref[...], wout_ref[1])
         + _dot(omla_ref[...], wout_ref[2]) + _dot(odil.astype(bf16), wout_ref[3]))
    x1 = x_ref[...] + y * _rms_scale(y, D_MODEL) * gpost_ref[...]
    hb = (x1 * _rms_scale(x1, D_MODEL) * gpre_ref[...]).astype(bf16)
    acc_ref[...] = jnp.zeros_like(acc_ref)
    for c in range(D_FF // FF_CHUNK):
        cols = slice(c * FF_CHUNK, (c + 1) * FF_CHUNK)
        g = _dot(hb, wg_ref[:, cols])
        u = _dot(hb, wu_ref[:, cols])
        f = (g * jax.nn.sigmoid(g) * u).astype(bf16)
        acc_ref[...] += _dot(f, wd_ref[cols, :])
    f = acc_ref[...]
    out_ref[...] = x1 + f * _rms_scale(f, D_MODEL) * gpf_ref[...]


def _rope_tables(positions):
    t = positions.reshape(-1).astype(jnp.float32)

    def tab(dim):
        inv = ROPE_THETA ** (-jnp.arange(0, dim, 2, dtype=jnp.float32) / dim)
        ang = t[:, None] * inv
        return jnp.cos(ang), jnp.sin(ang)

    c64, s64 = tab(HEAD_DIM)
    c16, s16 = tab(MLA_ROPE)
    ct64, st64 = jnp.tile(c64, (1, 4)), jnp.tile(s64, (1, 4))
    pad = jnp.zeros((t.shape[0], 128 - MLA_ROPE), jnp.float32)
    ct16 = jnp.concatenate([c16, c16, pad], axis=1)
    st16 = jnp.concatenate([s16, s16, pad], axis=1)
    cT16, sT16 = jnp.tile(c16, (1, 4)).T, jnp.tile(s16, (1, 4)).T
    return ct64, st64, ct64.T, st64.T, ct16, st16, cT16, sT16


_HALF_PERM = np.array([h * 64 + half * 32 + j for half in (0, 1) for h in range(4) for j in range(32)])
_QROPE_ROWS = np.array([h * 96 + 64 + half * 16 + j for half in (0, 1) for h in range(4) for j in range(16)])
_QNOPE_ROWS = np.array([h * 96 + j for h in range(4) for j in range(64)])
_KNOPE_COLS = np.array([h * 128 + j for h in range(4) for j in range(64)])
_VMLA_COLS = np.array([h * 128 + 64 + j for h in range(4) for j in range(64)])


def _prep_layer_weights(w_in, b_forget, g_mla_q, w_mla_q_up, g_mla_kv, w_mla_kv_up, w_out, w_gate, w_up, w_down):
    bf16 = jnp.bfloat16
    depth = w_in.shape[0]
    sl = lambda a, b: w_in[:, :, a:b]
    scale = HEAD_DIM ** -0.5 * LOG2E
    moba_q, moba_k, moba_v = sl(0, 256)[..., _HALF_PERM] * scale, sl(256, 512)[..., _HALF_PERM], sl(512, 768)
    fox_q, fox_k, fox_v = sl(768, 1024) * scale, sl(1024, 1280), sl(1280, 1536)
    fg, cq, ckv, kr = sl(1536, 1540), sl(1540, 1732), sl(1732, 1860), sl(1860, 1892)
    dil_q, dil_k, dil_v = sl(1892, 2148)[..., _HALF_PERM] * scale, sl(2148, 2404)[..., _HALF_PERM], sl(2404, 2660)
    kr_rot = jnp.concatenate([-kr[..., 16:], kr[..., :16]], axis=-1)
    z = lambda n: jnp.zeros((depth, D_MODEL, n), w_in.dtype)
    wtok = jnp.concatenate([moba_k, fox_k, dil_q, dil_k, dil_v, cq, z(256 - MLA_Q_RANK), ckv,
                            kr, fg, z(128 - 36), kr_rot, z(96)], axis=-1).astype(bf16)
    wtr = jnp.swapaxes(jnp.concatenate([moba_q, moba_v, fox_q, fox_v], axis=-1), 1, 2).astype(bf16)
    bfg = jnp.zeros((depth, 1, 128), jnp.float32).at[:, 0, _FG_LANE:_FG_LANE + N_HEADS].set(b_forget)
    gq = jnp.pad(g_mla_q, ((0, 0), (0, 256 - MLA_Q_RANK)))[:, None, :]
    wq_rows = jnp.swapaxes(w_mla_q_up, 1, 2)
    wqT = jnp.concatenate([wq_rows[:, _QNOPE_ROWS], wq_rows[:, _QROPE_ROWS]], axis=1)
    wqT = jnp.pad(wqT, ((0, 0), (0, 0), (0, 256 - MLA_Q_RANK))).astype(bf16)
    gkv = g_mla_kv[:, None, :]
    wkn = w_mla_kv_up[:, :, _KNOPE_COLS].astype(bf16)
    wvT = jnp.swapaxes(w_mla_kv_up[:, :, _VMLA_COLS], 1, 2).astype(bf16)
    wout = w_out.reshape(depth, 4, GROUP_W, D_MODEL).astype(bf16)
    wg, wu, wd = w_gate.astype(bf16), w_up.astype(bf16), w_down.astype(bf16)
    return wtok, wtr, bfg, gq, wqT, gkv, wkn, wvT, wout, wg, wu, wd


def _const_spec(shape):
    return pl.BlockSpec(shape, lambda *_: (0,) * len(shape))


def _params(*sem):
    return pltpu.CompilerParams(dimension_semantics=sem, vmem_limit_bytes=VMEM_LIMIT)


def _in_proj(x2, g, wtok, wtr, tri, tables, bfg, gq, wqT, gkv, wkn, wvT, seq):
    t = x2.shape[0]
    tm = ROW_TILE
    nt = t // tm
    bf16, f32 = jnp.bfloat16, jnp.float32
    ct64, st64, cT64, sT64, ct16, st16, cT16, sT16 = tables
    tok_spec = lambda w: pl.BlockSpec((tm, w), lambda i: (i, 0))
    tr_spec = lambda r: pl.BlockSpec((r, tm), lambda i: (0, i))
    blk3 = lambda n, r, c: pl.BlockSpec((n, r, c), lambda i: (i, 0, 0))
    mb = tm // MOBA_BLOCK
    ab = tm // KEY_BLOCK
    vrows = N_HEADS * V_ROWS
    vt_shape = jax.ShapeDtypeStruct((t // KEY_BLOCK, vrows, KEY_BLOCK), bf16)
    in_specs = [tok_spec(D_MODEL), _const_spec((1, D_MODEL)), _const_spec(wtok.shape), _const_spec(wtr.shape),
                _const_spec(tri.shape), tok_spec(128), tok_spec(128), tr_spec(128), tr_spec(128),
                tok_spec(128), tok_spec(128), tr_spec(64), tr_spec(64), _const_spec((1, 128)),
                _const_spec((1, 256)), _const_spec(wqT.shape), _const_spec((1, 128)), _const_spec(wkn.shape),
                _const_spec(wvT.shape)]
    out_shape = [
        jax.ShapeDtypeStruct((t, GROUP_W), bf16),
        jax.ShapeDtypeStruct((nt, mb, GROUP_W), f32),
        jax.ShapeDtypeStruct((GROUP_W, t), bf16),
        vt_shape,
        jax.ShapeDtypeStruct((t, 512), bf16),
        jax.ShapeDtypeStruct((GROUP_W, t), bf16),
        vt_shape,
        jax.ShapeDtypeStruct((nt, 1, 128), f32),
        *[jax.ShapeDtypeStruct((t // d, d * GROUP_W), bf16) for d in DILATIONS] * 3,
        jax.ShapeDtypeStruct((t, 512), bf16),
        jax.ShapeDtypeStruct((384, t), bf16),
        vt_shape,
    ]
    vt_spec = blk3(ab, vrows, KEY_BLOCK)
    out_specs = [tok_spec(GROUP_W), blk3(1, mb, GROUP_W), tr_spec(GROUP_W), vt_spec,
                 tok_spec(512), tr_spec(GROUP_W), vt_spec, blk3(1, 1, 128),
                 *[pl.BlockSpec((tm // d, d * GROUP_W), lambda i: (i, 0)) for d in DILATIONS] * 3,
                 tok_spec(512), tr_spec(384), vt_spec]
    return pl.pallas_call(
        functools.partial(_in_proj_kernel, tiles_per_seq=seq // tm),
        grid=(nt,), in_specs=in_specs, out_specs=out_specs, out_shape=out_shape,
        scratch_shapes=[pltpu.VMEM((1, 128), f32), pltpu.VMEM((3 * GROUP_W // 128, tm, 128), f32)],
        compiler_params=_params("arbitrary"), name="in_proj",
    )(x2, g, wtok, wtr, tri, ct64, st64, cT64, sT64, ct16, st16, cT16, sT16, bfg, gq, wqT, gkv, wkn, wvT)


def _attention_scratch(tq, tk=KEY_BLOCK):
    n = TILES_PER_STEP
    return [pltpu.VMEM((n, GROUP_W, N_HEADS * tq), jnp.bfloat16),
            pltpu.VMEM((n, 8, tq), jnp.float32),
            pltpu.VMEM((n, N_HEADS, V_ROWS, tq), jnp.float32),
            pltpu.VMEM((n, N_HEADS, tk, tq), jnp.float32),
            pltpu.VMEM((n, N_HEADS, tk, tq), jnp.float32),
            pltpu.VMEM((n, N_HEADS, 8, tq), jnp.float32),
            pltpu.VMEM((n, N_HEADS, 8, tq), jnp.float32)]


def _dense_attention(body, name, qT, k, vT, extra, batch, seq):
    tq = ATT_TILE * TILES_PER_STEP
    nq = seq // tq
    t = batch * seq
    in_specs = [pl.BlockSpec((qT.shape[0], tq), lambda b, i: (0, b * nq + i)),
                pl.BlockSpec((seq, k.shape[1]), lambda b, i: (b, 0)),
                pl.BlockSpec((seq // KEY_BLOCK, N_HEADS * V_ROWS, KEY_BLOCK), lambda b, i: (b, 0, 0))]
    in_specs += [pl.BlockSpec((seq // KEY_BLOCK, 1, 128), lambda b, i: (b, 0, 0)) for _ in extra]
    return pl.pallas_call(
        body, grid=(batch, nq), in_specs=in_specs,
        out_specs=pl.BlockSpec((tq, GROUP_W), lambda b, i: (b * nq + i, 0)),
        out_shape=jax.ShapeDtypeStruct((t, GROUP_W), jnp.bfloat16),
        scratch_shapes=_attention_scratch(ATT_TILE),
        compiler_params=_params("arbitrary", "arbitrary"), name=name,
    )(qT, k, vT, *extra)


def _moba_attention(qT, k, vT, kmean, batch, seq):
    tq = ATT_TILE * TILES_PER_STEP
    nq = seq // tq
    nb = seq // MOBA_BLOCK
    t = batch * seq
    return pl.pallas_call(
        _moba_kernel, grid=(batch, nq),
        in_specs=[pl.BlockSpec((GROUP_W, tq), lambda b, i: (0, b * nq + i)),
                  pl.BlockSpec((seq, GROUP_W), lambda b, i: (b, 0)),
                  pl.BlockSpec((seq // KEY_BLOCK, N_HEADS * V_ROWS, KEY_BLOCK), lambda b, i: (b, 0, 0)),
                  pl.BlockSpec((1, nb, GROUP_W), lambda b, i: (b, 0, 0))],
        out_specs=pl.BlockSpec((tq, GROUP_W), lambda b, i: (b * nq + i, 0)),
        out_shape=jax.ShapeDtypeStruct((t, GROUP_W), jnp.bfloat16),
        scratch_shapes=_attention_scratch(ATT_TILE) + [pltpu.VMEM((TILES_PER_STEP, N_HEADS * nb, ATT_TILE),
                                                                jnp.float32)],
        compiler_params=_params("arbitrary", "arbitrary"), name="moba_attention",
    )(qT, k, vT, kmean)


def _dilated_attention(q, k, v, batch, seq, dil):
    n = seq // dil
    tq = n if n <= 2 * DIL_TILE else DIL_TILE
    window = min(tq + DIL_WINDOW_STEPS, n)
    view = lambda a: a.reshape(batch, n, dil * GROUP_W)
    tiles = min(DIL_PROBLEMS, n // tq)
    classes = min(DIL_PROBLEMS // tiles, dil)
    qspec = pl.BlockSpec((1, tiles * tq, classes * GROUP_W), lambda b, r, i: (b, i, r))
    kspec = pl.BlockSpec((1, n, classes * GROUP_W), lambda b, r, i: (b, 0, r))
    shape = jax.ShapeDtypeStruct((batch, n, dil * GROUP_W), jnp.float32)
    o, lse = pl.pallas_call(
        functools.partial(_dilated_kernel, window=window, tq=tq),
        grid=(batch, dil // classes, n // (tiles * tq)),
        in_specs=[qspec, kspec, kspec], out_specs=[qspec, qspec], out_shape=[shape, shape],
        compiler_params=_params("arbitrary", "arbitrary", "arbitrary"), name=f"dilated_attention_d{dil}",
    )(view(q), view(k), view(v))
    return o.reshape(batch * n, dil * GROUP_W), lse.reshape(batch * n, dil * GROUP_W)


def _out_ffn(x2, omoba, ofox, omla, dil_outs, wout, gpost, gpre, wg, wu, wd, gpf):
    t = x2.shape[0]
    tm = ROW_TILE
    row = lambda w: pl.BlockSpec((tm, w), lambda i: (i, 0))
    (od1, l1), (od4, l4), (od16, l16) = dil_outs
    dil_specs = [pl.BlockSpec((tm // d, d * GROUP_W), lambda i: (i, 0)) for d in DILATIONS]
    in_specs = ([row(D_MODEL)] + [row(GROUP_W)] * 3 + dil_specs * 2
                + [_const_spec(wout.shape), _const_spec((1, D_MODEL)), _const_spec((1, D_MODEL)),
                   _const_spec(wg.shape), _const_spec(wu.shape), _const_spec(wd.shape), _const_spec((1, D_MODEL))])
    return pl.pallas_call(
        _out_ffn_kernel, grid=(t // tm,), in_specs=in_specs, out_specs=row(D_MODEL),
        out_shape=jax.ShapeDtypeStruct((t, D_MODEL), jnp.float32),
        scratch_shapes=[pltpu.VMEM((tm, D_MODEL), jnp.float32), pltpu.VMEM((4 * GROUP_W // 128, tm, 128), jnp.float32)],
        compiler_params=_params("arbitrary"), name="out_ffn",
    )(x2, omoba, ofox, omla, od1, od4, od16, l1, l4, l16, wout, gpost, gpre, wg, wu, wd, gpf)


def kernel(x, positions, w_in, b_forget, g_mla_q, w_mla_q_up, g_mla_kv, w_mla_kv_up, w_out, g_pre_mix, g_post_mix, w_gate, w_up, w_down, g_pre_ffn, g_post_ffn):
    batch, seq, _ = x.shape
    depth = w_in.shape[0]
    assert seq % ROW_TILE == 0 and seq % (DILATIONS[-1] * DIL_WINDOW_STEPS) == 0
    assert ROW_TILE == KEY_BLOCK
    tables = _rope_tables(positions)
    wtok, wtr, bfg, gq, wqT, gkv, wkn, wvT, wout, wg, wu, wd = _prep_layer_weights(
        w_in, b_forget, g_mla_q, w_mla_q_up, g_mla_kv, w_mla_kv_up, w_out, w_gate, w_up, w_down)
    tri = jnp.tril(jnp.ones((ROW_TILE, ROW_TILE), jnp.bfloat16))
    x2 = x.reshape(batch * seq, D_MODEL)
    for l in range(depth):
        (k_moba, kmean, qT_moba, vT_moba, k_fox, qT_fox, vT_fox, cbase, *dil_qkv,
         k_mla, qT_mla, vT_mla) = _in_proj(x2, g_pre_mix[l][None], wtok[l], wtr[l], tri, tables, bfg[l], gq[l],
                                          wqT[l], gkv[l], wkn[l], wvT[l], seq)
        nd = len(DILATIONS)
        o_moba = _moba_attention(qT_moba, k_moba, vT_moba, kmean.reshape(batch, seq // MOBA_BLOCK, GROUP_W),
                                 batch, seq)
        o_fox = _dense_attention(_fox_kernel, "fox_attention", qT_fox, k_fox, vT_fox, [cbase], batch, seq)
        o_mla = _dense_attention(_mla_kernel, "mla_attention", qT_mla, k_mla, vT_mla, [], batch, seq)
        dil_outs = [_dilated_attention(dil_qkv[n], dil_qkv[nd + n], dil_qkv[2 * nd + n], batch, seq, d)
                    for n, d in enumerate(DILATIONS)]
        x2 = _out_ffn(x2, o_moba, o_fox, o_mla, dil_outs, wout[l], g_post_mix[l][None], g_pre_ffn[l][None],
                      wg[l], wu[l], wd[l], g_post_ffn[l][None])
    return x2.reshape(batch, seq, D_MODEL)
```

```python
import functools

import numpy as np
import jax
import jax.numpy as jnp
from jax import lax
from jax.experimental import pallas as pl
from jax.experimental.pallas import tpu as pltpu

D_MODEL = 1024
HEAD_DIM = 64
N_HEADS = 4
GROUP_W = N_HEADS * HEAD_DIM
MOBA_BLOCK = 256
MOBA_TOPK = 3
MLA_Q_RANK = 192
MLA_KV_RANK = 128
MLA_NOPE = 64
MLA_ROPE = 32
DIL_WINDOW_STEPS = 128
DILATIONS = (1, 4, 16)
ROPE_THETA = 10000.0
RMS_EPS = 1e-6
D_FF = 2816
FF_CHUNK = 256
V_ROWS = 80
LOG2E = 1.4426950408889634
MASKED_LOGIT = -1e30

ROW_TILE = 512
ATT_TILE = 512
KEY_BLOCK = 512
TILES_PER_STEP = 2
IN_PROJ_SUBTILES = 2
SLAB = MOBA_BLOCK
DIL_TILE = 256
DIL_PROBLEMS = 8
VMEM_LIMIT = 56 * 1024 * 1024

_C_KMOBA, _C_KFOX, _C_QDIL, _C_KDIL, _C_VDIL, _C_CQ = 0, 256, 512, 768, 1024, 1280
_C_CKV, _C_X, _C_Y, _N_TOK = 1536, 1664, 1792, 1920
_FG_LANE = 32

_NT = (((1,), (1,)), ((), ()))


def _dot(a, b):
    return jnp.dot(a, b, preferred_element_type=jnp.float32)


def _dot_nt(a, b):
    return lax.dot_general(a, b, _NT, preferred_element_type=jnp.float32)


def _rms_scale(v, n):
    return lax.rsqrt(jnp.sum(v * v, axis=-1, keepdims=True) * (1.0 / n) + RMS_EPS)


_IN_PROJ_SPLIT = "r----rrllrrll------" + "rblbrlbb" + "rrrrrrrrr" + "rlb" + "-b"


def _in_proj_kernel(*refs, tiles_per_seq):
    for sub in range(IN_PROJ_SUBTILES):
        views = []
        for ref, kind in zip(refs, _IN_PROJ_SPLIT, strict=True):
            if kind == "-":
                views.append(ref)
                continue
            axis = 1 if kind == "l" else 0
            n = ref.shape[axis] // IN_PROJ_SUBTILES
            idx = (slice(None),) * axis + (slice(sub * n, (sub + 1) * n),)
            views.append(ref.at[idx])
        _in_proj_tile(pl.program_id(0) * IN_PROJ_SUBTILES + sub, *views, tiles_per_seq=tiles_per_seq)


def _in_proj_tile(tile, x_ref, g_ref, wtok_ref, wtr_ref, tri_ref, ct64_ref, st64_ref, cT64_ref, sT64_ref,
                  ct16_ref, st16_ref, cT16_ref, sT16_ref, bfg_ref, gq_ref, wqT_ref, gkv_ref, wkn_ref, wvT_ref,
                  kmoba_ref, kmean_ref, qTmoba_ref, vTmoba_ref, kfox_ref, qTfox_ref, vTfox_ref, cbase_ref,
                  q1_ref, q4_ref, q16_ref, k1_ref, k4_ref, k16_ref, v1_ref, v4_ref, v16_ref,
                  kmla_ref, qTmla_ref, vTmla_ref, carry_ref, dscr_ref, *, tiles_per_seq):
    bf16 = jnp.bfloat16
    qdil_refs, kdil_refs, vdil_refs = (q1_ref, q4_ref, q16_ref), (k1_ref, k4_ref, k16_ref), (v1_ref, v4_ref, v16_ref)
    x = x_ref[...]
    hb = (x * _rms_scale(x, D_MODEL) * g_ref[...]).astype(bf16)
    tm = x.shape[0]

    def tok(c0, w):
        return _dot(hb, wtok_ref[:, c0:c0 + w])

    def rope_tok(z):
        c, s = ct64_ref[...], st64_ref[...]
        x1, x2 = z[:, :128], z[:, 128:]
        return jnp.concatenate([x1 * c - x2 * s, x2 * c + x1 * s], axis=1)

    def with_ones(vT):
        ones = jnp.ones((V_ROWS - HEAD_DIM, tm), bf16)
        parts = []
        for h in range(N_HEADS):
            parts += [vT[h * HEAD_DIM:(h + 1) * HEAD_DIM].astype(bf16), ones]
        return jnp.concatenate(parts, axis=0)

    def rope_tr(zT):
        c, s = cT64_ref[...], sT64_ref[...]
        x1, x2 = zT[:128], zT[128:]
        return jnp.concatenate([x1 * c - x2 * s, x2 * c + x1 * s], axis=0)

    k_moba = rope_tok(tok(_C_KMOBA, GROUP_W))
    kmoba_ref[...] = k_moba.astype(bf16)
    for blk in range(tm // MOBA_BLOCK):
        kmean_ref[0, blk:blk + 1, :] = jnp.mean(k_moba[blk * MOBA_BLOCK:(blk + 1) * MOBA_BLOCK], axis=0, keepdims=True)
    qTmoba_ref[...] = rope_tr(_dot_nt(wtr_ref[0:256, :], hb)).astype(bf16)
    def store_value_blocks(ref, vT):
        for blk in range(tm // KEY_BLOCK):
            ref[blk] = vT[:, blk * KEY_BLOCK:(blk + 1) * KEY_BLOCK]

    store_value_blocks(vTmoba_ref, with_ones(_dot_nt(wtr_ref[256:512, :], hb)))

    qTfox_ref[...] = _dot_nt(wtr_ref[512:768, :], hb).astype(bf16)
    store_value_blocks(vTfox_ref, with_ones(_dot_nt(wtr_ref[768:1024, :], hb)))

    def emit_dilated(refs, slot, z):
        refs[0][...] = z.astype(bf16)
        for c in range(GROUP_W // 128):
            dscr_ref[slot + c] = z[:, c * 128:(c + 1) * 128]
        for ref, dil in zip(refs[1:], DILATIONS[1:]):
            for rho in range(dil):
                for c in range(GROUP_W // 128):
                    rows = dscr_ref[slot + c, pl.ds(rho, tm // dil, stride=dil), :]
                    ref[:, rho * GROUP_W + c * 128:rho * GROUP_W + (c + 1) * 128] = rows.astype(bf16)

    emit_dilated(qdil_refs, 0, rope_tok(tok(_C_QDIL, GROUP_W)))
    emit_dilated(kdil_refs, 2, rope_tok(tok(_C_KDIL, GROUP_W)))
    emit_dilated(vdil_refs, 4, tok(_C_VDIL, GROUP_W))

    xy = tok(_C_X, 256)
    xblk = xy[:, :128]
    yblk = xy[:, 128:]
    fg = xblk + bfg_ref[...]
    logf = jnp.minimum(fg, 0.0) - jnp.log1p(jnp.exp(-jnp.abs(fg)))
    lane = lax.broadcasted_iota(jnp.int32, logf.shape, 1)
    is_gate = (lane >= _FG_LANE) & (lane < _FG_LANE + N_HEADS)

    def three_pieces(v):
        v = jnp.where(is_gate, v, 0.0)
        p1 = v.astype(bf16).astype(jnp.float32)
        p2 = (v - p1).astype(bf16).astype(jnp.float32)
        p3 = (v - p1 - p2).astype(bf16).astype(jnp.float32)
        return p1 + pltpu.roll(p2, 8, 1) + pltpu.roll(p3, 16, 1)

    @pl.when(tile % tiles_per_seq == 0)
    def _():
        carry_ref[...] = jnp.zeros_like(carry_ref)

    part = _dot(tri_ref[...], three_pieces(logf).astype(bf16))
    cum = part + pltpu.roll(part, 128 - 8, 1) + pltpu.roll(part, 128 - 16, 1) + carry_ref[...]
    carry_ref[...] = cum[tm - 1:tm, :]
    cbase_ref[0] = cum[0:1, :]

    dcols = three_pieces((cum[0:1, :] - cum) * LOG2E)
    kf = tok(_C_KFOX, GROUP_W)
    kfox_ref[...] = jnp.concatenate([kf[:, :128], dcols, kf[:, 128:], dcols], axis=1).astype(bf16)

    zcq = tok(_C_CQ, 256)
    cq = (zcq * _rms_scale(zcq, MLA_Q_RANK) * gq_ref[...]).astype(bf16)
    qcT = _dot_nt(wqT_ref[...], cq)
    c16, s16 = cT16_ref[...], sT16_ref[...]
    r1h, r2h = qcT[256:320], qcT[320:384]
    qT = jnp.concatenate([qcT[0:256], r1h * c16 - r2h * s16, r2h * c16 + r1h * s16], axis=0)
    qTmla_ref[...] = (qT * ((MLA_NOPE + MLA_ROPE) ** -0.5 * LOG2E)).astype(bf16)

    zckv = tok(_C_CKV, MLA_KV_RANK)
    ckv = (zckv * _rms_scale(zckv, MLA_KV_RANK) * gkv_ref[...]).astype(bf16)
    kn = _dot(ckv, wkn_ref[...])
    krope = xblk * ct16_ref[...] + yblk * st16_ref[...]
    kmla_ref[...] = jnp.concatenate([kn[:, :128], krope, kn[:, 128:], krope], axis=1).astype(bf16)
    store_value_blocks(vTmla_ref, with_ones(_dot_nt(wvT_ref[...], ckv)))


def _softmax_update(parts, vb, m_ref, acc_ref, h, offsets=None, maxes=None):
    if offsets is None:
        offsets = [None] * len(parts)
    if maxes is None:
        maxes = [jnp.max(s, axis=0, keepdims=True) for s in parts]
    m_old = m_ref[h:h + 1, :]
    blk_max = None
    for mx, off in zip(maxes, offsets):
        mx = mx if off is None else mx + off
        blk_max = mx if blk_max is None else jnp.maximum(blk_max, mx)
    m_new = jnp.maximum(m_old, blk_max)
    ps = []
    for s, off in zip(parts, offsets):
        shift = m_new if off is None else m_new - off
        ps.append(jnp.exp2(s - shift).astype(jnp.bfloat16))
    p = ps[0] if len(ps) == 1 else jnp.concatenate(ps, axis=0)
    acc_ref[h] = jnp.exp2(m_old - m_new) * acc_ref[h] + _dot(vb, p)
    m_ref[h:h + 1, :] = m_new


def _store_scores(buf, first_head, s):
    s_ref, mx_ref = buf
    tq = s_ref.shape[2]
    for k in range(s.shape[1] // tq):
        sh = s[:, k * tq:(k + 1) * tq]
        s_ref[first_head + k] = sh
        for n in range(sh.shape[0] // SLAB):
            mx_ref[first_head + k, n:n + 1, :] = jnp.max(sh[n * SLAB:(n + 1) * SLAB], axis=0, keepdims=True)


def _slabs(buf, h, tq):
    s_ref, mx_ref = buf
    n_slabs = s_ref.shape[1] // SLAB
    return ([s_ref[h, n * SLAB:(n + 1) * SLAB, :] for n in range(n_slabs)],
            [mx_ref[h, n:n + 1, :] for n in range(n_slabs)])


def _diagonal_masks(tq, tk):
    key = lax.broadcasted_iota(jnp.int32, (tk, tq), 0)
    qry = lax.broadcasted_iota(jnp.int32, (tk, tq), 1)
    return [key + r * tk <= qry for r in range(tq // tk)]


def _init_state(m_ref, acc_ref):
    m_ref[...] = jnp.full(m_ref.shape, -jnp.inf, jnp.float32)
    acc_ref[...] = jnp.zeros(acc_ref.shape, jnp.float32)


def _finalize(acc_ref, o_ref):
    outs = [acc_ref[h, 0:HEAD_DIM, :] / acc_ref[h, HEAD_DIM:HEAD_DIM + 1, :] for h in range(N_HEADS)]
    o_ref[...] = jnp.concatenate(outs, axis=0).T.astype(o_ref.dtype)


def _vblock(vT_ref, j, h):
    return vT_ref[j, h * V_ROWS:(h + 1) * V_ROWS, :]


def _run_problems(problems, last):
    for first, buf_a, _, scores_into, _ in problems:
        for h in range(N_HEADS):
            scores_into(buf_a, first, h)
    for h in range(N_HEADS):
        for first, buf_a, buf_b, scores_into, consume in problems:
            scores_into(buf_b, 0, h)
            consume(buf_a, first, True, (h,))

    def pair(p, carry, last_trip=False):
        j0 = 2 * p
        for h in range(N_HEADS):
            for _, buf_a, buf_b, scores_into, consume in problems:
                scores_into(buf_a, j0 + 1, h)
                consume(buf_b, j0, False, (h,))
        for h in range(N_HEADS):
            for t, (_, buf_a, buf_b, scores_into, consume) in enumerate(problems):
                if not (last_trip and t % 2 == 0):
                    scores_into(buf_b, jnp.minimum(j0 + 2, last), h)
                consume(buf_a, j0 + 1, False, (h,))
        return carry

    trips = problems[0][0] // 2
    lax.fori_loop(0, trips - 1, pair, 0)

    @pl.when(trips > 0)
    def _():
        pair(trips - 1, 0, last_trip=True)

    for t, (first, _, buf_b, _, consume) in enumerate(problems):
        if t % 2 == 1:
            consume(buf_b, first - 1, False, range(N_HEADS))


def _key_rows(k_ref, j, tk):
    return k_ref[pl.ds(pl.multiple_of(j * tk, tk), tk), :]


def _causal_slabs(parts, causal):
    return [jnp.where(causal[n * SLAB:(n + 1) * SLAB], s, -jnp.inf) for n, s in enumerate(parts)]


def _two_tile_scores(buf, k_ref, qcat_ref, j, h, tk, tq):
    kb = k_ref[pl.ds(pl.multiple_of(j * tk, tk), tk), 256 * (h // 2):256 * (h // 2 + 1)]
    _store_scores(buf, h, _dot(kb, qcat_ref[:, h * tq:(h + 1) * tq]))


def _tile_views(t, tq, qT_ref, o_ref, scratch):
    return (qT_ref.at[:, t * tq:(t + 1) * tq], o_ref.at[t * tq:(t + 1) * tq, :]) + tuple(r.at[t] for r in scratch)


def _fox_kernel(qT_ref, k_ref, vT_ref, cbase_ref, o_ref, *scratch):
    tq = qT_ref.shape[1] // TILES_PER_STEP
    tk = scratch[3].shape[2]
    assert tq == tk
    causal = _diagonal_masks(tq, tk)[0]
    zeros64 = jnp.zeros((HEAD_DIM, tq), jnp.bfloat16)
    row = lax.broadcasted_iota(jnp.int32, (128, tq), 0)

    def problem(t):
        q_ref, _, qcat_ref, m_ref, acc_ref, sa_ref, sb_ref, mxa_ref, mxb_ref = _tile_views(t, tq, qT_ref, o_ref, scratch)
        first = pl.program_id(1) * TILES_PER_STEP + t
        for h in range(N_HEADS):
            qh = q_ref[h * HEAD_DIM:(h + 1) * HEAD_DIM, :]
            pick = (row == _FG_LANE + h) | (row == _FG_LANE + 8 + h) | (row == _FG_LANE + 16 + h)
            rows = ([qh, zeros64] if h % 2 == 0 else [zeros64, qh]) + [pick.astype(jnp.bfloat16)]
            qcat_ref[:, h * tq:(h + 1) * tq] = jnp.concatenate(rows, axis=0)
        _init_state(m_ref, acc_ref)

        def scores_into(buf, j, h):
            _two_tile_scores(buf, k_ref, qcat_ref, j, h, tk, tq)

        def consume(buf, j, own, heads):
            offs = (cbase_ref[first] - cbase_ref[j]) * LOG2E
            for h in heads:
                parts, maxes = _slabs(buf, h, tq)
                if own:
                    parts, maxes = _causal_slabs(parts, causal), None
                off = offs[:, _FG_LANE + h:_FG_LANE + h + 1]
                _softmax_update(parts, _vblock(vT_ref, j, h), m_ref, acc_ref, h, [off] * len(parts), maxes)

        return first, (sa_ref, mxa_ref), (sb_ref, mxb_ref), scores_into, consume

    _run_problems([problem(t) for t in range(TILES_PER_STEP)], k_ref.shape[0] // tk - 1)
    for t in range(TILES_PER_STEP):
        views = _tile_views(t, tq, qT_ref, o_ref, scratch)
        _finalize(views[4], views[1])


def _mla_kernel(qT_ref, k_ref, vT_ref, o_ref, *scratch):
    tq = qT_ref.shape[1] // TILES_PER_STEP
    tk = scratch[3].shape[2]
    assert tq == tk
    causal = _diagonal_masks(tq, tk)[0]
    zeros64 = jnp.zeros((MLA_NOPE, tq), jnp.bfloat16)
    zeros96 = jnp.zeros((256 - 128 - MLA_ROPE, tq), jnp.bfloat16)

    def problem(t):
        q_ref, _, qcat_ref, m_ref, acc_ref, sa_ref, sb_ref, mxa_ref, mxb_ref = _tile_views(t, tq, qT_ref, o_ref, scratch)
        for h in range(N_HEADS):
            qn = q_ref[h * MLA_NOPE:(h + 1) * MLA_NOPE, :]
            qr1 = q_ref[256 + 16 * h:256 + 16 * (h + 1), :]
            qr2 = q_ref[320 + 16 * h:320 + 16 * (h + 1), :]
            nope = [qn, zeros64] if h % 2 == 0 else [zeros64, qn]
            qcat_ref[:, h * tq:(h + 1) * tq] = jnp.concatenate(nope + [qr1, qr2, zeros96], axis=0)
        _init_state(m_ref, acc_ref)

        def scores_into(buf, j, h):
            _two_tile_scores(buf, k_ref, qcat_ref, j, h, tk, tq)

        def consume(buf, j, own, heads):
            for h in heads:
                parts, maxes = _slabs(buf, h, tq)
                if own:
                    parts, maxes = _causal_slabs(parts, causal), None
                _softmax_update(parts, _vblock(vT_ref, j, h), m_ref, acc_ref, h, None, maxes)

        return (pl.program_id(1) * TILES_PER_STEP + t, (sa_ref, mxa_ref), (sb_ref, mxb_ref), scores_into, consume)

    _run_problems([problem(t) for t in range(TILES_PER_STEP)], k_ref.shape[0] // tk - 1)
    for t in range(TILES_PER_STEP):
        views = _tile_views(t, tq, qT_ref, o_ref, scratch)
        _finalize(views[4], views[1])


def _moba_kernel(qT_ref, k_ref, vT_ref, km_ref, o_ref, *scratch):
    tq = qT_ref.shape[1] // TILES_PER_STEP
    tk = scratch[3].shape[2]
    assert tq == tk
    nb = km_ref.shape[1]
    per = tk // SLAB
    causal = _diagonal_masks(tq, tk)[0]
    km = km_ref[0].astype(jnp.bfloat16)
    row = lax.broadcasted_iota(jnp.int32, (GROUP_W, tq), 0)
    blk = lax.broadcasted_iota(jnp.int32, (nb, tq), 0)
    neg_inf = jnp.float32(-jnp.inf)

    def problem(t):
        (q_ref, _, qcat_ref, m_ref, acc_ref, sa_ref, sb_ref, mxa_ref, mxb_ref,
         bias_ref) = _tile_views(t, tq, qT_ref, o_ref, scratch)
        first = pl.program_id(1) * TILES_PER_STEP + t
        q = q_ref[...]
        qblk = first * (tq // MOBA_BLOCK) + lax.broadcasted_iota(jnp.int32, (1, tq), 1) // MOBA_BLOCK
        for h in range(N_HEADS):
            in_head = (((row >= 32 * h) & (row < 32 * (h + 1)))
                       | ((row >= 128 + 32 * h) & (row < 128 + 32 * (h + 1))))
            qm = jnp.where(in_head, q, jnp.zeros_like(q))
            qcat_ref[:, h * tq:(h + 1) * tq] = qm
            g = jnp.where(blk < qblk, _dot(km, qm), neg_inf)
            sel = jnp.zeros((nb, tq), jnp.bool_)
            for _ in range(MOBA_TOPK):
                mx = jnp.max(g, axis=0, keepdims=True)
                cand = jnp.where((g == mx) & (mx > neg_inf), blk, nb)
                chosen = blk == jnp.min(cand, axis=0, keepdims=True)
                sel = sel | chosen
                g = jnp.where(chosen, neg_inf, g)
            bias_ref[h * nb:(h + 1) * nb, :] = jnp.where(sel, 0.0, MASKED_LOGIT)
        _init_state(m_ref, acc_ref)

        def scores_into(buf, j, h):
            _store_scores(buf, h, _dot(_key_rows(k_ref, j, tk), qcat_ref[:, h * tq:(h + 1) * tq]))

        def consume(buf, j, own, heads):
            for h in heads:
                parts, maxes = _slabs(buf, h, tq)
                gates = [bias_ref[pl.ds(h * nb + per * j + n, 1), :] for n in range(per)]
                if own:
                    gates = [jnp.where(per * j + n < qblk, g, 0.0) for n, g in enumerate(gates)]
                    parts, maxes = _causal_slabs(parts, causal), None
                _softmax_update(parts, _vblock(vT_ref, j, h), m_ref, acc_ref, h, gates, maxes)

        return first, (sa_ref, mxa_ref), (sb_ref, mxb_ref), scores_into, consume

    _run_problems([problem(t) for t in range(TILES_PER_STEP)], k_ref.shape[0] // tk - 1)
    for t in range(TILES_PER_STEP):
        views = _tile_views(t, tq, qT_ref, o_ref, scratch)
        _finalize(views[4], views[1])


def _dilated_kernel(q_ref, k_ref, v_ref, o_ref, lse_ref, *, window, tq):
    i = pl.program_id(2)
    n = k_ref.shape[1]
    tiles = q_ref.shape[1] // tq
    lane = lax.broadcasted_iota(jnp.int32, (tq, GROUP_W), 1)
    for c in range(q_ref.shape[2] // GROUP_W):
        cls = slice(c * GROUP_W, (c + 1) * GROUP_W)
        for g in range(tiles):
            rows = slice(g * tq, (g + 1) * tq)
            q = q_ref[0, rows, cls]
            a = (i * tiles + g) * tq
            ks = jnp.clip(a - DIL_WINDOW_STEPS, 0, n - window)
            ks = pl.multiple_of(ks, DIL_WINDOW_STEPS)
            kw = k_ref[0, pl.ds(ks, window), cls]
            vw = v_ref[0, pl.ds(ks, window), cls]
            jq = a + lax.broadcasted_iota(jnp.int32, (tq, window), 0)
            jk = ks + lax.broadcasted_iota(jnp.int32, (tq, window), 1)
            band = (jq - jk >= 0) & (jq - jk <= DIL_WINDOW_STEPS)
            o = jnp.zeros((tq, GROUP_W), jnp.float32)
            lse = jnp.zeros((tq, GROUP_W), jnp.float32)
            for h in range(N_HEADS):
                in_head = (((lane >= 32 * h) & (lane < 32 * (h + 1)))
                           | ((lane >= 128 + 32 * h) & (lane < 128 + 32 * (h + 1))))
                qm = jnp.where(in_head, q, jnp.zeros_like(q))
                s = jnp.where(band, _dot_nt(qm, kw), -jnp.inf)
                m = jnp.max(s, axis=1, keepdims=True)
                p = jnp.exp2(s - m)
                l = jnp.sum(p, axis=1, keepdims=True)
                oh = _dot(p.astype(jnp.bfloat16), vw) / l
                out_lanes = (lane >= h * HEAD_DIM) & (lane < (h + 1) * HEAD_DIM)
                o = jnp.where(out_lanes, oh, o)
                lse = jnp.where(out_lanes, m + jnp.log2(l), lse)
            o_ref[0, rows, cls] = o
            lse_ref[0, rows, cls] = lse


def _out_ffn_kernel(x_ref, omoba_ref, ofox_ref, omla_ref, od1_ref, od4_ref, od16_ref, l1_ref, l4_ref, l16_ref,
                    wout_ref, gpost_ref, gpre_ref, wg_ref, wu_ref, wd_ref, gpf_ref, out_ref, acc_ref,
                    dscr_ref):
    bf16 = jnp.bfloat16
    tm = x_ref.shape[0]

    def token_order(ref, slot, dil):
        halves = GROUP_W // 128
        for rho in range(dil):
            for c in range(halves):
                lanes = slice(rho * GROUP_W + c * 128, rho * GROUP_W + (c + 1) * 128)
                dscr_ref[slot + c, pl.ds(rho, tm // dil, stride=dil), :] = ref[:, lanes]
        return jnp.concatenate([dscr_ref[slot + c] for c in range(halves)], axis=1)

    l1, od1 = l1_ref[...], od1_ref[...]
    l4, od4 = token_order(l4_ref, 0, 4), token_order(od4_ref, 2, 4)
    l16, od16 = token_order(l16_ref, 4, 16), token_order(od16_ref, 6, 16)
    m = jnp.maximum(jnp.maximum(l1, l4), l16)
    e1, e4, e16 = jnp.exp2(l1 - m), jnp.exp2(l4 - m), jnp.exp2(l16 - m)
    odil = (e1 * od1 + e4 * od4 + e16 * od16) / (e1 + e4 + e16)
    y = (_dot(omoba_ref[...], wout_ref[0]) + _dot(ofox_ref[...], wout_ref[1])
         + _dot(omla_ref[...], wout_ref[2]) + _dot(odil.astype(bf16), wout_ref[3]))
    x1 = x_ref[...] + y * _rms_scale(y, D_MODEL) * gpost_ref[...]
    hb = (x1 * _rms_scale(x1, D_MODEL) * gpre_ref[...]).astype(bf16)
    acc_ref[...] = jnp.zeros_like(acc_ref)
    for c in range(D_FF // FF_CHUNK):
        cols = slice(c * FF_CHUNK, (c + 1) * FF_CHUNK)
        g = _dot(hb, wg_ref[:, cols])
        u = _dot(hb, wu_ref[:, cols])
        f = (g * jax.nn.sigmoid(g) * u).astype(bf16)
        acc_ref[...] += _dot(f, wd_ref[cols, :])
    f = acc_ref[...]
    out_ref[...] = x1 + f * _rms_scale(f, D_MODEL) * gpf_ref[...]


def _rope_tables(positions):
    t = positions.reshape(-1).astype(jnp.float32)

    def tab(dim):
        inv = ROPE_THETA ** (-jnp.arange(0, dim, 2, dtype=jnp.float32) / dim)
        ang = t[:, None] * inv
        return jnp.cos(ang), jnp.sin(ang)

    c64, s64 = tab(HEAD_DIM)
    c16, s16 = tab(MLA_ROPE)
    ct64, st64 = jnp.tile(c64, (1, 4)), jnp.tile(s64, (1, 4))
    pad = jnp.zeros((t.shape[0], 128 - MLA_ROPE), jnp.float32)
    ct16 = jnp.concatenate([c16, c16, pad], axis=1)
    st16 = jnp.concatenate([s16, s16, pad], axis=1)
    cT16, sT16 = jnp.tile(c16, (1, 4)).T, jnp.tile(s16, (1, 4)).T
    return ct64, st64, ct64.T, st64.T, ct16, st16, cT16, sT16


_HALF_PERM = np.array([h * 64 + half * 32 + j for half in (0, 1) for h in range(4) for j in range(32)])
_QROPE_ROWS = np.array([h * 96 + 64 + half * 16 + j for half in (0, 1) for h in range(4) for j in range(16)])
_QNOPE_ROWS = np.array([h * 96 + j for h in range(4) for j in range(64)])
_KNOPE_COLS = np.array([h * 128 + j for h in range(4) for j in range(64)])
_VMLA_COLS = np.array([h * 128 + 64 + j for h in range(4) for j in range(64)])


def _prep_layer_weights(w_in, b_forget, g_mla_q, w_mla_q_up, g_mla_kv, w_mla_kv_up, w_out, w_gate, w_up, w_down):
    bf16 = jnp.bfloat16
    depth = w_in.shape[0]
    sl = lambda a, b: w_in[:, :, a:b]
    scale = HEAD_DIM ** -0.5 * LOG2E
    moba_q, moba_k, moba_v = sl(0, 256)[..., _HALF_PERM] * scale, sl(256, 512)[..., _HALF_PERM], sl(512, 768)
    fox_q, fox_k, fox_v = sl(768, 1024) * scale, sl(1024, 1280), sl(1280, 1536)
    fg, cq, ckv, kr = sl(1536, 1540), sl(1540, 1732), sl(1732, 1860), sl(1860, 1892)
    dil_q, dil_k, dil_v = sl(1892, 2148)[..., _HALF_PERM] * scale, sl(2148, 2404)[..., _HALF_PERM], sl(2404, 2660)
    kr_rot = jnp.concatenate([-kr[..., 16:], kr[..., :16]], axis=-1)
    z = lambda n: jnp.zeros((depth, D_MODEL, n), w_in.dtype)
    wtok = jnp.concatenate([moba_k, fox_k, dil_q, dil_k, dil_v, cq, z(256 - MLA_Q_RANK), ckv,
                            kr, fg, z(128 - 36), kr_rot, z(96)], axis=-1).astype(bf16)
    wtr = jnp.swapaxes(jnp.concatenate([moba_q, moba_v, fox_q, fox_v], axis=-1), 1, 2).astype(bf16)
    bfg = jnp.zeros((depth, 1, 128), jnp.float32).at[:, 0, _FG_LANE:_FG_LANE + N_HEADS].set(b_forget)
    gq = jnp.pad(g_mla_q, ((0, 0), (0, 256 - MLA_Q_RANK)))[:, None, :]
    wq_rows = jnp.swapaxes(w_mla_q_up, 1, 2)
    wqT = jnp.concatenate([wq_rows[:, _QNOPE_ROWS], wq_rows[:, _QROPE_ROWS]], axis=1)
    wqT = jnp.pad(wqT, ((0, 0), (0, 0), (0, 256 - MLA_Q_RANK))).astype(bf16)
    gkv = g_mla_kv[:, None, :]
    wkn = w_mla_kv_up[:, :, _KNOPE_COLS].astype(bf16)
    wvT = jnp.swapaxes(w_mla_kv_up[:, :, _VMLA_COLS], 1, 2).astype(bf16)
    wout = w_out.reshape(depth, 4, GROUP_W, D_MODEL).astype(bf16)
    wg, wu, wd = w_gate.astype(bf16), w_up.astype(bf16), w_down.astype(bf16)
    return wtok, wtr, bfg, gq, wqT, gkv, wkn, wvT, wout, wg, wu, wd


def _const_spec(shape):
    return pl.BlockSpec(shape, lambda *_: (0,) * len(shape))


def _params(*sem):
    return pltpu.CompilerParams(dimension_semantics=sem, vmem_limit_bytes=VMEM_LIMIT)


def _in_proj(x2, g, wtok, wtr, tri, tables, bfg, gq, wqT, gkv, wkn, wvT, seq):
    t = x2.shape[0]
    sub = IN_PROJ_SUBTILES
    tm = ROW_TILE * sub
    nt = t // ROW_TILE
    bf16, f32 = jnp.bfloat16, jnp.float32
    ct64, st64, cT64, sT64, ct16, st16, cT16, sT16 = tables
    tok_spec = lambda w: pl.BlockSpec((tm, w), lambda i: (i, 0))
    tr_spec = lambda r: pl.BlockSpec((r, tm), lambda i: (0, i))
    blk3 = lambda n, r, c: pl.BlockSpec((n, r, c), lambda i: (i, 0, 0))
    mb = ROW_TILE // MOBA_BLOCK
    ab = tm // KEY_BLOCK
    vrows = N_HEADS * V_ROWS
    vt_shape = jax.ShapeDtypeStruct((t // KEY_BLOCK, vrows, KEY_BLOCK), bf16)
    in_specs = [tok_spec(D_MODEL), _const_spec((1, D_MODEL)), _const_spec(wtok.shape), _const_spec(wtr.shape),
                _const_spec(tri.shape), tok_spec(128), tok_spec(128), tr_spec(128), tr_spec(128),
                tok_spec(128), tok_spec(128), tr_spec(64), tr_spec(64), _const_spec((1, 128)),
                _const_spec((1, 256)), _const_spec(wqT.shape), _const_spec((1, 128)), _const_spec(wkn.shape),
                _const_spec(wvT.shape)]
    out_shape = [
        jax.ShapeDtypeStruct((t, GROUP_W), bf16),
        jax.ShapeDtypeStruct((nt, mb, GROUP_W), f32),
        jax.ShapeDtypeStruct((GROUP_W, t), bf16),
        vt_shape,
        jax.ShapeDtypeStruct((t, 512), bf16),
        jax.ShapeDtypeStruct((GROUP_W, t), bf16),
        vt_shape,
        jax.ShapeDtypeStruct((nt, 1, 128), f32),
        *[jax.ShapeDtypeStruct((t // d, d * GROUP_W), bf16) for d in DILATIONS] * 3,
        jax.ShapeDtypeStruct((t, 512), bf16),
        jax.ShapeDtypeStruct((384, t), bf16),
        vt_shape,
    ]
    vt_spec = blk3(ab, vrows, KEY_BLOCK)
    out_specs = [tok_spec(GROUP_W), blk3(sub, mb, GROUP_W), tr_spec(GROUP_W), vt_spec,
                 tok_spec(512), tr_spec(GROUP_W), vt_spec, blk3(sub, 1, 128),
                 *[pl.BlockSpec((tm // d, d * GROUP_W), lambda i: (i, 0)) for d in DILATIONS] * 3,
                 tok_spec(512), tr_spec(384), vt_spec]
    return pl.pallas_call(
        functools.partial(_in_proj_kernel, tiles_per_seq=seq // ROW_TILE),
        grid=(t // tm,), in_specs=in_specs, out_specs=out_specs, out_shape=out_shape,
        scratch_shapes=[pltpu.VMEM((1, 128), f32), pltpu.VMEM((sub * 3 * GROUP_W // 128, ROW_TILE, 128), f32)],
        compiler_params=_params("arbitrary"), name="in_proj",
    )(x2, g, wtok, wtr, tri, ct64, st64, cT64, sT64, ct16, st16, cT16, sT16, bfg, gq, wqT, gkv, wkn, wvT)


def _attention_scratch(tq, tk=KEY_BLOCK):
    n = TILES_PER_STEP
    return [pltpu.VMEM((n, GROUP_W, N_HEADS * tq), jnp.bfloat16),
            pltpu.VMEM((n, 8, tq), jnp.float32),
            pltpu.VMEM((n, N_HEADS, V_ROWS, tq), jnp.float32),
            pltpu.VMEM((n, N_HEADS, tk, tq), jnp.float32),
            pltpu.VMEM((n, N_HEADS, tk, tq), jnp.float32),
            pltpu.VMEM((n, N_HEADS, 8, tq), jnp.float32),
            pltpu.VMEM((n, N_HEADS, 8, tq), jnp.float32)]


def _dense_attention(body, name, qT, k, vT, extra, batch, seq):
    tq = ATT_TILE * TILES_PER_STEP
    nq = seq // tq
    t = batch * seq
    in_specs = [pl.BlockSpec((qT.shape[0], tq), lambda b, i: (0, b * nq + i)),
                pl.BlockSpec((seq, k.shape[1]), lambda b, i: (b, 0)),
                pl.BlockSpec((seq // KEY_BLOCK, N_HEADS * V_ROWS, KEY_BLOCK), lambda b, i: (b, 0, 0))]
    in_specs += [pl.BlockSpec((seq // KEY_BLOCK, 1, 128), lambda b, i: (b, 0, 0)) for _ in extra]
    return pl.pallas_call(
        body, grid=(batch, nq), in_specs=in_specs,
        out_specs=pl.BlockSpec((tq, GROUP_W), lambda b, i: (b * nq + i, 0)),
        out_shape=jax.ShapeDtypeStruct((t, GROUP_W), jnp.bfloat16),
        scratch_shapes=_attention_scratch(ATT_TILE),
        compiler_params=_params("arbitrary", "arbitrary"), name=name,
    )(qT, k, vT, *extra)


def _moba_attention(qT, k, vT, kmean, batch, seq):
    tq = ATT_TILE * TILES_PER_STEP
    nq = seq // tq
    nb = seq // MOBA_BLOCK
    t = batch * seq
    return pl.pallas_call(
        _moba_kernel, grid=(batch, nq),
        in_specs=[pl.BlockSpec((GROUP_W, tq), lambda b, i: (0, b * nq + i)),
                  pl.BlockSpec((seq, GROUP_W), lambda b, i: (b, 0)),
                  pl.BlockSpec((seq // KEY_BLOCK, N_HEADS * V_ROWS, KEY_BLOCK), lambda b, i: (b, 0, 0)),
                  pl.BlockSpec((1, nb, GROUP_W), lambda b, i: (b, 0, 0))],
        out_specs=pl.BlockSpec((tq, GROUP_W), lambda b, i: (b * nq + i, 0)),
        out_shape=jax.ShapeDtypeStruct((t, GROUP_W), jnp.bfloat16),
        scratch_shapes=_attention_scratch(ATT_TILE) + [pltpu.VMEM((TILES_PER_STEP, N_HEADS * nb, ATT_TILE),
                                                                jnp.float32)],
        compiler_params=_params("arbitrary", "arbitrary"), name="moba_attention",
    )(qT, k, vT, kmean)


def _dilated_attention(q, k, v, batch, seq, dil):
    n = seq // dil
    tq = n if n <= 2 * DIL_TILE else DIL_TILE
    window = min(tq + DIL_WINDOW_STEPS, n)
    view = lambda a: a.reshape(batch, n, dil * GROUP_W)
    tiles = min(DIL_PROBLEMS, n // tq)
    classes = min(DIL_PROBLEMS // tiles, dil)
    qspec = pl.BlockSpec((1, tiles * tq, classes * GROUP_W), lambda b, r, i: (b, i, r))
    kspec = pl.BlockSpec((1, n, classes * GROUP_W), lambda b, r, i: (b, 0, r))
    shape = jax.ShapeDtypeStruct((batch, n, dil * GROUP_W), jnp.float32)
    o, lse = pl.pallas_call(
        functools.partial(_dilated_kernel, window=window, tq=tq),
        grid=(batch, dil // classes, n // (tiles * tq)),
        in_specs=[qspec, kspec, kspec], out_specs=[qspec, qspec], out_shape=[shape, shape],
        compiler_params=_params("arbitrary", "arbitrary", "arbitrary"), name=f"dilated_attention_d{dil}",
    )(view(q), view(k), view(v))
    return o.reshape(batch * n, dil * GROUP_W), lse.reshape(batch * n, dil * GROUP_W)


def _out_ffn(x2, omoba, ofox, omla, dil_outs, wout, gpost, gpre, wg, wu, wd, gpf):
    t = x2.shape[0]
    tm = ROW_TILE
    row = lambda w: pl.BlockSpec((tm, w), lambda i: (i, 0))
    (od1, l1), (od4, l4), (od16, l16) = dil_outs
    dil_specs = [pl.BlockSpec((tm // d, d * GROUP_W), lambda i: (i, 0)) for d in DILATIONS]
    in_specs = ([row(D_MODEL)] + [row(GROUP_W)] * 3 + dil_specs * 2
                + [_const_spec(wout.shape), _const_spec((1, D_MODEL)), _const_spec((1, D_MODEL)),
                   _const_spec(wg.shape), _const_spec(wu.shape), _const_spec(wd.shape), _const_spec((1, D_MODEL))])
    return pl.pallas_call(
        _out_ffn_kernel, grid=(t // tm,), in_specs=in_specs, out_specs=row(D_MODEL),
        out_shape=jax.ShapeDtypeStruct((t, D_MODEL), jnp.float32),
        scratch_shapes=[pltpu.VMEM((tm, D_MODEL), jnp.float32), pltpu.VMEM((4 * GROUP_W // 128, tm, 128), jnp.float32)],
        compiler_params=_params("arbitrary"), name="out_ffn",
    )(x2, omoba, ofox, omla, od1, od4, od16, l1, l4, l16, wout, gpost, gpre, wg, wu, wd, gpf)


def kernel(x, positions, w_in, b_forget, g_mla_q, w_mla_q_up, g_mla_kv, w_mla_kv_up, w_out, g_pre_mix, g_post_mix, w_gate, w_up, w_down, g_pre_ffn, g_post_ffn):
    batch, seq, _ = x.shape
    depth = w_in.shape[0]
    assert seq % ROW_TILE == 0 and seq % (DILATIONS[-1] * DIL_WINDOW_STEPS) == 0
    assert ROW_TILE == KEY_BLOCK
    tables = _rope_tables(positions)
    wtok, wtr, bfg, gq, wqT, gkv, wkn, wvT, wout, wg, wu, wd = _prep_layer_weights(
        w_in, b_forget, g_mla_q, w_mla_q_up, g_mla_kv, w_mla_kv_up, w_out, w_gate, w_up, w_down)
    tri = jnp.tril(jnp.ones((ROW_TILE, ROW_TILE), jnp.bfloat16))
    x2 = x.reshape(batch * seq, D_MODEL)
    for l in range(depth):
        (k_moba, kmean, qT_moba, vT_moba, k_fox, qT_fox, vT_fox, cbase, *dil_qkv,
         k_mla, qT_mla, vT_mla) = _in_proj(x2, g_pre_mix[l][None], wtok[l], wtr[l], tri, tables, bfg[l], gq[l],
                                          wqT[l], gkv[l], wkn[l], wvT[l], seq)
        nd = len(DILATIONS)
        o_moba = _moba_attention(qT_moba, k_moba, vT_moba, kmean.reshape(batch, seq // MOBA_BLOCK, GROUP_W),
                                 batch, seq)
        o_fox = _dense_attention(_fox_kernel, "fox_attention", qT_fox, k_fox, vT_fox, [cbase], batch, seq)
        o_mla = _dense_attention(_mla_kernel, "mla_attention", qT_mla, k_mla, vT_mla, [], batch, seq)
        dil_outs = [_dilated_attention(dil_qkv[n], dil_qkv[nd + n], dil_qkv[2 * nd + n], batch, seq, d)
                    for n, d in enumerate(DILATIONS)]
        x2 = _out_ffn(x2, o_moba, o_fox, o_mla, dil_outs, wout[l], g_post_mix[l][None], g_pre_ffn[l][None],
                      wg[l], wu[l], wd[l], g_post_ffn[l][None])
    return x2.reshape(batch, seq, D_MODEL)
```
